```python
import math
import jax, jax.numpy as jnp
from jax import lax
import numpy as np

D_MODEL = 1024
BATCH = 8
SEQ = 4096
DEPTH = 4

EPS = 1e-6
ROPE_THETA = 10000.0
NEG_INF = -1e30
HEAD_DIM = 64
N_Q_HEADS = 8
N_KV_HEADS = 2
GQA_GROUP = N_Q_HEADS // N_KV_HEADS
WINDOW = 128
BLOCK = 128
ATTN_WIDTH = N_Q_HEADS * HEAD_DIM
KV_WIDTH = N_KV_HEADS * HEAD_DIM
SSM_CH = 16
SSM_GROUPS = 32
SSM_STATE = 64
SSM_WIDTH = SSM_GROUPS * SSM_CH
N_MEM = 256
X_HEADS = 4
X_HEAD_DIM = 128
X_WIDTH = X_HEADS * X_HEAD_DIM

MIX_WIDTH = ATTN_WIDTH + SSM_WIDTH + X_WIDTH
SPLIT_SIZES = (ATTN_WIDTH, KV_WIDTH, KV_WIDTH, ATTN_WIDTH, SSM_WIDTH, SSM_WIDTH, X_WIDTH, X_WIDTH)
IN_WIDTH = sum(SPLIT_SIZES)
SPLIT_POINTS = tuple(int(v) for v in np.cumsum(SPLIT_SIZES)[:-1])

kernel_name = "hymba_style_swa_s5_memxattn_trunk"


def _rms(x, g):
    xf = x.astype(jnp.float32)
    y = xf * lax.rsqrt(jnp.mean(xf * xf, axis=-1, keepdims=True) + EPS)
    return (y * g.astype(jnp.float32)).astype(x.dtype)


def _rope(x, pos):
    half = x.shape[-1] // 2
    inv = ROPE_THETA ** (-jnp.arange(half, dtype=jnp.float32) / half)
    ang = pos.astype(jnp.float32)[..., None] * inv
    cos = jnp.cos(ang)[:, :, None, :]
    sin = jnp.sin(ang)[:, :, None, :]
    xf = x.astype(jnp.float32)
    x1, x2 = xf[..., :half], xf[..., half:]
    out = jnp.concatenate([x1 * cos - x2 * sin, x2 * cos + x1 * sin], axis=-1)
    return out.astype(x.dtype)


def _sliding_window_attn(q, k, v, sinks):
    b, s = q.shape[0], q.shape[1]
    nb = s // BLOCK
    qb = q.reshape(b, nb, BLOCK, N_KV_HEADS, GQA_GROUP, HEAD_DIM)
    kb = k.reshape(b, nb, BLOCK, N_KV_HEADS, HEAD_DIM)
    vb = v.reshape(b, nb, BLOCK, N_KV_HEADS, HEAD_DIM)
    zk = jnp.zeros_like(kb[:, :1])
    kk = jnp.concatenate([jnp.concatenate([zk, kb[:, :-1]], axis=1), kb], axis=2)
    vv = jnp.concatenate([jnp.concatenate([zk, vb[:, :-1]], axis=1), vb], axis=2)
    scale = 1.0 / math.sqrt(HEAD_DIM)
    sc = jnp.einsum('bnqhgd,bnkhd->bnhgqk', qb, kk).astype(jnp.float32) * scale
    i = jnp.arange(BLOCK)[:, None]
    j = jnp.arange(2 * BLOCK)[None, :]
    band = (j >= i + BLOCK - WINDOW + 1) & (j <= i + BLOCK)
    first = (jnp.arange(nb) > 0)[:, None, None] | (j >= BLOCK)[None]
    valid = band[None] & first
    sc = jnp.where(valid[None, :, None, None], sc, NEG_INF)
    sink = sinks.astype(jnp.float32).reshape(N_KV_HEADS, GQA_GROUP)[None, None, :, :, None, None]
    m = jnp.maximum(jnp.max(sc, axis=-1, keepdims=True), sink)
    p = jnp.exp(sc - m)
    p = p / (jnp.sum(p, axis=-1, keepdims=True) + jnp.exp(sink - m))
    o = jnp.einsum('bnhgqk,bnkhd->bnqhgd', p.astype(v.dtype), vv)
    return o.reshape(b, s, ATTN_WIDTH)


def _ssm_scan_combine(e1, e2):
    a1r, a1i, b1r, b1i = e1
    a2r, a2i, b2r, b2i = e2
    return (a1r * a2r - a1i * a2i,
            a1r * a2i + a1i * a2r,
            a2r * b1r - a2i * b1i + b2r,
            a2r * b1i + a2i * b1r + b2i)


def _s5(u, lam_re, lam_im, log_dt, b_re, b_im, c_re, c_im, d_skip, w_glu, b_glu):
    b, s = u.shape[0], u.shape[1]
    f32 = jnp.float32
    uf = u.astype(f32).reshape(b, s, SSM_GROUPS, SSM_CH)
    lr, li = lam_re.astype(f32), lam_im.astype(f32)
    dt = jnp.exp(log_dt.astype(f32))[:, None]
    mag = jnp.exp(lr * dt)
    ar, ai = mag * jnp.cos(li * dt), mag * jnp.sin(li * dt)
    den = lr * lr + li * li
    fr = ((ar - 1.0) * lr + ai * li) / den
    fi = (ai * lr - (ar - 1.0) * li) / den
    br, bi = b_re.astype(f32), b_im.astype(f32)
    bbr = fr[..., None] * br - fi[..., None] * bi
    bbi = fr[..., None] * bi + fi[..., None] * br
    xr = jnp.einsum('bsgc,gpc->bsgp', uf, bbr)
    xi = jnp.einsum('bsgc,gpc->bsgp', uf, bbi)
    a_r = jnp.broadcast_to(ar[None, None], (1, s, SSM_GROUPS, SSM_STATE))
    a_i = jnp.broadcast_to(ai[None, None], (1, s, SSM_GROUPS, SSM_STATE))
    _, _, hr, hi = lax.associative_scan(_ssm_scan_combine, (a_r, a_i, xr, xi), axis=1)
    y = (jnp.einsum('bsgp,gcp->bsgc', hr, c_re.astype(f32))
         - jnp.einsum('bsgp,gcp->bsgc', hi, c_im.astype(f32)))
    y = (y + d_skip.astype(f32).reshape(SSM_GROUPS, SSM_CH) * uf).reshape(b, s, SSM_WIDTH)
    y = jax.nn.gelu(y)
    y = y * jax.nn.sigmoid(y @ w_glu.astype(f32) + b_glu.astype(f32))
    return y.astype(u.dtype)


def _mem_cross_attn(q, k, v):
    b, s = q.shape[0], q.shape[1]
    sc = jnp.einsum('bshd,bmhd->bhsm', q, k).astype(jnp.float32) / math.sqrt(X_HEAD_DIM)
    p = jax.nn.softmax(sc, axis=-1)
    o = jnp.einsum('bhsm,bmhd->bshd', p.astype(v.dtype), v)
    return o.reshape(b, s, X_WIDTH)


def setup_inputs(seed: int = 0) -> dict:
    key = jax.random.key(seed)
    ks = jax.random.split(key, 24)
    f32 = jnp.float32
    nrm = lambda k, shape, std: jax.random.normal(k, shape, f32) * std
    x = nrm(ks[0], (BATCH, SEQ, D_MODEL), 1.0)
    mem = nrm(ks[1], (BATCH, N_MEM, D_MODEL), 1.0)
    offset = jax.random.randint(ks[2], (BATCH, 1), 0, 4096, dtype=jnp.int32)
    positions = offset + jnp.arange(SEQ, dtype=jnp.int32)[None, :]
    norm_g = 1.0 + nrm(ks[3], (DEPTH, D_MODEL), 0.02)
    w_in = nrm(ks[4], (DEPTH, D_MODEL, IN_WIDTH), D_MODEL ** -0.5)
    q_norm_g = 1.0 + nrm(ks[5], (DEPTH, HEAD_DIM), 0.02)
    k_norm_g = 1.0 + nrm(ks[6], (DEPTH, HEAD_DIM), 0.02)
    sinks = nrm(ks[7], (DEPTH, N_Q_HEADS), 0.5)
    lam_re = -0.5 + nrm(ks[8], (DEPTH, SSM_GROUPS, SSM_STATE), 0.01)
    lam_im = (math.pi * jnp.arange(SSM_STATE, dtype=f32))[None, None, :] + nrm(ks[9], (DEPTH, SSM_GROUPS, SSM_STATE), 0.01)
    log_dt = jax.random.uniform(ks[10], (DEPTH, SSM_GROUPS), f32, math.log(1e-3), math.log(1e-1))
    b_re = nrm(ks[11], (DEPTH, SSM_GROUPS, SSM_STATE, SSM_CH), (2.0 * SSM_CH) ** -0.5)
    b_im = nrm(ks[12], (DEPTH, SSM_GROUPS, SSM_STATE, SSM_CH), (2.0 * SSM_CH) ** -0.5)
    c_re = nrm(ks[13], (DEPTH, SSM_GROUPS, SSM_CH, SSM_STATE), (2.0 * SSM_STATE) ** -0.5)
    c_im = nrm(ks[14], (DEPTH, SSM_GROUPS, SSM_CH, SSM_STATE), (2.0 * SSM_STATE) ** -0.5)
    d_skip = nrm(ks[15], (DEPTH, SSM_WIDTH), 1.0)
    w_glu = nrm(ks[16], (DEPTH, SSM_WIDTH, SSM_WIDTH), SSM_WIDTH ** -0.5)
    b_glu = nrm(ks[17], (DEPTH, SSM_WIDTH), 0.02)
    mem_norm_g = 1.0 + nrm(ks[18], (DEPTH, D_MODEL), 0.02)
    w_mem_kv = nrm(ks[19], (DEPTH, D_MODEL, 2 * X_WIDTH), D_MODEL ** -0.5)
    xq_norm_g = 1.0 + nrm(ks[20], (DEPTH, X_HEAD_DIM), 0.02)
    xk_norm_g = 1.0 + nrm(ks[21], (DEPTH, X_HEAD_DIM), 0.02)
    w_out = nrm(ks[22], (DEPTH, MIX_WIDTH, D_MODEL), (MIX_WIDTH * 2.0 * DEPTH) ** -0.5)
    return {"x": x, "mem": mem, "positions": positions, "norm_g": norm_g, "w_in": w_in,
            "q_norm_g": q_norm_g, "k_norm_g": k_norm_g, "sinks": sinks,
            "lam_re": lam_re, "lam_im": lam_im, "log_dt": log_dt,
            "b_re": b_re, "b_im": b_im, "c_re": c_re, "c_im": c_im, "d_skip": d_skip,
            "w_glu": w_glu, "b_glu": b_glu, "mem_norm_g": mem_norm_g, "w_mem_kv": w_mem_kv,
            "xq_norm_g": xq_norm_g, "xk_norm_g": xk_norm_g, "w_out": w_out}


def reference(x, mem, positions, norm_g, w_in, q_norm_g, k_norm_g, sinks, lam_re, lam_im, log_dt,
              b_re, b_im, c_re, c_im, d_skip, w_glu, b_glu, mem_norm_g, w_mem_kv,
              xq_norm_g, xk_norm_g, w_out):
    b, s = x.shape[0], x.shape[1]
    for l in range(DEPTH):
        h = _rms(x, norm_g[l])
        z = h @ w_in[l]
        aq, ak, av, ag, su, sg, xq, xg = jnp.split(z, SPLIT_POINTS, axis=-1)
        aq = _rope(_rms(aq.reshape(b, s, N_Q_HEADS, HEAD_DIM), q_norm_g[l]), positions)
        ak = _rope(_rms(ak.reshape(b, s, N_KV_HEADS, HEAD_DIM), k_norm_g[l]), positions)
        av = av.reshape(b, s, N_KV_HEADS, HEAD_DIM)
        out_a = _sliding_window_attn(aq, ak, av, sinks[l]) * jax.nn.silu(ag)
        out_b = _s5(su, lam_re[l], lam_im[l], log_dt[l], b_re[l], b_im[l], c_re[l], c_im[l],
                    d_skip[l], w_glu[l], b_glu[l]) * jax.nn.silu(sg)
        mkv = _rms(mem, mem_norm_g[l]) @ w_mem_kv[l]
        mk, mv = jnp.split(mkv, 2, axis=-1)
        mk = _rms(mk.reshape(b, N_MEM, X_HEADS, X_HEAD_DIM), xk_norm_g[l])
        mv = mv.reshape(b, N_MEM, X_HEADS, X_HEAD_DIM)
        xq = _rms(xq.reshape(b, s, X_HEADS, X_HEAD_DIM), xq_norm_g[l])
        out_c = _mem_cross_attn(xq, mk, mv) * jax.nn.silu(xg)
        y = jnp.concatenate([out_a, out_b, out_c], axis=-1) @ w_out[l]
        x = x + y.astype(x.dtype)
    return x
```

```python
import functools
import math

import jax
import jax.numpy as jnp
import numpy as np
from jax import lax
from jax.experimental import pallas as pl
from jax.experimental.pallas import tpu as pltpu

F32 = jnp.float32
BF16 = jnp.bfloat16

EPS = 1e-6
ROPE_THETA = 10000.0
NEG_INF = -1e30

HEAD_DIM = 64
N_Q_HEADS = 8
N_KV_HEADS = 2
BLOCK = 128
ATTN_WIDTH = N_Q_HEADS * HEAD_DIM
KV_WIDTH = N_KV_HEADS * HEAD_DIM
SSM_CH = 16
SSM_GROUPS = 32
SSM_STATE = 64
SSM_WIDTH = SSM_GROUPS * SSM_CH
X_HEADS = 4
X_HEAD_DIM = 128
X_WIDTH = X_HEADS * X_HEAD_DIM

LANES = 128
CHUNK = 16
CK = CHUNK * SSM_CH
TQ = 512
VMEM_LIMIT = 56 * 1024 * 1024

_OFF_Q = 0
_OFF_K = _OFF_Q + ATTN_WIDTH
_OFF_V = _OFF_K + KV_WIDTH
_OFF_AG = _OFF_V + KV_WIDTH
_OFF_XQ = _OFF_AG + ATTN_WIDTH
_OFF_XG = _OFF_XQ + X_WIDTH
AC_WIDTH = _OFF_XG + X_WIDTH


def _sigmoid(v):
    return 1.0 / (1.0 + jnp.exp(-v))


def _silu(v):
    return v * _sigmoid(v)


def _rms_rows(v, gain):
    return v * lax.rsqrt(jnp.mean(v * v, axis=-1, keepdims=True) + EPS) * gain


def _dot_nt(a, b):
    return lax.dot_general(a, b, (((1,), (1,)), ((), ())), preferred_element_type=F32)


def _rope_kernel(pos_ref, inv_ref, cos_ref, sin_ref):
    ang = pos_ref[0].astype(F32) * inv_ref[...]
    lane = lax.broadcasted_iota(jnp.int32, ang.shape, 1)
    first_half = (lane % HEAD_DIM) < (HEAD_DIM // 2)
    cos_ref[0] = jnp.cos(ang)
    sin_ref[0] = jnp.where(first_half, -jnp.sin(ang), jnp.sin(ang))


def _rope_tables(positions):
    b, s = positions.shape
    half = HEAD_DIM // 2
    inv = ROPE_THETA ** (-jnp.arange(half, dtype=F32) / half)
    inv = jnp.tile(inv, LANES // half).reshape(1, LANES)
    ts = min(s, 1024)
    spec = pl.BlockSpec((1, ts, LANES), lambda i, j: (i, j, 0))
    return pl.pallas_call(
        _rope_kernel,
        out_shape=(jax.ShapeDtypeStruct((b, s, LANES), F32),) * 2,
        grid=(b, s // ts),
        in_specs=[pl.BlockSpec((1, ts, 1), lambda i, j: (i, j, 0)),
                  pl.BlockSpec((1, LANES), lambda i, j: (0, 0))],
        out_specs=(spec, spec),
        name="rope_tables",
    )(positions.reshape(b, s, 1), inv)


def _memkv_kernel(mem_ref, g_ref, w_ref, gk_ref, mk_ref, mv_ref):
    h = _rms_rows(mem_ref[0], g_ref[0]).astype(BF16)
    kv = jnp.dot(h, w_ref[0], preferred_element_type=F32)
    for hd in range(X_HEADS):
        sl = slice(hd * X_HEAD_DIM, (hd + 1) * X_HEAD_DIM)
        mk_ref[0, 0, :, sl] = _rms_rows(kv[:, sl], gk_ref[0]).astype(BF16)
    mv_ref[0, 0] = kv[:, X_WIDTH:].astype(BF16)


def _mem_kv(mem, mem_norm_g, w_mem_kv, xk_norm_g):
    b, n_mem, d = mem.shape
    depth = w_mem_kv.shape[0]
    out = jax.ShapeDtypeStruct((depth, b, n_mem, X_WIDTH), BF16)
    ospec = pl.BlockSpec((1, 1, n_mem, X_WIDTH), lambda l, i: (l, i, 0, 0))
    return pl.pallas_call(
        _memkv_kernel,
        out_shape=(out, out),
        grid=(depth, b),
        in_specs=[pl.BlockSpec((1, n_mem, d), lambda l, i: (i, 0, 0)),
                  pl.BlockSpec((1, 1, d), lambda l, i: (l, 0, 0)),
                  pl.BlockSpec((1, d, 2 * X_WIDTH), lambda l, i: (l, 0, 0)),
                  pl.BlockSpec((1, 1, X_HEAD_DIM), lambda l, i: (l, 0, 0))],
        out_specs=(ospec, ospec),
        name="mem_kv",
    )(mem, mem_norm_g.reshape(depth, 1, d), w_mem_kv.astype(BF16), xk_norm_g.reshape(depth, 1, X_HEAD_DIM))


def _s5_prep_kernel(lrc_ref, lic_ref, lrr_ref, lir_ref, ldt_ref, brt_ref, bit_ref, crt_ref, cit_ref,
                    wtws_ref, wcr_ref, wci_ref, pw_ref, *, n_scan):
    dt = jnp.exp(ldt_ref[0, 0])
    lrc, lic = lrc_ref[0, 0], lic_ref[0, 0]
    lrr, lir = lrr_ref[0, 0], lir_ref[0, 0]

    def power(lr, li, n):
        mag = jnp.exp(lr * dt * n)
        return mag * jnp.cos(li * dt * n), mag * jnp.sin(li * dt * n)

    ar, ai = power(lrc, lic, 1.0)
    den = lrc * lrc + lic * lic
    fr = ((ar - 1.0) * lrc + ai * lic) / den
    fi = (ai * lrc - (ar - 1.0) * lic) / den
    brt, bit = brt_ref[0, 0], bit_ref[0, 0]
    bbr = fr * brt - fi * bit
    bbi = fr * bit + fi * brt

    crt, cit = crt_ref[0, 0], cit_ref[0, 0]
    t_row = (lax.broadcasted_iota(jnp.int32, crt.shape, 0) // SSM_CH).astype(F32)
    pr, pi = power(lrr, lir, t_row)
    gr = crt * pr - cit * pi
    gi = crt * pi + cit * pr
    hi = lax.Precision.HIGHEST
    d = (jnp.dot(gr, bbr, precision=hi, preferred_element_type=F32)
         - jnp.dot(gi, bbi, precision=hi, preferred_element_type=F32))
    lane_s = lax.broadcasted_iota(jnp.int32, d.shape, 1) // SSM_CH
    wt = jnp.where(lane_s == 0, d, 0.0)
    for s in range(1, CHUNK):
        shifted = jnp.concatenate([jnp.zeros((s * SSM_CH, CK), F32), d[:CK - s * SSM_CH]], axis=0)
        wt = jnp.where(lane_s == s, shifted, wt)
    wtws_ref[0, 0, 0:CK, :] = wt.astype(BF16)

    n_lane = ((CHUNK - 1) - lax.broadcasted_iota(jnp.int32, brt.shape, 1) // SSM_CH).astype(F32)
    qr, qi = power(lrc, lic, n_lane)
    wtws_ref[0, 0, CK:CK + SSM_STATE, :] = (qr * bbr - qi * bbi).astype(BF16)
    wtws_ref[0, 0, CK + SSM_STATE:, :] = (qr * bbi + qi * bbr).astype(BF16)

    p1r, p1i = power(lrr, lir, t_row + 1.0)
    wcr_ref[0, 0] = (crt * p1r - cit * p1i).astype(BF16)
    wci_ref[0, 0] = (-(crt * p1i + cit * p1r)).astype(BF16)

    lane_j = jnp.minimum(lax.broadcasted_iota(jnp.int32, (SSM_STATE, LANES), 1), n_scan - 1)
    n_pw = jnp.left_shift(jnp.int32(CHUNK), lane_j).astype(F32)
    wr, wi = power(lrc, lic, n_pw)
    pw_ref[0, 0, 0:SSM_STATE, :] = wr
    pw_ref[0, 0, SSM_STATE:, :] = wi


def _s5_prep(lam_re, lam_im, log_dt, b_re, b_im, c_re, c_im, n_scan):
    depth, g, p = lam_re.shape
    col = lambda a: a.reshape(depth, g, p, 1)
    row = lambda a: a.reshape(depth, g, 1, p)
    tile_b = lambda a: jnp.tile(a, (1, 1, 1, CHUNK))
    tile_c = lambda a: jnp.tile(a, (1, 1, CHUNK, 1))
    spec4 = lambda r, c: pl.BlockSpec((1, 1, r, c), lambda l, i: (l, i, 0, 0))
    return pl.pallas_call(
        functools.partial(_s5_prep_kernel, n_scan=n_scan),
        out_shape=(jax.ShapeDtypeStruct((depth, g, CK + 2 * p, CK), BF16),
                   jax.ShapeDtypeStruct((depth, g, CK, p), BF16),
                   jax.ShapeDtypeStruct((depth, g, CK, p), BF16),
                   jax.ShapeDtypeStruct((depth, g, 2 * p, LANES), F32)),
        grid=(depth, g),
        in_specs=[spec4(p, 1), spec4(p, 1), spec4(1, p), spec4(1, p), spec4(1, 1),
                  spec4(p, CK), spec4(p, CK), spec4(CK, p), spec4(CK, p)],
        out_specs=(spec4(CK + 2 * p, CK), spec4(CK, p), spec4(CK, p), spec4(2 * p, LANES)),
        name="s5_prep",
    )(col(lam_re), col(lam_im), row(lam_re), row(lam_im), log_dt.reshape(depth, g, 1, 1),
      tile_b(b_re), tile_b(b_im), tile_c(c_re), tile_c(c_im))


def _ac_kernel(sinks_ref, x_ref, cos_ref, sin_ref, ng_ref, win_ref, gq_ref, gk_ref, mk_ref, mv_ref,
               gxq_ref, wout_ref, out_ref, z_ref, q_ref, k_ref, v_ref, mix_ref, *, layer):
    i = pl.program_id(1)
    x = x_ref[0]
    h = _rms_rows(x, ng_ref[0]).astype(BF16)
    z_ref[...] = jnp.dot(h, win_ref[0], preferred_element_type=F32)

    @pl.when(i == 0)
    def _():
        k_ref[0:BLOCK] = jnp.zeros((BLOCK, LANES), BF16)
        v_ref[0:BLOCK] = jnp.zeros((BLOCK, LANES), BF16)

    cos_t, sin_t = cos_ref[0], sin_ref[0]
    lane = lax.broadcasted_iota(jnp.int32, (TQ, LANES), 1)
    low_head = lane < HEAD_DIM
    first_half = (lane % HEAD_DIM) < (HEAD_DIM // 2)

    def head_norm_rope(v, gain):
        sq = v * v
        s_lo = jnp.sum(jnp.where(low_head, sq, 0.0), axis=-1, keepdims=True)
        s_hi = jnp.sum(jnp.where(low_head, 0.0, sq), axis=-1, keepdims=True)
        inv = jnp.where(low_head, lax.rsqrt(s_lo / HEAD_DIM + EPS), lax.rsqrt(s_hi / HEAD_DIM + EPS))
        vn = v * inv * gain
        swapped = jnp.where(first_half, pltpu.roll(vn, LANES - HEAD_DIM // 2, 1), pltpu.roll(vn, HEAD_DIM // 2, 1))
        return vn * cos_t + swapped * sin_t

    k_ref[BLOCK:] = head_norm_rope(z_ref[:, _OFF_K:_OFF_K + KV_WIDTH], gk_ref[0]).astype(BF16)
    v_ref[BLOCK:] = z_ref[:, _OFF_V:_OFF_V + KV_WIDTH].astype(BF16)
    scale = 1.0 / math.sqrt(HEAD_DIM)
    for j in range(ATTN_WIDTH // LANES):
        sl = slice(_OFF_Q + j * LANES, _OFF_Q + (j + 1) * LANES)
        q_ref[:, j * LANES:(j + 1) * LANES] = (head_norm_rope(z_ref[:, sl], gq_ref[0]) * scale).astype(BF16)

    row = lax.broadcasted_iota(jnp.int32, (2 * BLOCK, 2 * BLOCK), 0)
    col = lax.broadcasted_iota(jnp.int32, (2 * BLOCK, 2 * BLOCK), 1)
    qi = row % BLOCK
    low_rows = lax.broadcasted_iota(jnp.int32, (2 * BLOCK, 1), 0) < BLOCK
    low_lane_b = lax.broadcasted_iota(jnp.int32, (BLOCK, LANES), 1) < HEAD_DIM
    no_prev = jnp.where(i == 0, BLOCK, 0)
    for r in range(TQ // BLOCK):
        keys = k_ref[r * BLOCK:(r + 2) * BLOCK]
        vals = v_ref[r * BLOCK:(r + 2) * BLOCK]
        lower = jnp.maximum(qi + 1, no_prev) if r == 0 else qi + 1
        valid = (col >= lower) & (col <= qi + BLOCK)
        for j in range(ATTN_WIDTH // LANES):
            qb = q_ref[r * BLOCK:(r + 1) * BLOCK, j * LANES:(j + 1) * LANES]
            zero = jnp.zeros_like(qb)
            lhs = jnp.concatenate([jnp.where(low_lane_b, qb, zero), jnp.where(low_lane_b, zero, qb)], axis=0)
            sc = jnp.where(valid, _dot_nt(lhs, keys), NEG_INF)
            sink = jnp.where(low_rows, sinks_ref[layer, j], sinks_ref[layer, j + N_Q_HEADS // 2])
            m = jnp.maximum(jnp.max(sc, axis=-1, keepdims=True), sink)
            p = jnp.exp(sc - m)
            den = jnp.sum(p, axis=-1, keepdims=True) + jnp.exp(sink - m)
            o = jnp.dot(p.astype(BF16), vals, preferred_element_type=F32) / den
            oa = jnp.where(low_lane_b, o[:BLOCK], o[BLOCK:])
            gate = z_ref[r * BLOCK:(r + 1) * BLOCK, _OFF_AG + j * LANES:_OFF_AG + (j + 1) * LANES]
            mix_ref[r * BLOCK:(r + 1) * BLOCK, j * LANES:(j + 1) * LANES] = (oa * _silu(gate)).astype(BF16)

    k_ref[0:BLOCK] = k_ref[TQ:TQ + BLOCK]
    v_ref[0:BLOCK] = v_ref[TQ:TQ + BLOCK]

    xscale = 1.0 / math.sqrt(X_HEAD_DIM)
    for hd in range(X_HEADS):
        sl = slice(hd * X_HEAD_DIM, (hd + 1) * X_HEAD_DIM)
        xq = _rms_rows(z_ref[:, _OFF_XQ + hd * X_HEAD_DIM:_OFF_XQ + (hd + 1) * X_HEAD_DIM], gxq_ref[0]).astype(BF16)
        sc = _dot_nt(xq, mk_ref[0, 0, :, sl]) * xscale
        p = jnp.exp(sc - jnp.max(sc, axis=-1, keepdims=True))
        den = jnp.sum(p, axis=-1, keepdims=True)
        o = jnp.dot(p.astype(BF16), mv_ref[0, 0, :, sl], preferred_element_type=F32) / den
        gate = z_ref[:, _OFF_XG + hd * X_HEAD_DIM:_OFF_XG + (hd + 1) * X_HEAD_DIM]
        mix_ref[:, ATTN_WIDTH + hd * X_HEAD_DIM:ATTN_WIDTH + (hd + 1) * X_HEAD_DIM] = (o * _silu(gate)).astype(BF16)

    out_ref[0] = x + jnp.dot(mix_ref[...], wout_ref[0], preferred_element_type=F32)


def _ac_layer(layer, x, cos_t, sin_t, sinks, norm_g, w_in_ac, gq, gk, mk, mv, gxq, w_out_ac):
    b, s, d = x.shape
    n_mem = mk.shape[2]
    tok = lambda w: pl.BlockSpec((1, TQ, w), lambda bi, i: (bi, i, 0))
    per_layer = lambda *shape: pl.BlockSpec((1,) + shape, lambda bi, i: (layer,) + (0,) * len(shape))
    memspec = pl.BlockSpec((1, 1, n_mem, X_WIDTH), lambda bi, i: (layer, bi, 0, 0))
    return pl.pallas_call(
        functools.partial(_ac_kernel, layer=layer),
        out_shape=jax.ShapeDtypeStruct((b, s, d), F32),
        grid=(b, s // TQ),
        in_specs=[pl.BlockSpec(memory_space=pltpu.SMEM),
                  tok(d), tok(LANES), tok(LANES),
                  per_layer(1, d), per_layer(d, AC_WIDTH), per_layer(1, LANES), per_layer(1, LANES),
                  memspec, memspec, per_layer(1, X_HEAD_DIM), per_layer(ATTN_WIDTH + X_WIDTH, d)],
        out_specs=tok(d),
        scratch_shapes=[pltpu.VMEM((TQ, AC_WIDTH), F32),
                        pltpu.VMEM((TQ, ATTN_WIDTH), BF16),
                        pltpu.VMEM((BLOCK + TQ, LANES), BF16),
                        pltpu.VMEM((BLOCK + TQ, LANES), BF16),
                        pltpu.VMEM((TQ, ATTN_WIDTH + X_WIDTH), BF16)],
        compiler_params=pltpu.CompilerParams(dimension_semantics=("arbitrary", "arbitrary"),
                                             vmem_limit_bytes=VMEM_LIMIT),
        name=f"attn_layer{layer}",
    )(sinks, x, cos_t, sin_t, norm_g, w_in_ac, gq, gk, mk, mv, gxq, w_out_ac)


def _ssm_kernel(x_ref, o1_ref, ng_ref, winb_ref, wtws_ref, wc_ref, pw_ref, dsk_ref, wglu_ref, bglu_ref,
                woutb_ref, out_ref, z_ref, y_ref, sg_ref, *, n_chunk, n_scan):
    step = pl.program_id(1)

    @pl.when(step < CHUNK)
    def _project():
        h = _rms_rows(x_ref[0], ng_ref[0]).astype(BF16)
        ut = _dot_nt(winb_ref[0], h)
        u = ut[:SSM_WIDTH]
        z_ref[step] = u.astype(BF16).reshape(SSM_GROUPS, SSM_CH, n_chunk)
        y_ref[step] = (u * dsk_ref[0]).reshape(SSM_GROUPS, SSM_CH, n_chunk)
        sg_ref[step] = _silu(ut[SSM_WIDTH:])

    @pl.when(step == CHUNK - 1)
    def _chunks():
        lane = lax.broadcasted_iota(jnp.int32, (SSM_STATE, n_chunk), 1)

        def shift(v, k):
            return jnp.where(lane >= k, pltpu.roll(v, k, 1), 0.0)

        def group(g, carry):
            zg = z_ref[:, g].reshape(CK, n_chunk)
            r = jnp.dot(wtws_ref[0, g], zg, preferred_element_type=F32)
            hr, hi = r[CK:CK + SSM_STATE], r[CK + SSM_STATE:]
            pw = pw_ref[0, g]
            for j in range(n_scan):
                ar, ai = pw[:SSM_STATE, j:j + 1], pw[SSM_STATE:, j:j + 1]
                sr, si = shift(hr, 1 << j), shift(hi, 1 << j)
                hr, hi = hr + ar * sr - ai * si, hi + ar * si + ai * sr
            hprev = jnp.concatenate([shift(hr, 1), shift(hi, 1)], axis=0).astype(BF16)
            yg = r[:CK] + jnp.dot(wc_ref[0, g], hprev, preferred_element_type=F32)
            y_ref[:, g] = y_ref[:, g] + yg.reshape(CHUNK, SSM_CH, n_chunk)
            return carry

        lax.fori_loop(0, SSM_GROUPS, group, 0)

    @pl.when(step >= CHUNK)
    def _finish():
        t = step - CHUNK
        y = jax.nn.gelu(y_ref[t].reshape(SSM_WIDTH, n_chunk))
        gate = _sigmoid(jnp.dot(wglu_ref[0], y.astype(BF16), preferred_element_type=F32) + bglu_ref[0])
        ob = (y * gate * sg_ref[t]).T.astype(BF16)
        out_ref[0] = o1_ref[0] + jnp.dot(ob, woutb_ref[0], preferred_element_type=F32)


def _ssm_layer(layer, x, o1, norm_g, w_in_bt, wtws, wc, pw, d_skip, w_glu_t, b_glu, w_out_b, n_scan):
    b, s, d = x.shape
    n_chunk = s // CHUNK
    per_layer = lambda *shape: pl.BlockSpec((1,) + shape, lambda bi, st: (layer,) + (0,) * len(shape))
    x_spec = pl.BlockSpec((1, n_chunk, d), lambda bi, st: (bi, 0, jnp.minimum(st, CHUNK - 1)))
    o_spec = pl.BlockSpec((1, n_chunk, d), lambda bi, st: (bi, 0, jnp.maximum(st - CHUNK, 0)))
    out = pl.pallas_call(
        functools.partial(_ssm_kernel, n_chunk=n_chunk, n_scan=n_scan),
        out_shape=jax.ShapeDtypeStruct((b, n_chunk, CHUNK * d), F32),
        grid=(b, 2 * CHUNK),
        in_specs=[x_spec, o_spec,
                  per_layer(1, d), per_layer(2 * SSM_WIDTH, d),
                  per_layer(SSM_GROUPS, CK + 2 * SSM_STATE, CK), per_layer(SSM_GROUPS, CK, 2 * SSM_STATE),
                  per_layer(SSM_GROUPS, 2 * SSM_STATE, LANES),
                  per_layer(SSM_WIDTH, 1), per_layer(SSM_WIDTH, SSM_WIDTH), per_layer(SSM_WIDTH, 1),
                  per_layer(SSM_WIDTH, d)],
        out_specs=o_spec,
        scratch_shapes=[pltpu.VMEM((CHUNK, SSM_GROUPS, SSM_CH, n_chunk), BF16),
                        pltpu.VMEM((CHUNK, SSM_GROUPS, SSM_CH, n_chunk), F32),
                        pltpu.VMEM((CHUNK, SSM_WIDTH, n_chunk), F32)],
        compiler_params=pltpu.CompilerParams(dimension_semantics=("arbitrary", "arbitrary"),
                                             vmem_limit_bytes=VMEM_LIMIT),
        name=f"ssm_layer{layer}",
    )(x.reshape(b, n_chunk, CHUNK * d), o1.reshape(b, n_chunk, CHUNK * d), norm_g, w_in_bt, wtws, wc, pw,
      d_skip, w_glu_t, b_glu, w_out_b)
    return out.reshape(b, s, d)


def kernel(x, mem, positions, norm_g, w_in, q_norm_g, k_norm_g, sinks, lam_re, lam_im, log_dt, b_re, b_im,
           c_re, c_im, d_skip, w_glu, b_glu, mem_norm_g, w_mem_kv, xq_norm_g, xk_norm_g, w_out):
    b, s, d = x.shape
    depth = w_in.shape[0]
    assert s % TQ == 0 and (s // CHUNK) % LANES == 0
    n_scan = (s // CHUNK - 1).bit_length()

    pair = np.concatenate([np.r_[j * HEAD_DIM:(j + 1) * HEAD_DIM,
                                 (j + N_Q_HEADS // 2) * HEAD_DIM:(j + N_Q_HEADS // 2 + 1) * HEAD_DIM]
                           for j in range(N_Q_HEADS // 2)])
    o_k = ATTN_WIDTH
    o_v = o_k + KV_WIDTH
    o_ag = o_v + KV_WIDTH
    o_su = o_ag + ATTN_WIDTH
    o_sg = o_su + SSM_WIDTH
    o_xq = o_sg + SSM_WIDTH
    o_xg = o_xq + X_WIDTH
    w_in_ac = jnp.concatenate([w_in[:, :, :o_k][:, :, pair], w_in[:, :, o_k:o_ag],
                               w_in[:, :, o_ag:o_su][:, :, pair], w_in[:, :, o_xq:]], axis=-1).astype(BF16)
    w_in_bt = jnp.swapaxes(w_in[:, :, o_su:o_xq], 1, 2).astype(BF16)
    w_out_ac = jnp.concatenate([w_out[:, :ATTN_WIDTH][:, pair], w_out[:, ATTN_WIDTH + SSM_WIDTH:]],
                               axis=1).astype(BF16)
    w_out_b = w_out[:, ATTN_WIDTH:ATTN_WIDTH + SSM_WIDTH].astype(BF16)
    w_glu_t = jnp.swapaxes(w_glu, 1, 2).astype(BF16)
    norm_g3 = norm_g.reshape(depth, 1, d)
    gq = jnp.tile(q_norm_g, (1, LANES // HEAD_DIM)).reshape(depth, 1, LANES)
    gk = jnp.tile(k_norm_g, (1, LANES // HEAD_DIM)).reshape(depth, 1, LANES)
    gxq = xq_norm_g.reshape(depth, 1, X_HEAD_DIM)
    d_skip3 = d_skip.reshape(depth, SSM_WIDTH, 1)
    b_glu3 = b_glu.reshape(depth, SSM_WIDTH, 1)

    cos_t, sin_t = _rope_tables(positions)
    mk, mv = _mem_kv(mem, mem_norm_g, w_mem_kv, xk_norm_g)
    wtws, wcr, wci, pw = _s5_prep(lam_re, lam_im, log_dt, b_re, b_im, c_re, c_im, n_scan)
    wc = jnp.concatenate([wcr, wci], axis=-1)

    for layer in range(depth):
        o1 = _ac_layer(layer, x, cos_t, sin_t, sinks, norm_g3, w_in_ac, gq, gk, mk, mv, gxq, w_out_ac)
        x = _ssm_layer(layer, x, o1, norm_g3, w_in_bt, wtws, wc, pw, d_skip3, w_glu_t, b_glu3, w_out_b, n_scan)
    return x
```

```python
import functools
import math

import jax
import jax.numpy as jnp
import numpy as np
from jax import lax
from jax.experimental import pallas as pl
from jax.experimental.pallas import tpu as pltpu

F32 = jnp.float32
BF16 = jnp.bfloat16

EPS = 1e-6
ROPE_THETA = 10000.0
NEG_INF = -1e30

HEAD_DIM = 64
N_Q_HEADS = 8
N_KV_HEADS = 2
BLOCK = 128
ATTN_WIDTH = N_Q_HEADS * HEAD_DIM
KV_WIDTH = N_KV_HEADS * HEAD_DIM
SSM_CH = 16
SSM_GROUPS = 32
SSM_STATE = 64
SSM_WIDTH = SSM_GROUPS * SSM_CH
X_HEADS = 4
X_HEAD_DIM = 128
X_WIDTH = X_HEADS * X_HEAD_DIM

LANES = 128
CHUNK = 16
CK = CHUNK * SSM_CH
ROWS = BLOCK // CHUNK
TQ = 512
VMEM_LIMIT = 56 * 1024 * 1024

_OFF_Q = 0
_OFF_K = _OFF_Q + ATTN_WIDTH
_OFF_V = _OFF_K + KV_WIDTH
_OFF_AG = _OFF_V + KV_WIDTH
_OFF_XQ = _OFF_AG + ATTN_WIDTH
_OFF_XG = _OFF_XQ + X_WIDTH
AC_WIDTH = _OFF_XG + X_WIDTH


def _sigmoid(v):
    return 1.0 / (1.0 + jnp.exp(-v))


def _silu(v):
    return v * _sigmoid(v)


def _rms_rows(v, gain):
    return v * lax.rsqrt(jnp.mean(v * v, axis=-1, keepdims=True) + EPS) * gain


def _dot_nt(a, b):
    return lax.dot_general(a, b, (((1,), (1,)), ((), ())), preferred_element_type=F32)


def _rope_kernel(pos_ref, inv_ref, cos_ref, sin_ref):
    ang = pos_ref[0].astype(F32) * inv_ref[...]
    lane = lax.broadcasted_iota(jnp.int32, ang.shape, 1)
    first_half = (lane % HEAD_DIM) < (HEAD_DIM // 2)
    cos_ref[0] = jnp.cos(ang)
    sin_ref[0] = jnp.where(first_half, -jnp.sin(ang), jnp.sin(ang))


def _rope_tables(positions):
    b, s = positions.shape
    half = HEAD_DIM // 2
    inv = ROPE_THETA ** (-jnp.arange(half, dtype=F32) / half)
    inv = jnp.tile(inv, LANES // half).reshape(1, LANES)
    ts = min(s, 1024)
    spec = pl.BlockSpec((1, ts, LANES), lambda i, j: (i, j, 0))
    return pl.pallas_call(
        _rope_kernel,
        out_shape=(jax.ShapeDtypeStruct((b, s, LANES), F32),) * 2,
        grid=(b, s // ts),
        in_specs=[pl.BlockSpec((1, ts, 1), lambda i, j: (i, j, 0)),
                  pl.BlockSpec((1, LANES), lambda i, j: (0, 0))],
        out_specs=(spec, spec),
        name="rope_tables",
    )(positions.reshape(b, s, 1), inv)


def _memkv_kernel(mem_ref, g_ref, w_ref, gk_ref, mk_ref, mv_ref):
    h = _rms_rows(mem_ref[0], g_ref[0]).astype(BF16)
    kv = jnp.dot(h, w_ref[0], preferred_element_type=F32)
    for hd in range(X_HEADS):
        sl = slice(hd * X_HEAD_DIM, (hd + 1) * X_HEAD_DIM)
        mk_ref[0, 0, :, sl] = _rms_rows(kv[:, sl], gk_ref[0]).astype(BF16)
    mv_ref[0, 0] = kv[:, X_WIDTH:].astype(BF16)


def _mem_kv(mem, mem_norm_g, w_mem_kv, xk_norm_g):
    b, n_mem, d = mem.shape
    depth = w_mem_kv.shape[0]
    out = jax.ShapeDtypeStruct((depth, b, n_mem, X_WIDTH), BF16)
    ospec = pl.BlockSpec((1, 1, n_mem, X_WIDTH), lambda l, i: (l, i, 0, 0))
    return pl.pallas_call(
        _memkv_kernel,
        out_shape=(out, out),
        grid=(depth, b),
        in_specs=[pl.BlockSpec((1, n_mem, d), lambda l, i: (i, 0, 0)),
                  pl.BlockSpec((1, 1, d), lambda l, i: (l, 0, 0)),
                  pl.BlockSpec((1, d, 2 * X_WIDTH), lambda l, i: (l, 0, 0)),
                  pl.BlockSpec((1, 1, X_HEAD_DIM), lambda l, i: (l, 0, 0))],
        out_specs=(ospec, ospec),
        name="mem_kv",
    )(mem, mem_norm_g.reshape(depth, 1, d), w_mem_kv.astype(BF16), xk_norm_g.reshape(depth, 1, X_HEAD_DIM))


def _s5_prep_kernel(lrc_ref, lic_ref, lrr_ref, lir_ref, ldt_ref, brt_ref, bit_ref, crt_ref, cit_ref,
                    wtws_ref, wcr_ref, wci_ref, pw_ref, *, n_scan):
    dt = jnp.exp(ldt_ref[0, 0])
    lrc, lic = lrc_ref[0, 0], lic_ref[0, 0]
    lrr, lir = lrr_ref[0, 0], lir_ref[0, 0]

    def power(lr, li, n):
        mag = jnp.exp(lr * dt * n)
        return mag * jnp.cos(li * dt * n), mag * jnp.sin(li * dt * n)

    ar, ai = power(lrc, lic, 1.0)
    den = lrc * lrc + lic * lic
    fr = ((ar - 1.0) * lrc + ai * lic) / den
    fi = (ai * lrc - (ar - 1.0) * lic) / den
    brt, bit = brt_ref[0, 0], bit_ref[0, 0]
    bbr = fr * brt - fi * bit
    bbi = fr * bit + fi * brt

    crt, cit = crt_ref[0, 0], cit_ref[0, 0]
    t_row = (lax.broadcasted_iota(jnp.int32, crt.shape, 0) // SSM_CH).astype(F32)
    pr, pi = power(lrr, lir, t_row)
    gr = crt * pr - cit * pi
    gi = crt * pi + cit * pr
    hi = lax.Precision.HIGHEST
    d = (jnp.dot(gr, bbr, precision=hi, preferred_element_type=F32)
         - jnp.dot(gi, bbi, precision=hi, preferred_element_type=F32))
    lane_s = lax.broadcasted_iota(jnp.int32, d.shape, 1) // SSM_CH
    wt = jnp.where(lane_s == 0, d, 0.0)
    for s in range(1, CHUNK):
        shifted = jnp.concatenate([jnp.zeros((s * SSM_CH, CK), F32), d[:CK - s * SSM_CH]], axis=0)
        wt = jnp.where(lane_s == s, shifted, wt)
    wtws_ref[0, 0, 0:CK, :] = wt.astype(BF16)

    n_lane = ((CHUNK - 1) - lax.broadcasted_iota(jnp.int32, brt.shape, 1) // SSM_CH).astype(F32)
    qr, qi = power(lrc, lic, n_lane)
    wtws_ref[0, 0, CK:CK + SSM_STATE, :] = (qr * bbr - qi * bbi).astype(BF16)
    wtws_ref[0, 0, CK + SSM_STATE:, :] = (qr * bbi + qi * bbr).astype(BF16)

    p1r, p1i = power(lrr, lir, t_row + 1.0)
    wcr_ref[0, 0] = (crt * p1r - cit * p1i).astype(BF16)
    wci_ref[0, 0] = (-(crt * p1i + cit * p1r)).astype(BF16)

    lane_j = jnp.minimum(lax.broadcasted_iota(jnp.int32, (SSM_STATE, LANES), 1), n_scan - 1)
    n_pw = jnp.left_shift(jnp.int32(CHUNK), lane_j).astype(F32)
    wr, wi = power(lrc, lic, n_pw)
    pw_ref[0, 0, 0:SSM_STATE, :] = wr
    pw_ref[0, 0, SSM_STATE:, :] = wi


def _s5_prep(lam_re, lam_im, log_dt, b_re, b_im, c_re, c_im, n_scan):
    depth, g, p = lam_re.shape
    col = lambda a: a.reshape(depth, g, p, 1)
    row = lambda a: a.reshape(depth, g, 1, p)
    tile_b = lambda a: jnp.tile(a, (1, 1, 1, CHUNK))
    tile_c = lambda a: jnp.tile(a, (1, 1, CHUNK, 1))
    spec4 = lambda r, c: pl.BlockSpec((1, 1, r, c), lambda l, i: (l, i, 0, 0))
    return pl.pallas_call(
        functools.partial(_s5_prep_kernel, n_scan=n_scan),
        out_shape=(jax.ShapeDtypeStruct((depth, g, CK + 2 * p, CK), BF16),
                   jax.ShapeDtypeStruct((depth, g, CK, p), BF16),
                   jax.ShapeDtypeStruct((depth, g, CK, p), BF16),
                   jax.ShapeDtypeStruct((depth, g, 2 * p, LANES), F32)),
        grid=(depth, g),
        in_specs=[spec4(p, 1), spec4(p, 1), spec4(1, p), spec4(1, p), spec4(1, 1),
                  spec4(p, CK), spec4(p, CK), spec4(CK, p), spec4(CK, p)],
        out_specs=(spec4(CK + 2 * p, CK), spec4(CK, p), spec4(CK, p), spec4(2 * p, LANES)),
        name="s5_prep",
    )(col(lam_re), col(lam_im), row(lam_re), row(lam_im), log_dt.reshape(depth, g, 1, 1),
      tile_b(b_re), tile_b(b_im), tile_c(c_re), tile_c(c_im))


def _ac_kernel(sinks_ref, x_ref, cos_ref, sin_ref, ng_ref, win_ref, gq_ref, gk_ref, mk_ref, mv_ref,
               gxq_ref, wout_ref, out_ref, z_ref, q_ref, k_ref, v_ref, mix_ref, *, layer):
    i = pl.program_id(1)
    d_model = x_ref.shape[-1]
    x = jnp.concatenate([x_ref[0, :, r * ROWS:(r + 1) * ROWS, :].reshape(BLOCK, d_model)
                         for r in range(TQ // BLOCK)], axis=0)
    h = _rms_rows(x, ng_ref[0]).astype(BF16)
    z_ref[...] = jnp.dot(h, win_ref[0], preferred_element_type=F32)

    @pl.when(i == 0)
    def _():
        k_ref[0:BLOCK] = jnp.zeros((BLOCK, LANES), BF16)
        v_ref[0:BLOCK] = jnp.zeros((BLOCK, LANES), BF16)

    cos_t, sin_t = cos_ref[0], sin_ref[0]
    lane = lax.broadcasted_iota(jnp.int32, (TQ, LANES), 1)
    low_head = lane < HEAD_DIM
    first_half = (lane % HEAD_DIM) < (HEAD_DIM // 2)

    def head_norm_rope(v, gain):
        sq = v * v
        s_lo = jnp.sum(jnp.where(low_head, sq, 0.0), axis=-1, keepdims=True)
        s_hi = jnp.sum(jnp.where(low_head, 0.0, sq), axis=-1, keepdims=True)
        inv = jnp.where(low_head, lax.rsqrt(s_lo / HEAD_DIM + EPS), lax.rsqrt(s_hi / HEAD_DIM + EPS))
        vn = v * inv * gain
        swapped = jnp.where(first_half, pltpu.roll(vn, LANES - HEAD_DIM // 2, 1), pltpu.roll(vn, HEAD_DIM // 2, 1))
        return vn * cos_t + swapped * sin_t

    k_ref[BLOCK:] = head_norm_rope(z_ref[:, _OFF_K:_OFF_K + KV_WIDTH], gk_ref[0]).astype(BF16)
    v_ref[BLOCK:] = z_ref[:, _OFF_V:_OFF_V + KV_WIDTH].astype(BF16)
    scale = 1.0 / math.sqrt(HEAD_DIM)
    for j in range(ATTN_WIDTH // LANES):
        sl = slice(_OFF_Q + j * LANES, _OFF_Q + (j + 1) * LANES)
        q_ref[:, j * LANES:(j + 1) * LANES] = (head_norm_rope(z_ref[:, sl], gq_ref[0]) * scale).astype(BF16)

    row = lax.broadcasted_iota(jnp.int32, (2 * BLOCK, 2 * BLOCK), 0)
    col = lax.broadcasted_iota(jnp.int32, (2 * BLOCK, 2 * BLOCK), 1)
    local_tok = lambda rho: (rho % ROWS) * CHUNK + rho // ROWS
    qi = local_tok(row % BLOCK)
    col = (col // BLOCK) * BLOCK + local_tok(col % BLOCK)
    low_rows = lax.broadcasted_iota(jnp.int32, (2 * BLOCK, 1), 0) < BLOCK
    low_lane_b = lax.broadcasted_iota(jnp.int32, (BLOCK, LANES), 1) < HEAD_DIM
    no_prev = jnp.where(i == 0, BLOCK, 0)
    for r in range(TQ // BLOCK):
        keys = k_ref[r * BLOCK:(r + 2) * BLOCK]
        vals = v_ref[r * BLOCK:(r + 2) * BLOCK]
        lower = jnp.maximum(qi + 1, no_prev) if r == 0 else qi + 1
        valid = (col >= lower) & (col <= qi + BLOCK)
        for j in range(ATTN_WIDTH // LANES):
            qb = q_ref[r * BLOCK:(r + 1) * BLOCK, j * LANES:(j + 1) * LANES]
            zero = jnp.zeros_like(qb)
            lhs = jnp.concatenate([jnp.where(low_lane_b, qb, zero), jnp.where(low_lane_b, zero, qb)], axis=0)
            sc = jnp.where(valid, _dot_nt(lhs, keys), NEG_INF)
            sink = jnp.where(low_rows, sinks_ref[layer, j], sinks_ref[layer, j + N_Q_HEADS // 2])
            m = jnp.maximum(jnp.max(sc, axis=-1, keepdims=True), sink)
            p = jnp.exp(sc - m)
            den = jnp.sum(p, axis=-1, keepdims=True) + jnp.exp(sink - m)
            o = jnp.dot(p.astype(BF16), vals, preferred_element_type=F32) / den
            oa = jnp.where(low_lane_b, o[:BLOCK], o[BLOCK:])
            gate = z_ref[r * BLOCK:(r + 1) * BLOCK, _OFF_AG + j * LANES:_OFF_AG + (j + 1) * LANES]
            mix_ref[r * BLOCK:(r + 1) * BLOCK, j * LANES:(j + 1) * LANES] = (oa * _silu(gate)).astype(BF16)

    k_ref[0:BLOCK] = k_ref[TQ:TQ + BLOCK]
    v_ref[0:BLOCK] = v_ref[TQ:TQ + BLOCK]

    xscale = 1.0 / math.sqrt(X_HEAD_DIM)
    for hd in range(X_HEADS):
        sl = slice(hd * X_HEAD_DIM, (hd + 1) * X_HEAD_DIM)
        xq = _rms_rows(z_ref[:, _OFF_XQ + hd * X_HEAD_DIM:_OFF_XQ + (hd + 1) * X_HEAD_DIM], gxq_ref[0]).astype(BF16)
        sc = _dot_nt(xq, mk_ref[0, 0, :, sl]) * xscale
        p = jnp.exp(sc - jnp.max(sc, axis=-1, keepdims=True))
        den = jnp.sum(p, axis=-1, keepdims=True)
        o = jnp.dot(p.astype(BF16), mv_ref[0, 0, :, sl], preferred_element_type=F32) / den
        gate = z_ref[:, _OFF_XG + hd * X_HEAD_DIM:_OFF_XG + (hd + 1) * X_HEAD_DIM]
        mix_ref[:, ATTN_WIDTH + hd * X_HEAD_DIM:ATTN_WIDTH + (hd + 1) * X_HEAD_DIM] = (o * _silu(gate)).astype(BF16)

    res = x + jnp.dot(mix_ref[...], wout_ref[0], preferred_element_type=F32)
    for r in range(TQ // BLOCK):
        out_ref[0, :, r * ROWS:(r + 1) * ROWS, :] = res[r * BLOCK:(r + 1) * BLOCK].reshape(CHUNK, ROWS, d_model)


def _ac_layer(layer, x, cos_t, sin_t, sinks, norm_g, w_in_ac, gq, gk, mk, mv, gxq, w_out_ac):
    b, _, n_chunk, d = x.shape
    n_mem = mk.shape[2]
    tok = lambda w: pl.BlockSpec((1, TQ, w), lambda bi, i: (bi, i, 0))
    xspec = pl.BlockSpec((1, CHUNK, TQ // CHUNK, d), lambda bi, i: (bi, 0, i, 0))
    per_layer = lambda *shape: pl.BlockSpec((1,) + shape, lambda bi, i: (layer,) + (0,) * len(shape))
    memspec = pl.BlockSpec((1, 1, n_mem, X_WIDTH), lambda bi, i: (layer, bi, 0, 0))
    return pl.pallas_call(
        functools.partial(_ac_kernel, layer=layer),
        out_shape=jax.ShapeDtypeStruct(x.shape, F32),
        grid=(b, n_chunk * CHUNK // TQ),
        in_specs=[pl.BlockSpec(memory_space=pltpu.SMEM),
                  xspec, tok(LANES), tok(LANES),
                  per_layer(1, d), per_layer(d, AC_WIDTH), per_layer(1, LANES), per_layer(1, LANES),
                  memspec, memspec, per_layer(1, X_HEAD_DIM), per_layer(ATTN_WIDTH + X_WIDTH, d)],
        out_specs=xspec,
        scratch_shapes=[pltpu.VMEM((TQ, AC_WIDTH), F32),
                        pltpu.VMEM((TQ, ATTN_WIDTH), BF16),
                        pltpu.VMEM((BLOCK + TQ, LANES), BF16),
                        pltpu.VMEM((BLOCK + TQ, LANES), BF16),
                        pltpu.VMEM((TQ, ATTN_WIDTH + X_WIDTH), BF16)],
        compiler_params=pltpu.CompilerParams(dimension_semantics=("arbitrary", "arbitrary"),
                                             vmem_limit_bytes=VMEM_LIMIT),
        name=f"attn_layer{layer}",
    )(sinks, x, cos_t, sin_t, norm_g, w_in_ac, gq, gk, mk, mv, gxq, w_out_ac)


def _ssm_kernel(x_ref, o1_ref, ng_ref, winb_ref, wtws_ref, wcp_ref, pwr_ref, pwi_ref, dsk_ref, wglu_ref, bglu_ref,
                woutb_ref, out_ref, z_ref, y_ref, sg_ref, hs_ref, *, n_chunk, n_scan):
    step = pl.program_id(1)
    pad = n_chunk // 2

    @pl.when(step < CHUNK)
    def _project():
        h = _rms_rows(x_ref[0, 0], ng_ref[0]).astype(BF16)
        ut = _dot_nt(winb_ref[0], h)
        u = ut[:SSM_WIDTH]
        z_ref[step] = u.astype(BF16).reshape(SSM_GROUPS, SSM_CH, n_chunk)
        y_ref[step] = (u * dsk_ref[0]).reshape(SSM_GROUPS, SSM_CH, n_chunk)
        sg_ref[step] = _silu(ut[SSM_WIDTH:])

    @pl.when(step == CHUNK - 1)
    def _chunks():
        hs_ref[:, 0:pad, :] = jnp.zeros((2, pad, LANES), F32)
        rows = pl.ds(pad, n_chunk)

        def shifted(k):
            return hs_ref[0, pl.ds(pad - k, n_chunk), :], hs_ref[1, pl.ds(pad - k, n_chunk), :]

        def pair(gp, carry):
            g0, g1 = 2 * gp, 2 * gp + 1
            z0 = z_ref[:, g0].reshape(CK, n_chunk)
            z1 = z_ref[:, g1].reshape(CK, n_chunk)
            s0 = jnp.dot(wtws_ref[0, g0, CK:, :], z0, preferred_element_type=F32)
            s1 = jnp.dot(wtws_ref[0, g1, CK:, :], z1, preferred_element_type=F32)
            hr = jnp.concatenate([s0[:SSM_STATE], s1[:SSM_STATE]], axis=0).T
            hi = jnp.concatenate([s0[SSM_STATE:], s1[SSM_STATE:]], axis=0).T
            for j in range(n_scan):
                hs_ref[0, rows, :] = hr
                hs_ref[1, rows, :] = hi
                sr, si = shifted(1 << j)
                ar, ai = pwr_ref[0, gp, j:j + 1, :], pwi_ref[0, gp, j:j + 1, :]
                hr, hi = hr + ar * sr - ai * si, hi + ar * si + ai * sr
            hs_ref[0, rows, :] = hr
            hs_ref[1, rows, :] = hi
            pr, pi = shifted(1)
            yc = (_dot_nt(wcp_ref[0, gp, 0], pr.astype(BF16))
                  + _dot_nt(wcp_ref[0, gp, 1], pi.astype(BF16)))
            y0 = jnp.dot(wtws_ref[0, g0, 0:CK, :], z0, preferred_element_type=F32) + yc[:CK]
            y1 = jnp.dot(wtws_ref[0, g1, 0:CK, :], z1, preferred_element_type=F32) + yc[CK:]
            y_ref[:, g0] = y_ref[:, g0] + y0.reshape(CHUNK, SSM_CH, n_chunk)
            y_ref[:, g1] = y_ref[:, g1] + y1.reshape(CHUNK, SSM_CH, n_chunk)
            return carry

        lax.fori_loop(0, SSM_GROUPS // 2, pair, 0)

    @pl.when(step >= CHUNK)
    def _finish():
        t = step - CHUNK
        y = jax.nn.gelu(y_ref[t].reshape(SSM_WIDTH, n_chunk))
        gate = _sigmoid(jnp.dot(wglu_ref[0], y.astype(BF16), preferred_element_type=F32) + bglu_ref[0])
        ob = (y * gate * sg_ref[t]).T.astype(BF16)
        out_ref[0, 0] = o1_ref[0, 0] + jnp.dot(ob, woutb_ref[0], preferred_element_type=F32)


def _ssm_layer(layer, x, o1, norm_g, w_in_bt, wtws, wcp, pwr, pwi, d_skip, w_glu_t, b_glu, w_out_b, n_scan):
    b, _, n_chunk, d = x.shape
    per_layer = lambda *shape: pl.BlockSpec((1,) + shape, lambda bi, st: (layer,) + (0,) * len(shape))
    x_spec = pl.BlockSpec((1, 1, n_chunk, d), lambda bi, st: (bi, jnp.minimum(st, CHUNK - 1), 0, 0))
    o_spec = pl.BlockSpec((1, 1, n_chunk, d), lambda bi, st: (bi, jnp.maximum(st - CHUNK, 0), 0, 0))
    return pl.pallas_call(
        functools.partial(_ssm_kernel, n_chunk=n_chunk, n_scan=n_scan),
        out_shape=jax.ShapeDtypeStruct(x.shape, F32),
        grid=(b, 2 * CHUNK),
        in_specs=[x_spec, o_spec,
                  per_layer(1, d), per_layer(2 * SSM_WIDTH, d),
                  per_layer(SSM_GROUPS, CK + 2 * SSM_STATE, CK),
                  per_layer(SSM_GROUPS // 2, 2, 2 * CK, LANES),
                  per_layer(SSM_GROUPS // 2, n_scan, LANES), per_layer(SSM_GROUPS // 2, n_scan, LANES),
                  per_layer(SSM_WIDTH, 1), per_layer(SSM_WIDTH, SSM_WIDTH), per_layer(SSM_WIDTH, 1),
                  per_layer(SSM_WIDTH, d)],
        out_specs=o_spec,
        scratch_shapes=[pltpu.VMEM((CHUNK, SSM_GROUPS, SSM_CH, n_chunk), BF16),
                        pltpu.VMEM((CHUNK, SSM_GROUPS, SSM_CH, n_chunk), F32),
                        pltpu.VMEM((CHUNK, SSM_WIDTH, n_chunk), F32),
                        pltpu.VMEM((2, n_chunk // 2 + n_chunk, LANES), F32)],
        compiler_params=pltpu.CompilerParams(dimension_semantics=("arbitrary", "arbitrary"),
                                             vmem_limit_bytes=VMEM_LIMIT),
        name=f"ssm_layer{layer}",
    )(x, o1, norm_g, w_in_bt, wtws, wcp, pwr, pwi, d_skip, w_glu_t, b_glu, w_out_b)


def kernel(x, mem, positions, norm_g, w_in, q_norm_g, k_norm_g, sinks, lam_re, lam_im, log_dt, b_re, b_im,
           c_re, c_im, d_skip, w_glu, b_glu, mem_norm_g, w_mem_kv, xq_norm_g, xk_norm_g, w_out):
    b, s, d = x.shape
    depth = w_in.shape[0]
    assert s % TQ == 0 and (s // CHUNK) % LANES == 0
    n_scan = (s // CHUNK - 1).bit_length()

    pair = np.concatenate([np.r_[j * HEAD_DIM:(j + 1) * HEAD_DIM,
                                 (j + N_Q_HEADS // 2) * HEAD_DIM:(j + N_Q_HEADS // 2 + 1) * HEAD_DIM]
                           for j in range(N_Q_HEADS // 2)])
    o_k = ATTN_WIDTH
    o_v = o_k + KV_WIDTH
    o_ag = o_v + KV_WIDTH
    o_su = o_ag + ATTN_WIDTH
    o_sg = o_su + SSM_WIDTH
    o_xq = o_sg + SSM_WIDTH
    o_xg = o_xq + X_WIDTH
    w_in_ac = jnp.concatenate([w_in[:, :, :o_k][:, :, pair], w_in[:, :, o_k:o_ag],
                               w_in[:, :, o_ag:o_su][:, :, pair], w_in[:, :, o_xq:]], axis=-1).astype(BF16)
    w_in_bt = jnp.swapaxes(w_in[:, :, o_su:o_xq], 1, 2).astype(BF16)
    w_out_ac = jnp.concatenate([w_out[:, :ATTN_WIDTH][:, pair], w_out[:, ATTN_WIDTH + SSM_WIDTH:]],
                               axis=1).astype(BF16)
    w_out_b = w_out[:, ATTN_WIDTH:ATTN_WIDTH + SSM_WIDTH].astype(BF16)
    w_glu_t = jnp.swapaxes(w_glu, 1, 2).astype(BF16)
    norm_g3 = norm_g.reshape(depth, 1, d)
    gq = jnp.tile(q_norm_g, (1, LANES // HEAD_DIM)).reshape(depth, 1, LANES)
    gk = jnp.tile(k_norm_g, (1, LANES // HEAD_DIM)).reshape(depth, 1, LANES)
    gxq = xq_norm_g.reshape(depth, 1, X_HEAD_DIM)
    d_skip3 = d_skip.reshape(depth, SSM_WIDTH, 1)
    b_glu3 = b_glu.reshape(depth, SSM_WIDTH, 1)

    n_chunk = s // CHUNK
    pos_blocks = positions.reshape(b, s // BLOCK, ROWS, CHUNK).swapaxes(2, 3).reshape(b, s)
    cos_t, sin_t = _rope_tables(pos_blocks)
    mk, mv = _mem_kv(mem, mem_norm_g, w_mem_kv, xk_norm_g)
    wtws, wcr, wci, pw = _s5_prep(lam_re, lam_im, log_dt, b_re, b_im, c_re, c_im, n_scan)
    p = SSM_STATE
    zero = jnp.zeros_like(wcr)
    pair_rows = lambda w: jnp.concatenate([jnp.concatenate([w, zero], -1)[:, 0::2],
                                           jnp.concatenate([zero, w], -1)[:, 1::2]], axis=2)
    wcp = jnp.stack([pair_rows(wcr), pair_rows(wci)], axis=2)
    pair_lanes = lambda a: a[:, :, :, :n_scan].reshape(depth, SSM_GROUPS // 2, 2 * p, n_scan).swapaxes(2, 3)
    pwr, pwi = pair_lanes(pw[:, :, :p]), pair_lanes(pw[:, :, p:])

    xp = x.reshape(b, n_chunk, CHUNK, d).swapaxes(1, 2)
    for layer in range(depth):
        o1 = _ac_layer(layer, xp, cos_t, sin_t, sinks, norm_g3, w_in_ac, gq, gk, mk, mv, gxq, w_out_ac)
        xp = _ssm_layer(layer, xp, o1, norm_g3, w_in_bt, wtws, wcp, pwr, pwi, d_skip3, w_glu_t, b_glu3, w_out_b,
                        n_scan)
    return xp.swapaxes(1, 2).reshape(b, s, d)
```

```python
import functools
import math

import jax
import jax.numpy as jnp
from jax import lax
from jax.experimental import pallas as pl
from jax.experimental.pallas import tpu as pltpu

F32 = jnp.float32
BF16 = jnp.bfloat16

EPS = 1e-6
ROPE_THETA = 10000.0
NEG_INF = -1e30
LOG2E = math.log2(math.e)

HEAD_DIM = 64
N_Q_HEADS = 8
N_KV_HEADS = 2
GQA_GROUP = N_Q_HEADS // N_KV_HEADS
BLOCK = 128
ATTN_WIDTH = N_Q_HEADS * HEAD_DIM
KV_WIDTH = N_KV_HEADS * HEAD_DIM
SSM_CH = 16
SSM_GROUPS = 32
SSM_STATE = 64
SSM_WIDTH = SSM_GROUPS * SSM_CH
X_HEADS = 4
X_HEAD_DIM = 128
X_WIDTH = X_HEADS * X_HEAD_DIM

LANES = 128
CHUNK = 16
CK = CHUNK * SSM_CH
ROWS = BLOCK // CHUNK
TQ = 1024
SUB = 512
SPS = 2
PAIRS = 4
VMEM_LIMIT = 56 * 1024 * 1024

_OFF_Q = 0
_OFF_K = _OFF_Q + ATTN_WIDTH
_OFF_V = _OFF_K + KV_WIDTH
_OFF_AG = _OFF_V + KV_WIDTH
_OFF_XQ = _OFF_AG + ATTN_WIDTH
_OFF_XG = _OFF_XQ + X_WIDTH
AC_WIDTH = _OFF_XG + X_WIDTH


def _sigmoid(v):
    return 1.0 / (1.0 + jnp.exp(-v))


def _silu(v):
    return v * _sigmoid(v)


def _rms_rows(v, gain):
    return v * lax.rsqrt(jnp.mean(v * v, axis=-1, keepdims=True) + EPS) * gain


def _dot_nt(a, b):
    return lax.dot_general(a, b, (((1,), (1,)), ((), ())), preferred_element_type=F32)


def _rope_kernel(pos_ref, inv_ref, cos_ref, sin_ref):
    ang = inv_ref[...] * pos_ref[0].astype(F32)
    row = lax.broadcasted_iota(jnp.int32, ang.shape, 0)
    cos_ref[0] = jnp.cos(ang)
    sin_ref[0] = jnp.where(row < HEAD_DIM // 2, -jnp.sin(ang), jnp.sin(ang))


def _rope_tables(positions):
    b, s = positions.shape
    half = HEAD_DIM // 2
    inv = ROPE_THETA ** (-jnp.arange(half, dtype=F32) / half)
    inv = jnp.tile(inv, 2).reshape(HEAD_DIM, 1)
    ts = min(s, 2048)
    spec = pl.BlockSpec((1, HEAD_DIM, ts), lambda i, j: (i, 0, j))
    return pl.pallas_call(
        _rope_kernel,
        out_shape=(jax.ShapeDtypeStruct((b, HEAD_DIM, s), F32),) * 2,
        grid=(b, s // ts),
        in_specs=[pl.BlockSpec((1, 1, ts), lambda i, j: (i, 0, j)),
                  pl.BlockSpec((HEAD_DIM, 1), lambda i, j: (0, 0))],
        out_specs=(spec, spec),
        name="rope_tables",
    )(positions.reshape(b, 1, s), inv)


def _memkv_kernel(mem_ref, g_ref, w_ref, gk_ref, mk_ref, mv_ref):
    h = _rms_rows(mem_ref[0], g_ref[0]).astype(BF16)
    kv = jnp.dot(h, w_ref[0], preferred_element_type=F32)
    for hd in range(X_HEADS):
        sl = slice(hd * X_HEAD_DIM, (hd + 1) * X_HEAD_DIM)
        mk_ref[0, 0, :, sl] = _rms_rows(kv[:, sl], gk_ref[0]).astype(BF16)
    mv_ref[0, 0] = kv[:, X_WIDTH:].T.astype(BF16)


def _mem_kv(mem, mem_norm_g, w_mem_kv, xk_norm_g):
    b, n_mem, d = mem.shape
    depth = w_mem_kv.shape[0]
    out = jax.ShapeDtypeStruct((depth, b, n_mem, X_WIDTH), BF16)
    ospec = pl.BlockSpec((1, 1, n_mem, X_WIDTH), lambda l, i: (l, i, 0, 0))
    out_t = jax.ShapeDtypeStruct((depth, b, X_WIDTH, n_mem), BF16)
    ospec_t = pl.BlockSpec((1, 1, X_WIDTH, n_mem), lambda l, i: (l, i, 0, 0))
    return pl.pallas_call(
        _memkv_kernel,
        out_shape=(out, out_t),
        grid=(depth, b),
        in_specs=[pl.BlockSpec((1, n_mem, d), lambda l, i: (i, 0, 0)),
                  pl.BlockSpec((1, 1, d), lambda l, i: (l, 0, 0)),
                  pl.BlockSpec((1, d, 2 * X_WIDTH), lambda l, i: (l, 0, 0)),
                  pl.BlockSpec((1, 1, X_HEAD_DIM), lambda l, i: (l, 0, 0))],
        out_specs=(ospec, ospec_t),
        name="mem_kv",
    )(mem, mem_norm_g.reshape(depth, 1, d), w_mem_kv.astype(BF16), xk_norm_g.reshape(depth, 1, X_HEAD_DIM))


def _s5_prep_kernel(lr_ref, li_ref, ldt_ref, brt_ref, bit_ref, crt_ref, cit_ref,
                    wtws_ref, wc_ref, pwr_ref, pwi_ref, *, n_scan):
    n_groups = lr_ref.shape[1]
    tab_row = lax.broadcasted_iota(jnp.int32, (3 * CHUNK, LANES), 0)
    n_tab = jnp.where(tab_row < 2 * CHUNK, tab_row, 3 * CHUNK - 1 - tab_row).astype(F32)
    n_scan_rows = jnp.left_shift(jnp.int32(CHUNK), jnp.minimum(
        lax.broadcasted_iota(jnp.int32, (pwr_ref.shape[2], LANES), 0), n_scan - 1)).astype(F32)
    low = lax.broadcasted_iota(jnp.int32, (CK, LANES), 1) < SSM_STATE
    lane_s = lax.broadcasted_iota(jnp.int32, (CK, CK), 1) // SSM_CH
    hi = lax.Precision.HIGHEST

    def per_chunk_row(tab, first):
        rows = tab[first:first + CHUNK]
        return jnp.broadcast_to(rows[:, None, :], (CHUNK, SSM_CH, LANES)).reshape(CK, LANES)

    def group(g, carry):
        dt = jnp.exp(ldt_ref[0, g])
        lr, li = lr_ref[0, g], li_ref[0, g]

        def power(n):
            mag = jnp.exp(lr * dt * n)
            return mag * jnp.cos(li * dt * n), mag * jnp.sin(li * dt * n)

        tab_r, tab_i = power(n_tab)
        ar, ai = tab_r[1:2], tab_i[1:2]
        den = lr * lr + li * li
        fr = ((ar - 1.0) * lr + ai * li) / den
        fi = (ai * lr - (ar - 1.0) * li) / den
        brt, bit = brt_ref[0, g], bit_ref[0, g]
        bbr = fr * brt - fi * bit
        bbi = fr * bit + fi * brt

        crt, cit = crt_ref[0, g], cit_ref[0, g]

        def c_times_power(first):
            pr, pi = per_chunk_row(tab_r, first), per_chunk_row(tab_i, first)
            return jnp.where(low, crt * pr - cit * pi, -(crt * pi + cit * pr))

        d = lax.dot_general(c_times_power(0), jnp.where(low, bbr, bbi), (((1,), (1,)), ((), ())),
                            precision=hi, preferred_element_type=F32)
        wt = jnp.where(lane_s == 0, d, 0.0)
        for s in range(1, CHUNK):
            shifted = jnp.concatenate([jnp.zeros((s * SSM_CH, CK), F32), d[:CK - s * SSM_CH]], axis=0)
            wt = jnp.where(lane_s == s, shifted, wt)
        wtws_ref[0, g, 0:CK, :] = wt.astype(BF16)

        qr, qi = per_chunk_row(tab_r, 2 * CHUNK), per_chunk_row(tab_i, 2 * CHUNK)
        ws = jnp.where(low, qr * bbr - qi * bbi, qr * bbi + qi * bbr)
        wtws_ref[0, g, CK:, :] = ws.T.astype(BF16)

        wc_ref[0, g] = c_times_power(1).astype(BF16)

        wr, wi = power(n_scan_rows)
        pwr_ref[0, g] = wr
        pwi_ref[0, g] = wi
        return carry

    lax.fori_loop(0, n_groups, group, 0)


def _s5_prep(lam_re, lam_im, log_dt, b_re, b_im, c_re, c_im, n_scan):
    depth, g, p = lam_re.shape
    twice = lambda a: jnp.concatenate([a, a], axis=-1)
    row = lambda a: twice(a).reshape(depth, g, 1, 2 * p)
    tile_bt = lambda a: twice(jnp.tile(jnp.swapaxes(a, 2, 3), (1, 1, CHUNK, 1)))
    tile_c = lambda a: twice(jnp.tile(a, (1, 1, CHUNK, 1)))
    n_rows = -(-n_scan // 8) * 8
    gb = 8
    spec = lambda *shape: pl.BlockSpec((1, gb) + shape, lambda l, i: (l, i, 0, 0))
    return pl.pallas_call(
        functools.partial(_s5_prep_kernel, n_scan=n_scan),
        out_shape=(jax.ShapeDtypeStruct((depth, g, CK + 2 * p, CK), BF16),
                   jax.ShapeDtypeStruct((depth, g, CK, 2 * p), BF16),
                   jax.ShapeDtypeStruct((depth, g, n_rows, 2 * p), F32),
                   jax.ShapeDtypeStruct((depth, g, n_rows, 2 * p), F32)),
        grid=(depth, g // gb),
        in_specs=[spec(1, 2 * p), spec(1, 2 * p), spec(1, 1),
                  spec(CK, 2 * p), spec(CK, 2 * p), spec(CK, 2 * p), spec(CK, 2 * p)],
        out_specs=(spec(CK + 2 * p, CK), spec(CK, 2 * p), spec(n_rows, 2 * p), spec(n_rows, 2 * p)),
        compiler_params=pltpu.CompilerParams(vmem_limit_bytes=VMEM_LIMIT),
        name="s5_prep",
    )(row(lam_re), row(lam_im), log_dt.reshape(depth, g, 1, 1),
      tile_bt(b_re), tile_bt(b_im), tile_c(c_re), tile_c(c_im))


def _col_rms(v):
    return lax.rsqrt(jnp.mean(v * v, axis=0, keepdims=True) + EPS)


def _ac_kernel(sinks_ref, x_ref, cos_ref, sin_ref, ng_ref, win_ref, gq_ref, gk_ref, mk_ref, mvt_ref,
               gxq_ref, wout_ref, out_ref, z_ref, q_ref, k_ref, v_ref, mix_ref, *, layer):
    i = pl.program_id(1)
    d_model = x_ref.shape[-1]
    half = HEAD_DIM // 2
    qscale = LOG2E / math.sqrt(HEAD_DIM)
    xscale = LOG2E / math.sqrt(X_HEAD_DIM)
    n_mem = mk_ref.shape[2]

    @pl.when(i == 0)
    def _():
        k_ref[0:BLOCK] = jnp.zeros((BLOCK, LANES), BF16)
        v_ref[:, 0:BLOCK] = jnp.zeros((KV_WIDTH, BLOCK), BF16)

    key_row = lax.broadcasted_iota(jnp.int32, (2 * BLOCK, BLOCK), 0)
    q_col = lax.broadcasted_iota(jnp.int32, (2 * BLOCK, BLOCK), 1)
    local_tok = lambda rho: (rho % ROWS) * CHUNK + rho // ROWS
    qi = local_tok(q_col)
    kj = (key_row // BLOCK) * BLOCK + local_tok(key_row % BLOCK)
    band = (kj >= qi + 1) & (kj <= qi + BLOCK)
    band_first = band & (kj >= jnp.where(i == 0, BLOCK, 0))
    ones_rows = jnp.ones((2 * ROWS, 2 * BLOCK), BF16)
    ones_mem = jnp.ones((2 * ROWS, n_mem), BF16)
    zeros_q = jnp.zeros((HEAD_DIM, GQA_GROUP * BLOCK), BF16)

    def tokens(blocks):
        return jnp.concatenate([x_ref[0, :, r * ROWS:(r + 1) * ROWS, :].reshape(BLOCK, d_model) for r in blocks],
                               axis=0)

    for sub in range(TQ // SUB):
        lanes = slice(sub * SUB, (sub + 1) * SUB)
        h = _rms_rows(tokens(range(sub * SUB // BLOCK, (sub + 1) * SUB // BLOCK)), ng_ref[0]).astype(BF16)
        z_ref[:, lanes] = _dot_nt(win_ref[0], h)

    for sub in range(TQ // SUB):
        lanes = slice(sub * SUB, (sub + 1) * SUB)
        blocks = range(sub * SUB // BLOCK, (sub + 1) * SUB // BLOCK)
        cos_t, sin_t = cos_ref[0, :, lanes], sin_ref[0, :, lanes]

        def head_norm_rope(v, gain):
            vn = v * _col_rms(v) * gain
            return vn * cos_t + jnp.concatenate([vn[half:], vn[:half]], axis=0) * sin_t

        kt = jnp.concatenate(
            [head_norm_rope(z_ref[_OFF_K + hk * HEAD_DIM:_OFF_K + (hk + 1) * HEAD_DIM, lanes], gk_ref[0, :, 0:SUB])
             for hk in range(N_KV_HEADS)], axis=0)
        k_ref[BLOCK + sub * SUB:BLOCK + (sub + 1) * SUB] = kt.T.astype(BF16)
        v_ref[:, BLOCK + sub * SUB:BLOCK + (sub + 1) * SUB] = z_ref[_OFF_V:_OFF_V + KV_WIDTH, lanes].astype(BF16)
        for hq in range(N_Q_HEADS):
            rows = slice(hq * HEAD_DIM, (hq + 1) * HEAD_DIM)
            q_ref[rows, lanes] = (head_norm_rope(z_ref[rows, lanes], gq_ref[0, :, 0:SUB]) * qscale).astype(BF16)

        for r in blocks:
            cols = slice(r * BLOCK, (r + 1) * BLOCK)
            keys = k_ref[r * BLOCK:(r + 2) * BLOCK]
            valid = band_first if r == 0 else band
            for hk in range(N_KV_HEADS):
                heads = range(hk * GQA_GROUP, (hk + 1) * GQA_GROUP)
                qt = jnp.concatenate([q_ref[hq * HEAD_DIM:(hq + 1) * HEAD_DIM, cols] for hq in heads], axis=1)
                qt = jnp.concatenate([qt, zeros_q] if hk == 0 else [zeros_q, qt], axis=0)
                sc = jnp.dot(keys, qt, preferred_element_type=F32)
                pt, esink = [], []
                for g, hq in enumerate(heads):
                    s_h = jnp.where(valid, sc[:, g * BLOCK:(g + 1) * BLOCK], NEG_INF)
                    sink = sinks_ref[layer, hq] * LOG2E
                    m = jnp.maximum(jnp.max(s_h, axis=0, keepdims=True), sink)
                    pt.append(jnp.exp2(s_h - m).astype(BF16))
                    esink.append(jnp.exp2(sink - m))
                vals = jnp.concatenate(
                    [v_ref[hk * HEAD_DIM:(hk + 1) * HEAD_DIM, r * BLOCK:(r + 2) * BLOCK], ones_rows], axis=0)
                o = jnp.dot(vals, jnp.concatenate(pt, axis=1), preferred_element_type=F32)
                inv_den = 1.0 / (o[HEAD_DIM:HEAD_DIM + 1] + jnp.concatenate(esink, axis=1))
                o = o[:HEAD_DIM] * inv_den
                for g, hq in enumerate(heads):
                    rows = slice(hq * HEAD_DIM, (hq + 1) * HEAD_DIM)
                    gate = z_ref[_OFF_AG + hq * HEAD_DIM:_OFF_AG + (hq + 1) * HEAD_DIM, cols]
                    mix_ref[rows, cols] = (o[:, g * BLOCK:(g + 1) * BLOCK] * _silu(gate)).astype(BF16)

        for hd in range(X_HEADS):
            rows = slice(hd * X_HEAD_DIM, (hd + 1) * X_HEAD_DIM)
            xq = z_ref[_OFF_XQ + hd * X_HEAD_DIM:_OFF_XQ + (hd + 1) * X_HEAD_DIM, lanes]
            xq = (xq * _col_rms(xq) * gxq_ref[0, :, 0:SUB] * xscale).astype(BF16)
            sc = jnp.dot(mk_ref[0, 0, :, rows], xq, preferred_element_type=F32)
            p = jnp.exp2(sc - jnp.max(sc, axis=0, keepdims=True)).astype(BF16)
            vals = jnp.concatenate([mvt_ref[0, 0, rows, :], ones_mem], axis=0)
            o = jnp.dot(vals, p, preferred_element_type=F32)
            o = o[:X_HEAD_DIM] * (1.0 / o[X_HEAD_DIM:X_HEAD_DIM + 1])
            gate = z_ref[_OFF_XG + hd * X_HEAD_DIM:_OFF_XG + (hd + 1) * X_HEAD_DIM, lanes]
            mix_ref[ATTN_WIDTH + hd * X_HEAD_DIM:ATTN_WIDTH + (hd + 1) * X_HEAD_DIM, lanes] = (
                o * _silu(gate)).astype(BF16)

        y = lax.dot_general(mix_ref[:, lanes], wout_ref[0], (((0,), (0,)), ((), ())), preferred_element_type=F32)
        res = tokens(blocks) + y
        for n, r in enumerate(blocks):
            out_ref[0, :, r * ROWS:(r + 1) * ROWS, :] = res[n * BLOCK:(n + 1) * BLOCK].reshape(CHUNK, ROWS, d_model)

    k_ref[0:BLOCK] = k_ref[TQ:TQ + BLOCK]
    v_ref[:, 0:BLOCK] = v_ref[:, TQ:TQ + BLOCK]


def _ac_layer(layer, x, cos_t, sin_t, sinks, norm_g, w_in_act, gq, gk, mk, mvt, gxq, w_out_ac):
    b, _, n_chunk, d = x.shape
    n_mem = mk.shape[2]
    tab = pl.BlockSpec((1, HEAD_DIM, TQ), lambda bi, i: (bi, 0, i))
    xspec = pl.BlockSpec((1, CHUNK, TQ // CHUNK, d), lambda bi, i: (bi, 0, i, 0))
    per_layer = lambda *shape: pl.BlockSpec((1,) + shape, lambda bi, i: (layer,) + (0,) * len(shape),
                                            pipeline_mode=pl.Buffered(1))
    return pl.pallas_call(
        functools.partial(_ac_kernel, layer=layer),
        out_shape=jax.ShapeDtypeStruct(x.shape, F32),
        grid=(b, n_chunk * CHUNK // TQ),
        in_specs=[pl.BlockSpec(memory_space=pltpu.SMEM),
                  xspec, tab, tab,
                  per_layer(1, d), per_layer(AC_WIDTH, d), per_layer(HEAD_DIM, TQ), per_layer(HEAD_DIM, TQ),
                  pl.BlockSpec((1, 1, n_mem, X_WIDTH), lambda bi, i: (layer, bi, 0, 0)),
                  pl.BlockSpec((1, 1, X_WIDTH, n_mem), lambda bi, i: (layer, bi, 0, 0)),
                  per_layer(X_HEAD_DIM, TQ), per_layer(ATTN_WIDTH + X_WIDTH, d)],
        out_specs=xspec,
        scratch_shapes=[pltpu.VMEM((AC_WIDTH, TQ), F32),
                        pltpu.VMEM((ATTN_WIDTH, TQ), BF16),
                        pltpu.VMEM((BLOCK + TQ, LANES), BF16),
                        pltpu.VMEM((KV_WIDTH, BLOCK + TQ), BF16),
                        pltpu.VMEM((ATTN_WIDTH + X_WIDTH, TQ), BF16)],
        compiler_params=pltpu.CompilerParams(dimension_semantics=("arbitrary", "arbitrary"),
                                             vmem_limit_bytes=VMEM_LIMIT),
        name=f"attn_layer{layer}",
    )(sinks, x, cos_t, sin_t, norm_g, w_in_act, gq, gk, mk, mvt, gxq, w_out_ac)


def _ssm_kernel(x_ref, o1_ref, ng_ref, winb_ref, wtws_ref, wcp_ref, pwr_ref, pwi_ref, dsk_ref, wglu_ref, bglu_ref,
                woutb_ref, out_ref, z_ref, y_ref, sg_ref, hs_ref, *, n_chunk, n_scan):
    step = pl.program_id(1)
    n_proj = CHUNK // SPS
    pad = n_chunk // 2
    d_model = x_ref.shape[-1]

    @pl.when(step < n_proj)
    def _project():
        h = _rms_rows(x_ref[0].reshape(SPS * n_chunk, d_model), ng_ref[0]).astype(BF16)
        ut = _dot_nt(winb_ref[0], h)
        for e in range(SPS):
            s = SPS * step + e
            u = ut[:SSM_WIDTH, e * n_chunk:(e + 1) * n_chunk]
            z_ref[s] = u.astype(BF16).reshape(SSM_GROUPS, SSM_CH, n_chunk)
            y_ref[s] = (u * dsk_ref[0]).reshape(SSM_GROUPS, SSM_CH, n_chunk)
            sg_ref[s] = _silu(ut[SSM_WIDTH:, e * n_chunk:(e + 1) * n_chunk])

    @pl.when(step == n_proj - 1)
    def _chunks():
        hs_ref[:, :, 0:pad, :] = jnp.zeros((PAIRS, 2, pad, LANES), F32)
        rows = pl.ds(pad, n_chunk)

        def pairs(it, carry):
            gps = [PAIRS * it + k for k in range(PAIRS)]
            zs = [[z_ref[:, 2 * gp + e].reshape(CK, n_chunk) for e in range(2)] for gp in gps]
            hr, hi = [], []
            for k, gp in enumerate(gps):
                s = [jnp.dot(wtws_ref[0, 2 * gp + e, CK:, :], zs[k][e], preferred_element_type=F32)
                     for e in range(2)]
                hr.append(jnp.concatenate([s[0][:SSM_STATE], s[1][:SSM_STATE]], axis=0).T)
                hi.append(jnp.concatenate([s[0][SSM_STATE:], s[1][SSM_STATE:]], axis=0).T)
            for j in range(n_scan + 1):
                shift = (1 << j) if j < n_scan else 1
                for k, gp in enumerate(gps):
                    hs_ref[k, 0, rows, :] = hr[k]
                    hs_ref[k, 1, rows, :] = hi[k]
                for k, gp in enumerate(gps):
                    sr = hs_ref[k, 0, pl.ds(pad - shift, n_chunk), :]
                    si = hs_ref[k, 1, pl.ds(pad - shift, n_chunk), :]
                    if j < n_scan:
                        ar, ai = pwr_ref[0, gp, j:j + 1, :], pwi_ref[0, gp, j:j + 1, :]
                        hr[k], hi[k] = hr[k] + ar * sr - ai * si, hi[k] + ar * si + ai * sr
                    else:
                        yc = (_dot_nt(wcp_ref[0, gp, 0], sr.astype(BF16))
                              + _dot_nt(wcp_ref[0, gp, 1], si.astype(BF16)))
                        for e in range(2):
                            y = jnp.dot(wtws_ref[0, 2 * gp + e, 0:CK, :], zs[k][e], preferred_element_type=F32)
                            y = y + yc[e * CK:(e + 1) * CK]
                            y_ref[:, 2 * gp + e] = y_ref[:, 2 * gp + e] + y.reshape(CHUNK, SSM_CH, n_chunk)
            return carry

        lax.fori_loop(0, SSM_GROUPS // (2 * PAIRS), pairs, 0)

    @pl.when(step >= n_proj)
    def _finish():
        t0 = SPS * (step - n_proj)
        y = jnp.concatenate([y_ref[t0 + e].reshape(SSM_WIDTH, n_chunk) for e in range(SPS)], axis=1)
        sg = jnp.concatenate([sg_ref[t0 + e] for e in range(SPS)], axis=1)
        y = jax.nn.gelu(y)
        gate = _sigmoid(jnp.dot(wglu_ref[0], y.astype(BF16), preferred_element_type=F32) + bglu_ref[0])
        ob = (y * gate * sg).T.astype(BF16)
        res = o1_ref[0].reshape(SPS * n_chunk, d_model) + jnp.dot(ob, woutb_ref[0], preferred_element_type=F32)
        out_ref[0] = res.reshape(SPS, n_chunk, d_model)


def _ssm_layer(layer, x, o1, norm_g, w_in_bt, wtws, wcp, pwr, pwi, d_skip, w_glu_t, b_glu, w_out_b, n_scan):
    b, _, n_chunk, d = x.shape
    n_proj = CHUNK // SPS
    per_layer = lambda *shape: pl.BlockSpec((1,) + shape, lambda bi, st: (layer,) + (0,) * len(shape),
                                            pipeline_mode=pl.Buffered(1))
    x_spec = pl.BlockSpec((1, SPS, n_chunk, d), lambda bi, st: (bi, jnp.minimum(st, n_proj - 1), 0, 0))
    o_spec = pl.BlockSpec((1, SPS, n_chunk, d), lambda bi, st: (bi, jnp.maximum(st - n_proj, 0), 0, 0))
    return pl.pallas_call(
        functools.partial(_ssm_kernel, n_chunk=n_chunk, n_scan=n_scan),
        out_shape=jax.ShapeDtypeStruct(x.shape, F32),
        grid=(b, 2 * n_proj),
        in_specs=[x_spec, o_spec,
                  per_layer(1, d), per_layer(2 * SSM_WIDTH, d),
                  per_layer(SSM_GROUPS, CK + 2 * SSM_STATE, CK),
                  per_layer(SSM_GROUPS // 2, 2, 2 * CK, LANES),
                  per_layer(*pwr.shape[1:]), per_layer(*pwi.shape[1:]),
                  per_layer(SSM_WIDTH, 1), per_layer(SSM_WIDTH, SSM_WIDTH), per_layer(SSM_WIDTH, 1),
                  per_layer(SSM_WIDTH, d)],
        out_specs=o_spec,
        scratch_shapes=[pltpu.VMEM((CHUNK, SSM_GROUPS, SSM_CH, n_chunk), BF16),
                        pltpu.VMEM((CHUNK, SSM_GROUPS, SSM_CH, n_chunk), F32),
                        pltpu.VMEM((CHUNK, SSM_WIDTH, n_chunk), F32),
                        pltpu.VMEM((PAIRS, 2, n_chunk // 2 + n_chunk, LANES), F32)],
        compiler_params=pltpu.CompilerParams(dimension_semantics=("arbitrary", "arbitrary"),
                                             vmem_limit_bytes=VMEM_LIMIT),
        name=f"ssm_layer{layer}",
    )(x, o1, norm_g, w_in_bt, wtws, wcp, pwr, pwi, d_skip, w_glu_t, b_glu, w_out_b)


def kernel(x, mem, positions, norm_g, w_in, q_norm_g, k_norm_g, sinks, lam_re, lam_im, log_dt, b_re, b_im,
           c_re, c_im, d_skip, w_glu, b_glu, mem_norm_g, w_mem_kv, xq_norm_g, xk_norm_g, w_out):
    b, s, d = x.shape
    depth = w_in.shape[0]
    assert s % TQ == 0 and (s // CHUNK) % LANES == 0
    n_scan = (s // CHUNK - 1).bit_length()

    o_su = _OFF_AG + ATTN_WIDTH
    o_xq = o_su + 2 * SSM_WIDTH
    w_in_act = jnp.swapaxes(jnp.concatenate([w_in[:, :, :o_su], w_in[:, :, o_xq:]], axis=-1), 1, 2).astype(BF16)
    w_in_bt = jnp.swapaxes(w_in[:, :, o_su:o_xq], 1, 2).astype(BF16)
    w_out_ac = jnp.concatenate([w_out[:, :ATTN_WIDTH], w_out[:, ATTN_WIDTH + SSM_WIDTH:]], axis=1).astype(BF16)
    w_out_b = w_out[:, ATTN_WIDTH:ATTN_WIDTH + SSM_WIDTH].astype(BF16)
    w_glu_t = jnp.swapaxes(w_glu, 1, 2).astype(BF16)
    norm_g3 = norm_g.reshape(depth, 1, d)
    over_tokens = lambda g: jnp.broadcast_to(g[:, :, None], g.shape + (TQ,))
    gq, gk, gxq = over_tokens(q_norm_g), over_tokens(k_norm_g), over_tokens(xq_norm_g)
    d_skip3 = d_skip.reshape(depth, SSM_WIDTH, 1)
    b_glu3 = b_glu.reshape(depth, SSM_WIDTH, 1)

    n_chunk = s // CHUNK
    pos_blocks = positions.reshape(b, s // BLOCK, ROWS, CHUNK).swapaxes(2, 3).reshape(b, s)
    cos_t, sin_t = _rope_tables(pos_blocks)
    mk, mvt = _mem_kv(mem, mem_norm_g, w_mem_kv, xk_norm_g)
    wtws, wc, pwr, pwi = _s5_prep(lam_re, lam_im, log_dt, b_re, b_im, c_re, c_im, n_scan)
    p = SSM_STATE
    zero = jnp.zeros_like(wc[..., :p])
    pair_rows = lambda w: jnp.concatenate([jnp.concatenate([w, zero], -1)[:, 0::2],
                                           jnp.concatenate([zero, w], -1)[:, 1::2]], axis=2)
    wcp = jnp.stack([pair_rows(wc[..., :p]), pair_rows(wc[..., p:])], axis=2)
    pair_lanes = lambda a: jnp.concatenate([a[:, 0::2, :, :p], a[:, 1::2, :, p:]], axis=-1)
    pwr, pwi = pair_lanes(pwr), pair_lanes(pwi)

    xp = x.reshape(b, n_chunk, CHUNK, d).swapaxes(1, 2)
    for layer in range(depth):
        o1 = _ac_layer(layer, xp, cos_t, sin_t, sinks, norm_g3, w_in_act, gq, gk, mk, mvt, gxq, w_out_ac)
        xp = _ssm_layer(layer, xp, o1, norm_g3, w_in_bt, wtws, wcp, pwr, pwi, d_skip3, w_glu_t, b_glu3, w_out_b,
                        n_scan)
    return xp.swapaxes(1, 2).reshape(b, s, d)
```

```python
import functools
import math

import jax
import jax.numpy as jnp
from jax import lax
from jax.experimental import pallas as pl
from jax.experimental.pallas import tpu as pltpu

F32 = jnp.float32
BF16 = jnp.bfloat16

EPS = 1e-6
ROPE_THETA = 10000.0
NEG_INF = -1e30
LOG2E = math.log2(math.e)

HEAD_DIM = 64
N_Q_HEADS = 8
N_KV_HEADS = 2
GQA_GROUP = N_Q_HEADS // N_KV_HEADS
BLOCK = 128
ATTN_WIDTH = N_Q_HEADS * HEAD_DIM
KV_WIDTH = N_KV_HEADS * HEAD_DIM
SSM_CH = 16
SSM_GROUPS = 32
SSM_STATE = 64
SSM_WIDTH = SSM_GROUPS * SSM_CH
X_HEADS = 4
X_HEAD_DIM = 128
X_WIDTH = X_HEADS * X_HEAD_DIM

LANES = 128
CHUNK = 16
CK = CHUNK * SSM_CH
ROWS = BLOCK // CHUNK
TQ = 1024
SUB = 512
SPS = 2
OUT_CHUNKS = 4
PAIRS = 4
VMEM_LIMIT = 56 * 1024 * 1024

_OFF_Q = 0
_OFF_K = _OFF_Q + ATTN_WIDTH
_OFF_V = _OFF_K + KV_WIDTH
_OFF_AG = _OFF_V + KV_WIDTH
_OFF_XQ = _OFF_AG + ATTN_WIDTH
_OFF_XG = _OFF_XQ + X_WIDTH
AC_WIDTH = _OFF_XG + X_WIDTH


def _sigmoid(v):
    return 1.0 / (1.0 + jnp.exp(-v))


def _silu(v):
    return v * _sigmoid(v)


def _rms_rows(v, gain):
    return v * lax.rsqrt(jnp.mean(v * v, axis=-1, keepdims=True) + EPS) * gain


def _dot_nt(a, b):
    return lax.dot_general(a, b, (((1,), (1,)), ((), ())), preferred_element_type=F32)


def _rope_kernel(pos_ref, inv_ref, cos_ref, sin_ref):
    ang = inv_ref[...] * pos_ref[0].astype(F32)
    row = lax.broadcasted_iota(jnp.int32, ang.shape, 0)
    cos_ref[0] = jnp.cos(ang)
    sin_ref[0] = jnp.where(row < HEAD_DIM // 2, -jnp.sin(ang), jnp.sin(ang))


def _rope_tables(positions):
    b, s = positions.shape
    half = HEAD_DIM // 2
    inv = ROPE_THETA ** (-jnp.arange(half, dtype=F32) / half)
    inv = jnp.tile(inv, 2).reshape(HEAD_DIM, 1)
    ts = min(s, 2048)
    spec = pl.BlockSpec((1, HEAD_DIM, ts), lambda i, j: (i, 0, j))
    return pl.pallas_call(
        _rope_kernel,
        out_shape=(jax.ShapeDtypeStruct((b, HEAD_DIM, s), F32),) * 2,
        grid=(b, s // ts),
        in_specs=[pl.BlockSpec((1, 1, ts), lambda i, j: (i, 0, j)),
                  pl.BlockSpec((HEAD_DIM, 1), lambda i, j: (0, 0))],
        out_specs=(spec, spec),
        name="rope_tables",
    )(positions.reshape(b, 1, s), inv)


def _memkv_kernel(mem_ref, g_ref, w_ref, gk_ref, mk_ref, mv_ref):
    h = _rms_rows(mem_ref[0], g_ref[0]).astype(BF16)
    kv = jnp.dot(h, w_ref[0], preferred_element_type=F32)
    for hd in range(X_HEADS):
        sl = slice(hd * X_HEAD_DIM, (hd + 1) * X_HEAD_DIM)
        mk_ref[0, 0, :, sl] = _rms_rows(kv[:, sl], gk_ref[0]).astype(BF16)
    mv_ref[0, 0] = kv[:, X_WIDTH:].T.astype(BF16)


def _mem_kv(mem, mem_norm_g, w_mem_kv, xk_norm_g):
    b, n_mem, d = mem.shape
    depth = w_mem_kv.shape[0]
    out = jax.ShapeDtypeStruct((depth, b, n_mem, X_WIDTH), BF16)
    ospec = pl.BlockSpec((1, 1, n_mem, X_WIDTH), lambda l, i: (l, i, 0, 0))
    out_t = jax.ShapeDtypeStruct((depth, b, X_WIDTH, n_mem), BF16)
    ospec_t = pl.BlockSpec((1, 1, X_WIDTH, n_mem), lambda l, i: (l, i, 0, 0))
    return pl.pallas_call(
        _memkv_kernel,
        out_shape=(out, out_t),
        grid=(depth, b),
        in_specs=[pl.BlockSpec((1, n_mem, d), lambda l, i: (i, 0, 0)),
                  pl.BlockSpec((1, 1, d), lambda l, i: (l, 0, 0)),
                  pl.BlockSpec((1, d, 2 * X_WIDTH), lambda l, i: (l, 0, 0)),
                  pl.BlockSpec((1, 1, X_HEAD_DIM), lambda l, i: (l, 0, 0))],
        out_specs=(ospec, ospec_t),
        name="mem_kv",
    )(mem, mem_norm_g.reshape(depth, 1, d), w_mem_kv.astype(BF16), xk_norm_g.reshape(depth, 1, X_HEAD_DIM))


def _s5_prep_kernel(lr_ref, li_ref, ldt_ref, brt_ref, bit_ref, crt_ref, cit_ref,
                    wtws_ref, wcp_ref, pwr_ref, pwi_ref, *, n_scan):
    n_groups = lr_ref.shape[1]
    tab_row = lax.broadcasted_iota(jnp.int32, (3 * CHUNK, LANES), 0)
    n_tab = jnp.where(tab_row < 2 * CHUNK, tab_row, 3 * CHUNK - 1 - tab_row).astype(F32)
    n_scan_rows = jnp.left_shift(jnp.int32(CHUNK), jnp.minimum(
        lax.broadcasted_iota(jnp.int32, (pwr_ref.shape[2], LANES), 0), n_scan - 1)).astype(F32)
    low = lax.broadcasted_iota(jnp.int32, (CK, LANES), 1) < SSM_STATE
    lane_s = lax.broadcasted_iota(jnp.int32, (CK, CK), 1) // SSM_CH
    hi = lax.Precision.HIGHEST

    def per_chunk_row(tab, first):
        rows = tab[first:first + CHUNK]
        return jnp.broadcast_to(rows[:, None, :], (CHUNK, SSM_CH, LANES)).reshape(CK, LANES)

    low_rows = low[:pwr_ref.shape[2]]

    def group(g, gp, e):
        own = low if e == 0 else jnp.logical_not(low)
        dt = jnp.exp(ldt_ref[0, g])
        lr, li = lr_ref[0, g], li_ref[0, g]

        def power(n):
            mag = jnp.exp(lr * dt * n)
            return mag * jnp.cos(li * dt * n), mag * jnp.sin(li * dt * n)

        tab_r, tab_i = power(n_tab)
        ar, ai = tab_r[1:2], tab_i[1:2]
        den = lr * lr + li * li
        fr = ((ar - 1.0) * lr + ai * li) / den
        fi = (ai * lr - (ar - 1.0) * li) / den
        brt, bit = brt_ref[0, g], bit_ref[0, g]
        bbr = fr * brt - fi * bit
        bbi = fr * bit + fi * brt

        crt, cit = crt_ref[0, g], cit_ref[0, g]

        def c_times_power(first):
            pr, pi = per_chunk_row(tab_r, first), per_chunk_row(tab_i, first)
            return crt * pr - cit * pi, -(crt * pi + cit * pr)

        g_re, g_mim = c_times_power(0)
        d = lax.dot_general(jnp.where(low, g_re, g_mim), jnp.where(low, bbr, bbi), (((1,), (1,)), ((), ())),
                            precision=hi, preferred_element_type=F32)
        wt = jnp.where(lane_s == 0, d, 0.0)
        for s in range(1, CHUNK):
            shifted = jnp.concatenate([jnp.zeros((s * SSM_CH, CK), F32), d[:CK - s * SSM_CH]], axis=0)
            wt = jnp.where(lane_s == s, shifted, wt)
        wtws_ref[0, g, 0:CK, :] = wt.astype(BF16)

        qr, qi = per_chunk_row(tab_r, 2 * CHUNK), per_chunk_row(tab_i, 2 * CHUNK)
        ws = jnp.where(low, qr * bbr - qi * bbi, qr * bbi + qi * bbr)
        wtws_ref[0, g, CK:, :] = ws.T.astype(BF16)

        c_re, c_mim = c_times_power(1)
        wcp_ref[0, gp, 0, e * CK:(e + 1) * CK, :] = jnp.where(own, c_re, 0.0).astype(BF16)
        wcp_ref[0, gp, 1, e * CK:(e + 1) * CK, :] = jnp.where(own, c_mim, 0.0).astype(BF16)

        return power(n_scan_rows)

    def pair(gp, carry):
        (r0, i0), (r1, i1) = group(2 * gp, gp, 0), group(2 * gp + 1, gp, 1)
        pwr_ref[0, gp] = jnp.where(low_rows, r0, r1)
        pwi_ref[0, gp] = jnp.where(low_rows, i0, i1)
        return carry

    lax.fori_loop(0, n_groups // 2, pair, 0)


def _s5_prep(lam_re, lam_im, log_dt, b_re, b_im, c_re, c_im, n_scan):
    depth, g, p = lam_re.shape
    twice = lambda a: jnp.concatenate([a, a], axis=-1)
    row = lambda a: twice(a).reshape(depth, g, 1, 2 * p)
    tile_bt = lambda a: twice(jnp.tile(jnp.swapaxes(a, 2, 3), (1, 1, CHUNK, 1)))
    tile_c = lambda a: twice(jnp.tile(a, (1, 1, CHUNK, 1)))
    n_rows = -(-n_scan // 8) * 8
    gb = 8
    spec = lambda *shape: pl.BlockSpec((1, gb) + shape, lambda l, i: (l, i, 0, 0))
    pair_spec = lambda *shape: pl.BlockSpec((1, gb // 2) + shape, lambda l, i: (l, i) + (0,) * len(shape))
    return pl.pallas_call(
        functools.partial(_s5_prep_kernel, n_scan=n_scan),
        out_shape=(jax.ShapeDtypeStruct((depth, g, CK + 2 * p, CK), BF16),
                   jax.ShapeDtypeStruct((depth, g // 2, 2, 2 * CK, 2 * p), BF16),
                   jax.ShapeDtypeStruct((depth, g // 2, n_rows, 2 * p), F32),
                   jax.ShapeDtypeStruct((depth, g // 2, n_rows, 2 * p), F32)),
        grid=(depth, g // gb),
        in_specs=[spec(1, 2 * p), spec(1, 2 * p), spec(1, 1),
                  spec(CK, 2 * p), spec(CK, 2 * p), spec(CK, 2 * p), spec(CK, 2 * p)],
        out_specs=(spec(CK + 2 * p, CK), pair_spec(2, 2 * CK, 2 * p), pair_spec(n_rows, 2 * p),
                   pair_spec(n_rows, 2 * p)),
        compiler_params=pltpu.CompilerParams(vmem_limit_bytes=VMEM_LIMIT),
        name="s5_prep",
    )(row(lam_re), row(lam_im), log_dt.reshape(depth, g, 1, 1),
      tile_bt(b_re), tile_bt(b_im), tile_c(c_re), tile_c(c_im))


def _col_rms(v):
    return lax.rsqrt(jnp.mean(v * v, axis=0, keepdims=True) + EPS)


def _ac_kernel(sinks_ref, x_ref, cos_ref, sin_ref, ng_ref, win_ref, gq_ref, gk_ref, mk_ref, mvt_ref,
               gxq_ref, wout_ref, out_ref, k_ref, v_ref, *sub_refs, layer):
    i = pl.program_id(1)
    d_model = x_ref.shape[-1]
    half = HEAD_DIM // 2
    qscale = LOG2E / math.sqrt(HEAD_DIM)
    xscale = LOG2E / math.sqrt(X_HEAD_DIM)
    n_mem = mk_ref.shape[2]

    @pl.when(i == 0)
    def _():
        k_ref[0:BLOCK] = jnp.zeros((BLOCK, LANES), BF16)
        v_ref[:, 0:BLOCK] = jnp.zeros((KV_WIDTH, BLOCK), BF16)

    key_row = lax.broadcasted_iota(jnp.int32, (2 * BLOCK, BLOCK), 0)
    q_col = lax.broadcasted_iota(jnp.int32, (2 * BLOCK, BLOCK), 1)
    local_tok = lambda rho: (rho % ROWS) * CHUNK + rho // ROWS
    qi = local_tok(q_col)
    kj = (key_row // BLOCK) * BLOCK + local_tok(key_row % BLOCK)
    band = (kj >= qi + 1) & (kj <= qi + BLOCK)
    band_first = band & (kj >= jnp.where(i == 0, BLOCK, 0))
    ones_rows = jnp.ones((2 * ROWS, 2 * BLOCK), BF16)
    ones_mem = jnp.ones((2 * ROWS, n_mem), BF16)
    zeros_q = jnp.zeros((HEAD_DIM, GQA_GROUP * BLOCK), BF16)

    def tokens(blocks):
        return jnp.concatenate([x_ref[0, :, r * ROWS:(r + 1) * ROWS, :].reshape(BLOCK, d_model) for r in blocks],
                               axis=0)

    n_sub = TQ // SUB
    z_refs, q_refs, mix_refs, h_refs = (sub_refs[k * n_sub:(k + 1) * n_sub] for k in range(4))

    def part(refs, rows, lanes):
        sub = lanes.start // SUB
        return refs[sub].at[rows, lanes.start - sub * SUB:lanes.stop - sub * SUB]

    lanes_of = lambda sub: slice(sub * SUB, (sub + 1) * SUB)
    blocks_of = lambda sub: range(sub * SUB // BLOCK, (sub + 1) * SUB // BLOCK)
    proj_rows = (slice(0, _OFF_AG), slice(_OFF_AG, _OFF_XQ), slice(_OFF_XQ, _OFF_XG), slice(_OFF_XG, AC_WIDTH))
    out_cols = tuple(slice(c * d_model // OUT_CHUNKS, (c + 1) * d_model // OUT_CHUNKS) for c in range(OUT_CHUNKS))

    def hidden(sub):
        h_refs[sub][...] = _rms_rows(tokens(blocks_of(sub)), ng_ref[0]).astype(BF16)

    def project(sub, c):
        z_refs[sub][proj_rows[c], :] = _dot_nt(win_ref[0, proj_rows[c], :], h_refs[sub][...])

    def qkv(sub):
        lanes = lanes_of(sub)
        cos_t, sin_t = cos_ref[0, :, lanes], sin_ref[0, :, lanes]

        def head_norm_rope(v, gain):
            vn = v * _col_rms(v) * gain
            return vn * cos_t + jnp.concatenate([vn[half:], vn[:half]], axis=0) * sin_t

        kt = jnp.concatenate(
            [head_norm_rope(z_refs[sub][_OFF_K + hk * HEAD_DIM:_OFF_K + (hk + 1) * HEAD_DIM, :], gk_ref[0, :, 0:SUB])
             for hk in range(N_KV_HEADS)], axis=0)
        k_ref[BLOCK + sub * SUB:BLOCK + (sub + 1) * SUB] = kt.T.astype(BF16)
        v_ref[:, BLOCK + sub * SUB:BLOCK + (sub + 1) * SUB] = z_refs[sub][_OFF_V:_OFF_V + KV_WIDTH, :].astype(BF16)
        for hq in range(N_Q_HEADS):
            rows = slice(hq * HEAD_DIM, (hq + 1) * HEAD_DIM)
            q_refs[sub][rows, :] = (head_norm_rope(z_refs[sub][rows, :], gq_ref[0, :, 0:SUB]) * qscale).astype(BF16)

    def attend(r, hk):
        cols = slice(r * BLOCK, (r + 1) * BLOCK)
        keys = k_ref[r * BLOCK:(r + 2) * BLOCK]
        valid = band_first if r == 0 else band
        heads = range(hk * GQA_GROUP, (hk + 1) * GQA_GROUP)
        qt = jnp.concatenate([part(q_refs, slice(hq * HEAD_DIM, (hq + 1) * HEAD_DIM), cols)[...] for hq in heads],
                             axis=1)
        qt = jnp.concatenate([qt, zeros_q] if hk == 0 else [zeros_q, qt], axis=0)
        sc = jnp.dot(keys, qt, preferred_element_type=F32)
        pt, esink = [], []
        for g, hq in enumerate(heads):
            s_h = jnp.where(valid, sc[:, g * BLOCK:(g + 1) * BLOCK], NEG_INF)
            sink = sinks_ref[layer, hq] * LOG2E
            m = jnp.maximum(jnp.max(s_h, axis=0, keepdims=True), sink)
            pt.append(jnp.exp2(s_h - m).astype(BF16))
            esink.append(jnp.exp2(sink - m))
        vals = jnp.concatenate(
            [v_ref[hk * HEAD_DIM:(hk + 1) * HEAD_DIM, r * BLOCK:(r + 2) * BLOCK], ones_rows], axis=0)
        o = jnp.dot(vals, jnp.concatenate(pt, axis=1), preferred_element_type=F32)
        inv_den = 1.0 / (o[HEAD_DIM:HEAD_DIM + 1] + jnp.concatenate(esink, axis=1))
        o = o[:HEAD_DIM] * inv_den
        for g, hq in enumerate(heads):
            rows = slice(hq * HEAD_DIM, (hq + 1) * HEAD_DIM)
            gate = part(z_refs, slice(_OFF_AG + hq * HEAD_DIM, _OFF_AG + (hq + 1) * HEAD_DIM), cols)[...]
            part(mix_refs, rows, cols)[...] = (o[:, g * BLOCK:(g + 1) * BLOCK] * _silu(gate)).astype(BF16)

    def cross(sub, hd):
        lanes = lanes_of(sub)
        rows = slice(hd * X_HEAD_DIM, (hd + 1) * X_HEAD_DIM)
        xq = z_refs[sub][_OFF_XQ + hd * X_HEAD_DIM:_OFF_XQ + (hd + 1) * X_HEAD_DIM, :]
        xq = (xq * _col_rms(xq) * gxq_ref[0, :, 0:SUB] * xscale).astype(BF16)
        sc = jnp.dot(mk_ref[0, 0, :, rows], xq, preferred_element_type=F32)
        p = jnp.exp2(sc - jnp.max(sc, axis=0, keepdims=True)).astype(BF16)
        vals = jnp.concatenate([mvt_ref[0, 0, rows, :], ones_mem], axis=0)
        o = jnp.dot(vals, p, preferred_element_type=F32)
        o = o[:X_HEAD_DIM] * (1.0 / o[X_HEAD_DIM:X_HEAD_DIM + 1])
        gate = z_refs[sub][_OFF_XG + hd * X_HEAD_DIM:_OFF_XG + (hd + 1) * X_HEAD_DIM, :]
        mix_refs[sub][ATTN_WIDTH + hd * X_HEAD_DIM:ATTN_WIDTH + (hd + 1) * X_HEAD_DIM, :] = (
            o * _silu(gate)).astype(BF16)

    def output(sub, c):
        cols = out_cols[c]
        width = d_model // OUT_CHUNKS
        y = lax.dot_general(mix_refs[sub][...], wout_ref[0, :, cols], (((0,), (0,)), ((), ())),
                            preferred_element_type=F32)
        for n, r in enumerate(blocks_of(sub)):
            xr = x_ref[0, :, r * ROWS:(r + 1) * ROWS, cols]
            out_ref[0, :, r * ROWS:(r + 1) * ROWS, cols] = xr + y[n * BLOCK:(n + 1) * BLOCK].reshape(
                CHUNK, ROWS, width)

    hidden(0)
    project(0, 0)
    qkv(0)
    for c in range(1, len(proj_rows)):
        project(0, c)
    for sub in range(n_sub):
        vec = [functools.partial(attend, r, hk) for r in blocks_of(sub) for hk in range(N_KV_HEADS)]
        vec += [functools.partial(cross, sub, hd) for hd in range(X_HEADS)]
        mxu = []
        if sub + 1 < n_sub:
            hidden(sub + 1)
            mxu += [functools.partial(project, sub + 1, c) for c in range(len(proj_rows))]
        if sub >= 1:
            mxu += [functools.partial(output, sub - 1, c) for c in range(OUT_CHUNKS)]
        every = -(-len(vec) // max(len(mxu), 1))
        for n, task in enumerate(vec):
            if n % every == 0 and mxu:
                mxu.pop(0)()
            task()
        for task in mxu:
            task()
        if sub + 1 < n_sub:
            qkv(sub + 1)
    for c in range(OUT_CHUNKS):
        output(n_sub - 1, c)

    k_ref[0:BLOCK] = k_ref[TQ:TQ + BLOCK]
    v_ref[:, 0:BLOCK] = v_ref[:, TQ:TQ + BLOCK]


def _ac_layer(layer, x, cos_t, sin_t, sinks, norm_g, w_in_act, gq, gk, mk, mvt, gxq, w_out_ac):
    b, _, n_chunk, d = x.shape
    n_mem = mk.shape[2]
    n_sub = TQ // SUB
    tab =pl.BlockSpec((1, HEAD_DIM, TQ), lambda bi, i: (bi, 0, i))
    xspec = pl.BlockSpec((1, CHUNK, TQ // CHUNK, d), lambda bi, i: (bi, 0, i, 0))
    per_layer = lambda *shape: pl.BlockSpec((1,) + shape, lambda bi, i: (layer,) + (0,) * len(shape),
                                            pipeline_mode=pl.Buffered(1))
    return pl.pallas_call(
        functools.partial(_ac_kernel, layer=layer),
        out_shape=jax.ShapeDtypeStruct(x.shape, F32),
        grid=(b, n_chunk * CHUNK // TQ),
        in_specs=[pl.BlockSpec(memory_space=pltpu.SMEM),
                  xspec, tab, tab,
                  per_layer(1, d), per_layer(AC_WIDTH, d), per_layer(HEAD_DIM, TQ), per_layer(HEAD_DIM, TQ),
                  pl.BlockSpec((1, 1, n_mem, X_WIDTH), lambda bi, i: (layer, bi, 0, 0)),
                  pl.BlockSpec((1, 1, X_WIDTH, n_mem), lambda bi, i: (layer, bi, 0, 0)),
                  per_layer(X_HEAD_DIM, TQ), per_layer(ATTN_WIDTH + X_WIDTH, d)],
        out_specs=xspec,
        scratch_shapes=([pltpu.VMEM((BLOCK + TQ, LANES), BF16),
                         pltpu.VMEM((KV_WIDTH, BLOCK + TQ), BF16)]
                        + [pltpu.VMEM((AC_WIDTH, SUB), F32)] * n_sub
                        + [pltpu.VMEM((ATTN_WIDTH, SUB), BF16)] * n_sub
                        + [pltpu.VMEM((ATTN_WIDTH + X_WIDTH, SUB), BF16)] * n_sub
                        + [pltpu.VMEM((SUB, d), BF16)] * n_sub),
        compiler_params=pltpu.CompilerParams(dimension_semantics=("arbitrary", "arbitrary"),
                                             vmem_limit_bytes=VMEM_LIMIT),
        name=f"attn_layer{layer}",
    )(sinks, x, cos_t, sin_t, norm_g, w_in_act, gq, gk, mk, mvt, gxq, w_out_ac)


def _ssm_kernel(x_ref, o1_ref, ng_ref, winb_ref, wtws_ref, wcp_ref, pwr_ref, pwi_ref, dsk_ref, wglu_ref, bglu_ref,
                woutb_ref, out_ref, z_ref, y_ref, sg_ref, hs_ref, *, n_chunk, n_scan):
    step = pl.program_id(1)
    n_proj = CHUNK // SPS
    pad = n_chunk // 2
    d_model = x_ref.shape[-1]

    @pl.when(step < n_proj)
    def _project():
        h = _rms_rows(x_ref[0].reshape(SPS * n_chunk, d_model), ng_ref[0]).astype(BF16)
        ut = _dot_nt(winb_ref[0], h)
        for e in range(SPS):
            s = SPS * step + e
            u = ut[:SSM_WIDTH, e * n_chunk:(e + 1) * n_chunk]
            z_ref[s] = u.astype(BF16).reshape(SSM_GROUPS, SSM_CH, n_chunk)
            y_ref[s] = (u * dsk_ref[0]).reshape(SSM_GROUPS, SSM_CH, n_chunk)
            sg_ref[s] = _silu(ut[SSM_WIDTH:, e * n_chunk:(e + 1) * n_chunk])

    @pl.when(step == n_proj - 1)
    def _chunks():
        hs_ref[:, :, 0:pad, :] = jnp.zeros((PAIRS, 2, pad, LANES), F32)
        rows = pl.ds(pad, n_chunk)

        def pairs(it, carry):
            gps = [PAIRS * it + k for k in range(PAIRS)]
            zs = [[z_ref[:, 2 * gp + e].reshape(CK, n_chunk) for e in range(2)] for gp in gps]
            hr, hi = [], []
            for k, gp in enumerate(gps):
                s = [jnp.dot(wtws_ref[0, 2 * gp + e, CK:, :], zs[k][e], preferred_element_type=F32)
                     for e in range(2)]
                hr.append(jnp.concatenate([s[0][:SSM_STATE], s[1][:SSM_STATE]], axis=0).T)
                hi.append(jnp.concatenate([s[0][SSM_STATE:], s[1][SSM_STATE:]], axis=0).T)
            for j in range(n_scan + 1):
                shift = (1 << j) if j < n_scan else 1
                for k, gp in enumerate(gps):
                    hs_ref[k, 0, rows, :] = hr[k]
                    hs_ref[k, 1, rows, :] = hi[k]
                for k, gp in enumerate(gps):
                    sr = hs_ref[k, 0, pl.ds(pad - shift, n_chunk), :]
                    si = hs_ref[k, 1, pl.ds(pad - shift, n_chunk), :]
                    if j < n_scan:
                        ar, ai = pwr_ref[0, gp, j:j + 1, :], pwi_ref[0, gp, j:j + 1, :]
                        hr[k], hi[k] = hr[k] + ar * sr - ai * si, hi[k] + ar * si + ai * sr
                    else:
                        yc = (_dot_nt(wcp_ref[0, gp, 0], sr.astype(BF16))
                              + _dot_nt(wcp_ref[0, gp, 1], si.astype(BF16)))
                        for e in range(2):
                            y = jnp.dot(wtws_ref[0, 2 * gp + e, 0:CK, :], zs[k][e], preferred_element_type=F32)
                            y = y + yc[e * CK:(e + 1) * CK]
                            y_ref[:, 2 * gp + e] = y_ref[:, 2 * gp + e] + y.reshape(CHUNK, SSM_CH, n_chunk)
            return carry

        lax.fori_loop(0, SSM_GROUPS // (2 * PAIRS), pairs, 0)

    @pl.when(step >= n_proj)
    def _finish():
        t0 = SPS * (step - n_proj)
        y = jnp.concatenate([y_ref[t0 + e].reshape(SSM_WIDTH, n_chunk) for e in range(SPS)], axis=1)
        sg = jnp.concatenate([sg_ref[t0 + e] for e in range(SPS)], axis=1)
        y = jax.nn.gelu(y)
        gate = _sigmoid(jnp.dot(wglu_ref[0], y.astype(BF16), preferred_element_type=F32) + bglu_ref[0])
        ob = (y * gate * sg).T.astype(BF16)
        res = o1_ref[0].reshape(SPS * n_chunk, d_model) + jnp.dot(ob, woutb_ref[0], preferred_element_type=F32)
        out_ref[0] = res.reshape(SPS, n_chunk, d_model)


def _ssm_layer(layer, x, o1, norm_g, w_in_bt, wtws, wcp, pwr, pwi, d_skip, w_glu_t, b_glu, w_out_b, n_scan):
    b, _, n_chunk, d = x.shape
    n_proj = CHUNK // SPS
    per_layer = lambda *shape: pl.BlockSpec((1,) + shape, lambda bi, st: (layer,) + (0,) * len(shape),
                                            pipeline_mode=pl.Buffered(1))
    x_spec = pl.BlockSpec((1, SPS, n_chunk, d), lambda bi, st: (bi, jnp.minimum(st, n_proj - 1), 0, 0))
    o_spec = pl.BlockSpec((1, SPS, n_chunk, d), lambda bi, st: (bi, jnp.maximum(st - n_proj, 0), 0, 0))
    return pl.pallas_call(
        functools.partial(_ssm_kernel, n_chunk=n_chunk, n_scan=n_scan),
        out_shape=jax.ShapeDtypeStruct(x.shape, F32),
        grid=(b, 2 * n_proj),
        in_specs=[x_spec, o_spec,
                  per_layer(1, d), per_layer(2 * SSM_WIDTH, d),
                  per_layer(SSM_GROUPS, CK + 2 * SSM_STATE, CK),
                  per_layer(SSM_GROUPS // 2, 2, 2 * CK, LANES),
                  per_layer(*pwr.shape[1:]), per_layer(*pwi.shape[1:]),
                  per_layer(SSM_WIDTH, 1), per_layer(SSM_WIDTH, SSM_WIDTH), per_layer(SSM_WIDTH, 1),
                  per_layer(SSM_WIDTH, d)],
        out_specs=o_spec,
        scratch_shapes=[pltpu.VMEM((CHUNK, SSM_GROUPS, SSM_CH, n_chunk), BF16),
                        pltpu.VMEM((CHUNK, SSM_GROUPS, SSM_CH, n_chunk), F32),
                        pltpu.VMEM((CHUNK, SSM_WIDTH, n_chunk), F32),
                        pltpu.VMEM((PAIRS, 2, n_chunk // 2 + n_chunk, LANES), F32)],
        compiler_params=pltpu.CompilerParams(dimension_semantics=("arbitrary", "arbitrary"),
                                             vmem_limit_bytes=VMEM_LIMIT),
        name=f"ssm_layer{layer}",
    )(x, o1, norm_g, w_in_bt, wtws, wcp, pwr, pwi, d_skip, w_glu_t, b_glu, w_out_b)


def kernel(x, mem, positions, norm_g, w_in, q_norm_g, k_norm_g, sinks, lam_re, lam_im, log_dt, b_re, b_im,
           c_re, c_im, d_skip, w_glu, b_glu, mem_norm_g, w_mem_kv, xq_norm_g, xk_norm_g, w_out):
    b, s, d = x.shape
    depth = w_in.shape[0]
    assert s % TQ == 0 and (s // CHUNK) % LANES == 0
    n_scan = (s // CHUNK - 1).bit_length()

    o_su = _OFF_AG + ATTN_WIDTH
    o_xq = o_su + 2 * SSM_WIDTH
    w_in_act = jnp.swapaxes(jnp.concatenate([w_in[:, :, :o_su], w_in[:, :, o_xq:]], axis=-1), 1, 2).astype(BF16)
    w_in_bt = jnp.swapaxes(w_in[:, :, o_su:o_xq], 1, 2).astype(BF16)
    w_out_ac = jnp.concatenate([w_out[:, :ATTN_WIDTH], w_out[:, ATTN_WIDTH + SSM_WIDTH:]], axis=1).astype(BF16)
    w_out_b = w_out[:, ATTN_WIDTH:ATTN_WIDTH + SSM_WIDTH].astype(BF16)
    w_glu_t = jnp.swapaxes(w_glu, 1, 2).astype(BF16)
    norm_g3 = norm_g.reshape(depth, 1, d)
    over_tokens = lambda g: jnp.broadcast_to(g[:, :, None], g.shape + (TQ,))
    gq, gk, gxq = over_tokens(q_norm_g), over_tokens(k_norm_g), over_tokens(xq_norm_g)
    d_skip3 = d_skip.reshape(depth, SSM_WIDTH, 1)
    b_glu3 = b_glu.reshape(depth, SSM_WIDTH, 1)

    n_chunk = s // CHUNK
    pos_blocks = positions.reshape(b, s // BLOCK, ROWS, CHUNK).swapaxes(2, 3).reshape(b, s)
    cos_t, sin_t = _rope_tables(pos_blocks)
    mk, mvt = _mem_kv(mem, mem_norm_g, w_mem_kv, xk_norm_g)
    wtws, wcp, pwr, pwi = _s5_prep(lam_re, lam_im, log_dt, b_re, b_im, c_re, c_im, n_scan)

    xp = x.reshape(b, n_chunk, CHUNK, d).swapaxes(1, 2)
    for layer in range(depth):
        o1 = _ac_layer(layer, xp, cos_t, sin_t, sinks, norm_g3, w_in_act, gq, gk, mk, mvt, gxq, w_out_ac)
        xp = _ssm_layer(layer, xp, o1, norm_g3, w_in_bt, wtws, wcp, pwr, pwi, d_skip3, w_glu_t, b_glu3, w_out_b,
                        n_scan)
    return xp.swapaxes(1, 2).reshape(b, s, d)
```

```python
import functools
import math

import jax
import jax.numpy as jnp
from jax import lax
from jax.experimental import pallas as pl
from jax.experimental.pallas import tpu as pltpu

F32 = jnp.float32
BF16 = jnp.bfloat16

EPS = 1e-6
ROPE_THETA = 10000.0
NEG_INF = -1e30
LOG2E = math.log2(math.e)

HEAD_DIM = 64
N_Q_HEADS = 8
N_KV_HEADS = 2
GQA_GROUP = N_Q_HEADS // N_KV_HEADS
BLOCK = 128
ATTN_WIDTH = N_Q_HEADS * HEAD_DIM
KV_WIDTH = N_KV_HEADS * HEAD_DIM
SSM_CH = 16
SSM_GROUPS = 32
SSM_STATE = 64
SSM_WIDTH = SSM_GROUPS * SSM_CH
X_HEADS = 4
X_HEAD_DIM = 128
X_WIDTH = X_HEADS * X_HEAD_DIM

LANES = 128
CHUNK = 16
CK = CHUNK * SSM_CH
ROWS = BLOCK // CHUNK
TQ = 1024
SUB = 512
SPS = 2
OUT_CHUNKS = 4
PAIRS = 4
VMEM_LIMIT = 56 * 1024 * 1024

_OFF_Q = 0
_OFF_K = _OFF_Q + ATTN_WIDTH
_OFF_V = _OFF_K + KV_WIDTH
_OFF_AG = _OFF_V + KV_WIDTH
_OFF_XQ = _OFF_AG + ATTN_WIDTH
_OFF_XG = _OFF_XQ + X_WIDTH
AC_WIDTH = _OFF_XG + X_WIDTH


def _sigmoid(v):
    return 1.0 / (1.0 + jnp.exp(-v))


def _silu(v):
    return v * _sigmoid(v)


def _rms_rows(v, gain):
    return v * lax.rsqrt(jnp.mean(v * v, axis=-1, keepdims=True) + EPS) * gain


def _dot_nt(a, b):
    return lax.dot_general(a, b, (((1,), (1,)), ((), ())), preferred_element_type=F32)


def _rope_kernel(pos_ref, inv_ref, cos_ref, sin_ref):
    ang = inv_ref[...] * pos_ref[0].astype(F32)
    c, s = jnp.cos(ang), jnp.sin(ang)
    cos_ref[0] = jnp.concatenate([c, c], axis=0)
    sin_ref[0] = jnp.concatenate([-s, s], axis=0)


def _rope_tables(positions):
    b, s = positions.shape
    half = HEAD_DIM // 2
    inv = (ROPE_THETA ** (-jnp.arange(half, dtype=F32) / half)).reshape(half, 1)
    ts = min(s, 2048)
    spec = pl.BlockSpec((1, HEAD_DIM, ts), lambda i, j: (i, 0, j))
    return pl.pallas_call(
        _rope_kernel,
        out_shape=(jax.ShapeDtypeStruct((b, HEAD_DIM, s), F32),) * 2,
        grid=(b, s // ts),
        in_specs=[pl.BlockSpec((1, 1, ts), lambda i, j: (i, 0, j)),
                  pl.BlockSpec((HEAD_DIM // 2, 1), lambda i, j: (0, 0))],
        out_specs=(spec, spec),
        name="rope_tables",
    )(positions.reshape(b, 1, s), inv)


def _memkv_kernel(mem_ref, g_ref, w_ref, gk_ref, mk_ref, mv_ref):
    h = _rms_rows(mem_ref[0], g_ref[0]).astype(BF16)
    kv = jnp.dot(h, w_ref[0], preferred_element_type=F32)
    for hd in range(X_HEADS):
        sl = slice(hd * X_HEAD_DIM, (hd + 1) * X_HEAD_DIM)
        mk_ref[0, 0, :, sl] = _rms_rows(kv[:, sl], gk_ref[0]).astype(BF16)
    mv_ref[0, 0] = kv[:, X_WIDTH:].T.astype(BF16)


def _mem_kv(mem, mem_norm_g, w_mem_kv, xk_norm_g):
    b, n_mem, d = mem.shape
    depth = w_mem_kv.shape[0]
    out = jax.ShapeDtypeStruct((depth, b, n_mem, X_WIDTH), BF16)
    ospec = pl.BlockSpec((1, 1, n_mem, X_WIDTH), lambda l, i: (l, i, 0, 0))
    out_t = jax.ShapeDtypeStruct((depth, b, X_WIDTH, n_mem), BF16)
    ospec_t = pl.BlockSpec((1, 1, X_WIDTH, n_mem), lambda l, i: (l, i, 0, 0))
    return pl.pallas_call(
        _memkv_kernel,
        out_shape=(out, out_t),
        grid=(depth, b),
        in_specs=[pl.BlockSpec((1, n_mem, d), lambda l, i: (i, 0, 0)),
                  pl.BlockSpec((1, 1, d), lambda l, i: (l, 0, 0)),
                  pl.BlockSpec((1, d, 2 * X_WIDTH), lambda l, i: (l, 0, 0)),
                  pl.BlockSpec((1, 1, X_HEAD_DIM), lambda l, i: (l, 0, 0))],
        out_specs=(ospec, ospec_t),
        name="mem_kv",
    )(mem, mem_norm_g.reshape(depth, 1, d), w_mem_kv.astype(BF16), xk_norm_g.reshape(depth, 1, X_HEAD_DIM))


def _s5_prep_kernel(lr_ref, li_ref, ldt_ref, brt_ref, bit_ref, crt_ref, cit_ref,
                    wtws_ref, wcp_ref, pwr_ref, pwi_ref, *, n_scan):
    n_groups = lr_ref.shape[1]
    tab_row = lax.broadcasted_iota(jnp.int32, (3 * CHUNK, LANES), 0)
    n_tab = jnp.where(tab_row < 2 * CHUNK, tab_row, 3 * CHUNK - 1 - tab_row).astype(F32)
    n_scan_rows = jnp.left_shift(jnp.int32(CHUNK), jnp.minimum(
        lax.broadcasted_iota(jnp.int32, (pwr_ref.shape[2], LANES), 0), n_scan - 1)).astype(F32)
    low = lax.broadcasted_iota(jnp.int32, (CK, LANES), 1) < SSM_STATE
    lane_s = lax.broadcasted_iota(jnp.int32, (CK, CK), 1) // SSM_CH
    hi = lax.Precision.HIGHEST

    def per_chunk_row(tab, first):
        rows = tab[first:first + CHUNK]
        return jnp.broadcast_to(rows[:, None, :], (CHUNK, SSM_CH, LANES)).reshape(CK, LANES)

    low_rows = low[:pwr_ref.shape[2]]

    def group(g, gp, e):
        own = low if e == 0 else jnp.logical_not(low)
        dt = jnp.exp(ldt_ref[0, g])
        lr, li = lr_ref[0, g], li_ref[0, g]

        def power(n):
            mag = jnp.exp(lr * dt * n)
            return mag * jnp.cos(li * dt * n), mag * jnp.sin(li * dt * n)

        tab_r, tab_i = power(n_tab)
        ar, ai = tab_r[1:2], tab_i[1:2]
        den = lr * lr + li * li
        fr = ((ar - 1.0) * lr + ai * li) / den
        fi = (ai * lr - (ar - 1.0) * li) / den
        every_chunk_row = lambda a: jnp.concatenate([a] * CHUNK, axis=0)
        brt, bit = every_chunk_row(brt_ref[0, g]), every_chunk_row(bit_ref[0, g])
        bbr = fr * brt - fi * bit
        bbi = fr * bit + fi * brt

        crt, cit = every_chunk_row(crt_ref[0, g]), every_chunk_row(cit_ref[0, g])

        def c_times_power(first):
            pr, pi = per_chunk_row(tab_r, first), per_chunk_row(tab_i, first)
            return crt * pr - cit * pi, -(crt * pi + cit * pr)

        g_re, g_mim = c_times_power(0)
        d = lax.dot_general(jnp.where(low, g_re, g_mim), jnp.where(low, bbr, bbi), (((1,), (1,)), ((), ())),
                            precision=hi, preferred_element_type=F32)
        wt = jnp.where(lane_s == 0, d, 0.0)
        for s in range(1, CHUNK):
            shifted = jnp.concatenate([jnp.zeros((s * SSM_CH, CK), F32), d[:CK - s * SSM_CH]], axis=0)
            wt = jnp.where(lane_s == s, shifted, wt)
        wtws_ref[0, g, 0:CK, :] = wt.astype(BF16)

        qr, qi = per_chunk_row(tab_r, 2 * CHUNK), per_chunk_row(tab_i, 2 * CHUNK)
        ws = jnp.where(low, qr * bbr - qi * bbi, qr * bbi + qi * bbr)
        wtws_ref[0, g, CK:, :] = ws.T.astype(BF16)

        c_re, c_mim = c_times_power(1)
        wcp_ref[0, gp, 0, e * CK:(e + 1) * CK, :] = jnp.where(own, c_re, 0.0).astype(BF16)
        wcp_ref[0, gp, 1, e * CK:(e + 1) * CK, :] = jnp.where(own, c_mim, 0.0).astype(BF16)

        return power(n_scan_rows)

    def pair(gp, carry):
        (r0, i0), (r1, i1) = group(2 * gp, gp, 0), group(2 * gp + 1, gp, 1)
        pwr_ref[0, gp] = jnp.where(low_rows, r0, r1)
        pwi_ref[0, gp] = jnp.where(low_rows, i0, i1)
        return carry

    lax.fori_loop(0, n_groups // 2, pair, 0)


def _s5_prep(lam_re, lam_im, log_dt, b_re, b_im, c_re, c_im, n_scan):
    depth, g, p = lam_re.shape
    twice = lambda a: jnp.concatenate([a, a], axis=-1)
    row = lambda a: twice(a).reshape(depth, g, 1, 2 * p)
    b_t = lambda a: twice(jnp.swapaxes(a, 2, 3))
    c_t = twice
    n_rows = -(-n_scan // 8) * 8
    gb = 8
    spec = lambda *shape: pl.BlockSpec((1, gb) + shape, lambda l, i: (l, i, 0, 0))
    pair_spec = lambda *shape: pl.BlockSpec((1, gb // 2) + shape, lambda l, i: (l, i) + (0,) * len(shape))
    return pl.pallas_call(
        functools.partial(_s5_prep_kernel, n_scan=n_scan),
        out_shape=(jax.ShapeDtypeStruct((depth, g, CK + 2 * p, CK), BF16),
                   jax.ShapeDtypeStruct((depth, g // 2, 2, 2 * CK, 2 * p), BF16),
                   jax.ShapeDtypeStruct((depth, g // 2, n_rows, 2 * p), F32),
                   jax.ShapeDtypeStruct((depth, g // 2, n_rows, 2 * p), F32)),
        grid=(depth, g // gb),
        in_specs=[spec(1, 2 * p), spec(1, 2 * p), spec(1, 1),
                  spec(SSM_CH, 2 * p), spec(SSM_CH, 2 * p), spec(SSM_CH, 2 * p), spec(SSM_CH, 2 * p)],
        out_specs=(spec(CK + 2 * p, CK), pair_spec(2, 2 * CK, 2 * p), pair_spec(n_rows, 2 * p),
                   pair_spec(n_rows, 2 * p)),
        compiler_params=pltpu.CompilerParams(vmem_limit_bytes=VMEM_LIMIT),
        name="s5_prep",
    )(row(lam_re), row(lam_im), log_dt.reshape(depth, g, 1, 1),
      b_t(b_re), b_t(b_im), c_t(c_re), c_t(c_im))


def _col_rms(v):
    return lax.rsqrt(jnp.mean(v * v, axis=0, keepdims=True) + EPS)


def _ac_kernel(sinks_ref, x_ref, ob_ref, cos_ref, sin_ref, ng_ref, win_ref, gq_ref, gk_ref, mk_ref, mvt_ref,
               gxq_ref, wout_ref, woutb_ref, out_ref, k_ref, v_ref, *sub_refs, layer):
    i = pl.program_id(1)
    d_model = x_ref.shape[-1]
    half = HEAD_DIM // 2
    qscale = LOG2E / math.sqrt(HEAD_DIM)
    xscale = LOG2E / math.sqrt(X_HEAD_DIM)
    n_mem = mk_ref.shape[2]

    @pl.when(i == 0)
    def _():
        k_ref[0:BLOCK] = jnp.zeros((BLOCK, LANES), BF16)
        v_ref[:, 0:BLOCK] = jnp.zeros((KV_WIDTH, BLOCK), BF16)

    key_row = lax.broadcasted_iota(jnp.int32, (2 * BLOCK, BLOCK), 0)
    q_col = lax.broadcasted_iota(jnp.int32, (2 * BLOCK, BLOCK), 1)
    local_tok = lambda rho: (rho % ROWS) * CHUNK + rho // ROWS
    qi = local_tok(q_col)
    kj = (key_row // BLOCK) * BLOCK + local_tok(key_row % BLOCK)
    band = (kj >= qi + 1) & (kj <= qi + BLOCK)
    band_first = band & (kj >= jnp.where(i == 0, BLOCK, 0))
    ones_rows = jnp.ones((2 * ROWS, 2 * BLOCK), BF16)
    ones_mem = jnp.ones((2 * ROWS, n_mem), BF16)
    zeros_q = jnp.zeros((HEAD_DIM, GQA_GROUP * BLOCK), BF16)

    def tokens(blocks):
        return jnp.concatenate([x_ref[0, :, r * ROWS:(r + 1) * ROWS, :].reshape(BLOCK, d_model) for r in blocks],
                               axis=0)

    n_sub = TQ // SUB
    z_refs, q_refs, mix_refs, h_refs, ob_refs = (sub_refs[k * n_sub:(k + 1) * n_sub] for k in range(5))

    def part(refs, rows, lanes):
        sub = lanes.start // SUB
        return refs[sub].at[rows, lanes.start - sub * SUB:lanes.stop - sub * SUB]

    lanes_of = lambda sub: slice(sub * SUB, (sub + 1) * SUB)
    blocks_of = lambda sub: range(sub * SUB // BLOCK, (sub + 1) * SUB // BLOCK)
    proj_rows = (slice(0, _OFF_AG), slice(_OFF_AG, _OFF_XQ), slice(_OFF_XQ, _OFF_XG), slice(_OFF_XG, AC_WIDTH))
    out_cols = tuple(slice(c * d_model // OUT_CHUNKS, (c + 1) * d_model // OUT_CHUNKS) for c in range(OUT_CHUNKS))

    def hidden(sub):
        h_refs[sub][...] = _rms_rows(tokens(blocks_of(sub)), ng_ref[0]).astype(BF16)
        ob_refs[sub][...] = jnp.concatenate(
            [ob_ref[0, :, r * ROWS:(r + 1) * ROWS, :].reshape(BLOCK, SSM_WIDTH) for r in blocks_of(sub)],
            axis=0).astype(BF16)

    def project(sub, c):
        z_refs[sub][proj_rows[c], :] = _dot_nt(win_ref[0, proj_rows[c], :], h_refs[sub][...])

    def qkv(sub):
        lanes = lanes_of(sub)
        cos_t, sin_t = cos_ref[0, :, lanes], sin_ref[0, :, lanes]

        def head_norm_rope(v, gain):
            vn = v * _col_rms(v) * gain
            return vn * cos_t + jnp.concatenate([vn[half:], vn[:half]], axis=0) * sin_t

        kt = jnp.concatenate(
            [head_norm_rope(z_refs[sub][_OFF_K + hk * HEAD_DIM:_OFF_K + (hk + 1) * HEAD_DIM, :], gk_ref[0])
             for hk in range(N_KV_HEADS)], axis=0)
        k_ref[BLOCK + sub * SUB:BLOCK + (sub + 1) * SUB] = kt.T.astype(BF16)
        v_ref[:, BLOCK + sub * SUB:BLOCK + (sub + 1) * SUB] = z_refs[sub][_OFF_V:_OFF_V + KV_WIDTH, :].astype(BF16)
        for hq in range(N_Q_HEADS):
            rows = slice(hq * HEAD_DIM, (hq + 1) * HEAD_DIM)
            q_refs[sub][rows, :] = (head_norm_rope(z_refs[sub][rows, :], gq_ref[0]) * qscale).astype(BF16)

    def attend(r, hk):
        cols = slice(r * BLOCK, (r + 1) * BLOCK)
        keys = k_ref[r * BLOCK:(r + 2) * BLOCK]
        valid = band_first if r == 0 else band
        heads = range(hk * GQA_GROUP, (hk + 1) * GQA_GROUP)
        qt = jnp.concatenate([part(q_refs, slice(hq * HEAD_DIM, (hq + 1) * HEAD_DIM), cols)[...] for hq in heads],
                             axis=1)
        qt = jnp.concatenate([qt, zeros_q] if hk == 0 else [zeros_q, qt], axis=0)
        sc = jnp.dot(keys, qt, preferred_element_type=F32)
        pt, esink = [], []
        for g, hq in enumerate(heads):
            s_h = jnp.where(valid, sc[:, g * BLOCK:(g + 1) * BLOCK], NEG_INF)
            sink = sinks_ref[layer, hq] * LOG2E
            m = jnp.maximum(jnp.max(s_h, axis=0, keepdims=True), sink)
            pt.append(jnp.exp2(s_h - m).astype(BF16))
            esink.append(jnp.exp2(sink - m))
        vals = jnp.concatenate(
            [v_ref[hk * HEAD_DIM:(hk + 1) * HEAD_DIM, r * BLOCK:(r + 2) * BLOCK], ones_rows], axis=0)
        o = jnp.dot(vals, jnp.concatenate(pt, axis=1), preferred_element_type=F32)
        inv_den = 1.0 / (o[HEAD_DIM:HEAD_DIM + 1] + jnp.concatenate(esink, axis=1))
        o = o[:HEAD_DIM] * inv_den
        for g, hq in enumerate(heads):
            rows = slice(hq * HEAD_DIM, (hq + 1) * HEAD_DIM)
            gate = part(z_refs, slice(_OFF_AG + hq * HEAD_DIM, _OFF_AG + (hq + 1) * HEAD_DIM), cols)[...]
            part(mix_refs, rows, cols)[...] = (o[:, g * BLOCK:(g + 1) * BLOCK] * _silu(gate)).astype(BF16)

    def cross(sub, hd):
        lanes = lanes_of(sub)
        rows = slice(hd * X_HEAD_DIM, (hd + 1) * X_HEAD_DIM)
        xq = z_refs[sub][_OFF_XQ + hd * X_HEAD_DIM:_OFF_XQ + (hd + 1) * X_HEAD_DIM, :]
        xq = (xq * _col_rms(xq) * gxq_ref[0] * xscale).astype(BF16)
        sc = jnp.dot(mk_ref[0, 0, :, rows], xq, preferred_element_type=F32)
        p = jnp.exp2(sc - jnp.max(sc, axis=0, keepdims=True)).astype(BF16)
        vals = jnp.concatenate([mvt_ref[0, 0, rows, :], ones_mem], axis=0)
        o = jnp.dot(vals, p, preferred_element_type=F32)
        o = o[:X_HEAD_DIM] * (1.0 / o[X_HEAD_DIM:X_HEAD_DIM + 1])
        gate = z_refs[sub][_OFF_XG + hd * X_HEAD_DIM:_OFF_XG + (hd + 1) * X_HEAD_DIM, :]
        mix_refs[sub][ATTN_WIDTH + hd * X_HEAD_DIM:ATTN_WIDTH + (hd + 1) * X_HEAD_DIM, :] = (
            o * _silu(gate)).astype(BF16)

    def output(sub, c):
        cols = out_cols[c]
        width = d_model // OUT_CHUNKS
        y = lax.dot_general(mix_refs[sub][...], wout_ref[0, :, cols], (((0,), (0,)), ((), ())),
                            preferred_element_type=F32)
        y = y + jnp.dot(ob_refs[sub][...], woutb_ref[0, :, cols], preferred_element_type=F32)
        for n, r in enumerate(blocks_of(sub)):
            xr = x_ref[0, :, r * ROWS:(r + 1) * ROWS, cols]
            out_ref[0, :, r * ROWS:(r + 1) * ROWS, cols] = xr + y[n * BLOCK:(n + 1) * BLOCK].reshape(
                CHUNK, ROWS, width)

    hidden(0)
    project(0, 0)
    qkv(0)
    for c in range(1, len(proj_rows)):
        project(0, c)
    for sub in range(n_sub):
        vec = [functools.partial(attend, r, hk) for r in blocks_of(sub) for hk in range(N_KV_HEADS)]
        vec += [functools.partial(cross, sub, hd) for hd in range(X_HEADS)]
        mxu = []
        if sub + 1 < n_sub:
            hidden(sub + 1)
            mxu += [functools.partial(project, sub + 1, c) for c in range(len(proj_rows))]
        if sub >= 1:
            mxu += [functools.partial(output, sub - 1, c) for c in range(OUT_CHUNKS)]
        every = -(-len(vec) // max(len(mxu), 1))
        for n, task in enumerate(vec):
            if n % every == 0 and mxu:
                mxu.pop(0)()
            task()
        for task in mxu:
            task()
        if sub + 1 < n_sub:
            qkv(sub + 1)
    for c in range(OUT_CHUNKS):
        output(n_sub - 1, c)

    k_ref[0:BLOCK] = k_ref[TQ:TQ + BLOCK]
    v_ref[:, 0:BLOCK] = v_ref[:, TQ:TQ + BLOCK]


def _ac_layer(layer, x, ob, cos_t, sin_t, sinks, norm_g, w_in_act, gq, gk, mk, mvt, gxq, w_out_ac, w_out_b):
    b, _, n_chunk, d = x.shape
    n_mem = mk.shape[2]
    n_sub = TQ // SUB
    tab = pl.BlockSpec((1, HEAD_DIM, TQ), lambda bi, i: (bi, 0, i))
    xspec = pl.BlockSpec((1, CHUNK, TQ // CHUNK, d), lambda bi, i: (bi, 0, i, 0))
    obspec = pl.BlockSpec((1, CHUNK, TQ // CHUNK, SSM_WIDTH), lambda bi, i: (bi, 0, i, 0))
    per_layer = lambda *shape: pl.BlockSpec((1,) + shape, lambda bi, i: (layer,) + (0,) * len(shape),
                                            pipeline_mode=pl.Buffered(1))
    return pl.pallas_call(
        functools.partial(_ac_kernel, layer=layer),
        out_shape=jax.ShapeDtypeStruct(x.shape, F32),
        grid=(b, n_chunk * CHUNK // TQ),
        in_specs=[pl.BlockSpec(memory_space=pltpu.SMEM),
                  xspec, obspec, tab, tab,
                  per_layer(1, d), per_layer(AC_WIDTH, d), per_layer(HEAD_DIM, SUB), per_layer(HEAD_DIM, SUB),
                  pl.BlockSpec((1, 1, n_mem, X_WIDTH), lambda bi, i: (layer, bi, 0, 0)),
                  pl.BlockSpec((1, 1, X_WIDTH, n_mem), lambda bi, i: (layer, bi, 0, 0)),
                  per_layer(X_HEAD_DIM, SUB), per_layer(ATTN_WIDTH + X_WIDTH, d), per_layer(SSM_WIDTH, d)],
        out_specs=xspec,
        scratch_shapes=([pltpu.VMEM((BLOCK + TQ, LANES), BF16),
                         pltpu.VMEM((KV_WIDTH, BLOCK + TQ), BF16)]
                        + [pltpu.VMEM((AC_WIDTH, SUB), F32)] * n_sub
                        + [pltpu.VMEM((ATTN_WIDTH, SUB), BF16)] * n_sub
                        + [pltpu.VMEM((ATTN_WIDTH + X_WIDTH, SUB), BF16)] * n_sub
                        + [pltpu.VMEM((SUB, d), BF16)] * n_sub
                        + [pltpu.VMEM((SUB, SSM_WIDTH), BF16)] * n_sub),
        compiler_params=pltpu.CompilerParams(dimension_semantics=("arbitrary", "arbitrary"),
                                             vmem_limit_bytes=VMEM_LIMIT),
        name=f"attn_layer{layer}",
    )(sinks, x, ob, cos_t, sin_t, norm_g, w_in_act, gq, gk, mk, mvt, gxq, w_out_ac, w_out_b)


def _ssm_kernel(x_ref, ng_ref, winb_ref, wtws_ref, wcp_ref, pwr_ref, pwi_ref, dsk_ref, wglu_ref, bglu_ref,
                out_ref, z_ref, y_ref, sg_ref, hs_ref, *, n_chunk, n_scan):
    step = pl.program_id(1)
    n_proj = CHUNK // SPS
    pad = n_chunk // 2
    d_model = x_ref.shape[-1]

    @pl.when(step < n_proj)
    def _project():
        h = _rms_rows(x_ref[0].reshape(SPS * n_chunk, d_model), ng_ref[0]).astype(BF16)
        ut = _dot_nt(winb_ref[0], h)
        for e in range(SPS):
            s = SPS * step + e
            u = ut[:SSM_WIDTH, e * n_chunk:(e + 1) * n_chunk]
            z_ref[s] = u.astype(BF16).reshape(SSM_GROUPS, SSM_CH, n_chunk)
            y_ref[s] = (u * dsk_ref[0]).reshape(SSM_GROUPS, SSM_CH, n_chunk)
            sg_ref[s] = _silu(ut[SSM_WIDTH:, e * n_chunk:(e + 1) * n_chunk])

    @pl.when(step == n_proj - 1)
    def _chunks():
        hs_ref[:, :, 0:pad, :] = jnp.zeros((PAIRS, 2, pad, LANES), F32)
        rows = pl.ds(pad, n_chunk)

        def pairs(it, carry):
            gps = [PAIRS * it + k for k in range(PAIRS)]
            zs = [[z_ref[:, 2 * gp + e].reshape(CK, n_chunk) for e in range(2)] for gp in gps]
            hr, hi = [], []
            for k, gp in enumerate(gps):
                s = [jnp.dot(wtws_ref[0, 2 * gp + e, CK:, :], zs[k][e], preferred_element_type=F32)
                     for e in range(2)]
                hr.append(jnp.concatenate([s[0][:SSM_STATE], s[1][:SSM_STATE]], axis=0).T)
                hi.append(jnp.concatenate([s[0][SSM_STATE:], s[1][SSM_STATE:]], axis=0).T)
            for j in range(n_scan + 1):
                shift = (1 << j) if j < n_scan else 1
                for k, gp in enumerate(gps):
                    hs_ref[k, 0, rows, :] = hr[k]
                    hs_ref[k, 1, rows, :] = hi[k]
                for k, gp in enumerate(gps):
                    sr = hs_ref[k, 0, pl.ds(pad - shift, n_chunk), :]
                    si = hs_ref[k, 1, pl.ds(pad - shift, n_chunk), :]
                    if j < n_scan:
                        ar, ai = pwr_ref[0, gp, j:j + 1, :], pwi_ref[0, gp, j:j + 1, :]
                        hr[k], hi[k] = hr[k] + ar * sr - ai * si, hi[k] + ar * si + ai * sr
                    else:
                        yc = (_dot_nt(wcp_ref[0, gp, 0], sr.astype(BF16))
                              + _dot_nt(wcp_ref[0, gp, 1], si.astype(BF16)))
                        for e in range(2):
                            y = jnp.dot(wtws_ref[0, 2 * gp + e, 0:CK, :], zs[k][e], preferred_element_type=F32)
                            y = y + yc[e * CK:(e + 1) * CK]
                            y_ref[:, 2 * gp + e] = y_ref[:, 2 * gp + e] + y.reshape(CHUNK, SSM_CH, n_chunk)
            return carry

        lax.fori_loop(0, SSM_GROUPS // (2 * PAIRS), pairs, 0)

    @pl.when(step >= n_proj)
    def _finish():
        t0 = SPS * (step - n_proj)
        y = jnp.concatenate([y_ref[t0 + e].reshape(SSM_WIDTH, n_chunk) for e in range(SPS)], axis=1)
        sg = jnp.concatenate([sg_ref[t0 + e] for e in range(SPS)], axis=1)
        y = jax.nn.gelu(y)
        gate = _sigmoid(jnp.dot(wglu_ref[0], y.astype(BF16), preferred_element_type=F32) + bglu_ref[0])
        out_ref[0] = (y * gate * sg).T.reshape(SPS, n_chunk, SSM_WIDTH)


def _ssm_layer(layer, x, norm_g, w_in_bt, wtws, wcp, pwr, pwi, d_skip, w_glu_t, b_glu, n_scan):
    b, _, n_chunk, d = x.shape
    n_proj = CHUNK // SPS
    per_layer = lambda *shape: pl.BlockSpec((1,) + shape, lambda bi, st: (layer,) + (0,) * len(shape),
                                            pipeline_mode=pl.Buffered(1))
    x_spec = pl.BlockSpec((1, SPS, n_chunk, d), lambda bi, st: (bi, jnp.minimum(st, n_proj - 1), 0, 0))
    o_spec = pl.BlockSpec((1, SPS, n_chunk, SSM_WIDTH), lambda bi, st: (bi, jnp.maximum(st - n_proj, 0), 0, 0))
    return pl.pallas_call(
        functools.partial(_ssm_kernel, n_chunk=n_chunk, n_scan=n_scan),
        out_shape=jax.ShapeDtypeStruct(x.shape[:3] + (SSM_WIDTH,), F32),
        grid=(b, 2 * n_proj),
        in_specs=[x_spec,
                  per_layer(1, d), per_layer(2 * SSM_WIDTH, d),
                  per_layer(SSM_GROUPS, CK + 2 * SSM_STATE, CK),
                  per_layer(SSM_GROUPS // 2, 2, 2 * CK, LANES),
                  per_layer(*pwr.shape[1:]), per_layer(*pwi.shape[1:]),
                  per_layer(SSM_WIDTH, 1), per_layer(SSM_WIDTH, SSM_WIDTH), per_layer(SSM_WIDTH, 1)],
        out_specs=o_spec,
        scratch_shapes=[pltpu.VMEM((CHUNK, SSM_GROUPS, SSM_CH, n_chunk), BF16),
                        pltpu.VMEM((CHUNK, SSM_GROUPS, SSM_CH, n_chunk), F32),
                        pltpu.VMEM((CHUNK, SSM_WIDTH, n_chunk), F32),
                        pltpu.VMEM((PAIRS, 2, n_chunk // 2 + n_chunk, LANES), F32)],
        compiler_params=pltpu.CompilerParams(dimension_semantics=("arbitrary", "arbitrary"),
                                             vmem_limit_bytes=VMEM_LIMIT),
        name=f"ssm_layer{layer}",
    )(x, norm_g, w_in_bt, wtws, wcp, pwr, pwi, d_skip, w_glu_t, b_glu)


def kernel(x, mem, positions, norm_g, w_in, q_norm_g, k_norm_g, sinks, lam_re, lam_im, log_dt, b_re, b_im,
           c_re, c_im, d_skip, w_glu, b_glu, mem_norm_g, w_mem_kv, xq_norm_g, xk_norm_g, w_out):
    b, s, d = x.shape
    depth = w_in.shape[0]
    assert s % TQ == 0 and (s // CHUNK) % LANES == 0
    n_scan = (s // CHUNK - 1).bit_length()

    o_su = _OFF_AG + ATTN_WIDTH
    o_xq = o_su + 2 * SSM_WIDTH
    w_in_act = jnp.swapaxes(jnp.concatenate([w_in[:, :, :o_su], w_in[:, :, o_xq:]], axis=-1), 1, 2).astype(BF16)
    w_in_bt = jnp.swapaxes(w_in[:, :, o_su:o_xq], 1, 2).astype(BF16)
    w_out_ac = jnp.concatenate([w_out[:, :ATTN_WIDTH], w_out[:, ATTN_WIDTH + SSM_WIDTH:]], axis=1).astype(BF16)
    w_out_b = w_out[:, ATTN_WIDTH:ATTN_WIDTH + SSM_WIDTH].astype(BF16)
    w_glu_t = jnp.swapaxes(w_glu, 1, 2).astype(BF16)
    norm_g3 = norm_g.reshape(depth, 1, d)
    over_tokens = lambda g: jnp.broadcast_to(g[:, :, None], g.shape + (SUB,))
    gq, gk, gxq = over_tokens(q_norm_g), over_tokens(k_norm_g), over_tokens(xq_norm_g)
    d_skip3 = d_skip.reshape(depth, SSM_WIDTH, 1)
    b_glu3 = b_glu.reshape(depth, SSM_WIDTH, 1)

    n_chunk = s // CHUNK
    pos_blocks = positions.reshape(b, s // BLOCK, ROWS, CHUNK).swapaxes(2, 3).reshape(b, s)
    cos_t, sin_t = _rope_tables(pos_blocks)
    mk, mvt = _mem_kv(mem, mem_norm_g, w_mem_kv, xk_norm_g)
    wtws, wcp, pwr, pwi = _s5_prep(lam_re, lam_im, log_dt, b_re, b_im, c_re, c_im, n_scan)

    xp = x.reshape(b, n_chunk, CHUNK, d).swapaxes(1, 2)
    for layer in range(depth):
        ob = _ssm_layer(layer, xp, norm_g3, w_in_bt, wtws, wcp, pwr, pwi, d_skip3, w_glu_t, b_glu3, n_scan)
        xp = _ac_layer(layer, xp, ob, cos_t, sin_t, sinks, norm_g3, w_in_act, gq, gk, mk, mvt, gxq, w_out_ac,
                       w_out_b)
    return xp.swapaxes(1, 2).reshape(b, s, d)
```

```python
import functools
import math

import jax
import jax.numpy as jnp
from jax import lax
from jax.experimental import pallas as pl
from jax.experimental.pallas import tpu as pltpu

F32 = jnp.float32
BF16 = jnp.bfloat16

EPS = 1e-6
ROPE_THETA = 10000.0
NEG_INF = -1e30
LOG2E = math.log2(math.e)

HEAD_DIM = 64
N_Q_HEADS = 8
N_KV_HEADS = 2
GQA_GROUP = N_Q_HEADS // N_KV_HEADS
BLOCK = 128
ATTN_WIDTH = N_Q_HEADS * HEAD_DIM
KV_WIDTH = N_KV_HEADS * HEAD_DIM
SSM_CH = 16
SSM_GROUPS = 32
SSM_STATE = 64
SSM_WIDTH = SSM_GROUPS * SSM_CH
X_HEADS = 4
X_HEAD_DIM = 128
X_WIDTH = X_HEADS * X_HEAD_DIM

LANES = 128
CHUNK = 16
CK = CHUNK * SSM_CH
ROWS = BLOCK // CHUNK
TQ = 1024
SUB = 512
SPS = 4
OUT_CHUNKS = 4
SCAN_BLOCK = 8
PAIRS = 8
VMEM_LIMIT = 56 * 1024 * 1024

_OFF_Q = 0
_OFF_K = _OFF_Q + ATTN_WIDTH
_OFF_V = _OFF_K + KV_WIDTH
_OFF_AG = _OFF_V + KV_WIDTH
_OFF_XQ = _OFF_AG + ATTN_WIDTH
_OFF_XG = _OFF_XQ + X_WIDTH
AC_WIDTH = _OFF_XG + X_WIDTH


def _sigmoid(v):
    return 1.0 / (1.0 + jnp.exp(-v))


def _silu(v):
    return v * _sigmoid(v)


def _rms_rows(v, gain):
    return v * lax.rsqrt(jnp.mean(v * v, axis=-1, keepdims=True) + EPS) * gain


def _dot_nt(a, b):
    return lax.dot_general(a, b, (((1,), (1,)), ((), ())), preferred_element_type=F32)


def _rope_kernel(pos_ref, inv_ref, cos_ref, sin_ref):
    ang = inv_ref[...] * pos_ref[0].astype(F32)
    c, s = jnp.cos(ang), jnp.sin(ang)
    cos_ref[0] = jnp.concatenate([c, c], axis=0)
    sin_ref[0] = jnp.concatenate([-s, s], axis=0)


def _rope_tables(positions):
    b, s = positions.shape
    half = HEAD_DIM // 2
    inv = (ROPE_THETA ** (-jnp.arange(half, dtype=F32) / half)).reshape(half, 1)
    ts = min(s, 2048)
    spec = pl.BlockSpec((1, HEAD_DIM, ts), lambda i, j: (i, 0, j))
    return pl.pallas_call(
        _rope_kernel,
        out_shape=(jax.ShapeDtypeStruct((b, HEAD_DIM, s), F32),) * 2,
        grid=(b, s // ts),
        in_specs=[pl.BlockSpec((1, 1, ts), lambda i, j: (i, 0, j)),
                  pl.BlockSpec((HEAD_DIM // 2, 1), lambda i, j: (0, 0))],
        out_specs=(spec, spec),
        name="rope_tables",
    )(positions.reshape(b, 1, s), inv)


def _memkv_kernel(mem_ref, g_ref, w_ref, gk_ref, mk_ref, mv_ref):
    h = _rms_rows(mem_ref[0], g_ref[0]).astype(BF16)
    kv = jnp.dot(h, w_ref[0], preferred_element_type=F32)
    for hd in range(X_HEADS):
        sl = slice(hd * X_HEAD_DIM, (hd + 1) * X_HEAD_DIM)
        mk_ref[0, 0, :, sl] = _rms_rows(kv[:, sl], gk_ref[0]).astype(BF16)
    mv_ref[0, 0] = kv[:, X_WIDTH:].T.astype(BF16)


def _mem_kv(mem, mem_norm_g, w_mem_kv, xk_norm_g):
    b, n_mem, d = mem.shape
    depth = w_mem_kv.shape[0]
    out = jax.ShapeDtypeStruct((depth, b, n_mem, X_WIDTH), BF16)
    ospec = pl.BlockSpec((1, 1, n_mem, X_WIDTH), lambda l, i: (l, i, 0, 0))
    out_t = jax.ShapeDtypeStruct((depth, b, X_WIDTH, n_mem), BF16)
    ospec_t = pl.BlockSpec((1, 1, X_WIDTH, n_mem), lambda l, i: (l, i, 0, 0))
    return pl.pallas_call(
        _memkv_kernel,
        out_shape=(out, out_t),
        grid=(depth, b),
        in_specs=[pl.BlockSpec((1, n_mem, d), lambda l, i: (i, 0, 0)),
                  pl.BlockSpec((1, 1, d), lambda l, i: (l, 0, 0)),
                  pl.BlockSpec((1, d, 2 * X_WIDTH), lambda l, i: (l, 0, 0)),
                  pl.BlockSpec((1, 1, X_HEAD_DIM), lambda l, i: (l, 0, 0))],
        out_specs=(ospec, ospec_t),
        name="mem_kv",
    )(mem, mem_norm_g.reshape(depth, 1, d), w_mem_kv.astype(BF16), xk_norm_g.reshape(depth, 1, X_HEAD_DIM))


def _s5_prep_kernel(lr_ref, li_ref, ldt_ref, brt_ref, bit_ref, crt_ref, cit_ref,
                    wtws_ref, wcp_ref, pwr_ref, pwi_ref, *, n_scan):
    n_groups = lr_ref.shape[1]
    tab_row = lax.broadcasted_iota(jnp.int32, (3 * CHUNK, LANES), 0)
    n_tab = jnp.where(tab_row < 2 * CHUNK, tab_row, 3 * CHUNK - 1 - tab_row).astype(F32)
    scan_row = lax.broadcasted_iota(jnp.int32, (2 * SCAN_BLOCK, LANES), 0)
    n_scan_rows = jnp.where(scan_row < SCAN_BLOCK,
                            jnp.left_shift(jnp.int32(CHUNK), jnp.minimum(scan_row, n_scan - 1)),
                            CHUNK * (scan_row - SCAN_BLOCK + 1)).astype(F32)
    low = lax.broadcasted_iota(jnp.int32, (CK, LANES), 1) < SSM_STATE
    lane_s = lax.broadcasted_iota(jnp.int32, (CK, CK), 1) // SSM_CH
    hi = lax.Precision.HIGHEST

    def per_chunk_row(tab, first):
        rows = tab[first:first + CHUNK]
        return jnp.broadcast_to(rows[:, None, :], (CHUNK, SSM_CH, LANES)).reshape(CK, LANES)

    low_rows = lax.broadcasted_iota(jnp.int32, pwr_ref.shape[2:], 1) < SSM_STATE

    def group(g, gp, e):
        own = low if e == 0 else jnp.logical_not(low)
        dt = jnp.exp(ldt_ref[0, g])
        lr, li = lr_ref[0, g], li_ref[0, g]

        def power(n):
            mag = jnp.exp(lr * dt * n)
            return mag * jnp.cos(li * dt * n), mag * jnp.sin(li * dt * n)

        tab_r, tab_i = power(n_tab)
        ar, ai = tab_r[1:2], tab_i[1:2]
        den = lr * lr + li * li
        fr = ((ar - 1.0) * lr + ai * li) / den
        fi = (ai * lr - (ar - 1.0) * li) / den
        every_chunk_row = lambda a: jnp.concatenate([a] * CHUNK, axis=0)
        brt, bit = every_chunk_row(brt_ref[0, g]), every_chunk_row(bit_ref[0, g])
        bbr = fr * brt - fi * bit
        bbi = fr * bit + fi * brt

        crt, cit = every_chunk_row(crt_ref[0, g]), every_chunk_row(cit_ref[0, g])

        def c_times_power(first):
            pr, pi = per_chunk_row(tab_r, first), per_chunk_row(tab_i, first)
            return crt * pr - cit * pi, -(crt * pi + cit * pr)

        g_re, g_mim = c_times_power(0)
        d = lax.dot_general(jnp.where(low, g_re, g_mim), jnp.where(low, bbr, bbi), (((1,), (1,)), ((), ())),
                            precision=hi, preferred_element_type=F32)
        wt = jnp.where(lane_s == 0, d, 0.0)
        for s in range(1, CHUNK):
            shifted = jnp.concatenate([jnp.zeros((s * SSM_CH, CK), F32), d[:CK - s * SSM_CH]], axis=0)
            wt = jnp.where(lane_s == s, shifted, wt)
        wtws_ref[0, g, 0:CK, :] = wt.astype(BF16)

        qr, qi = per_chunk_row(tab_r, 2 * CHUNK), per_chunk_row(tab_i, 2 * CHUNK)
        ws = jnp.where(low, qr * bbr - qi * bbi, qr * bbi + qi * bbr)
        wtws_ref[0, g, CK:, :] = ws.T.astype(BF16)

        c_re, c_mim = c_times_power(1)
        wcp_ref[0, gp, 0, e * CK:(e + 1) * CK, :] = jnp.where(own, c_re, 0.0).astype(BF16)
        wcp_ref[0, gp, 1, e * CK:(e + 1) * CK, :] = jnp.where(own, c_mim, 0.0).astype(BF16)

        return power(n_scan_rows)

    def pair(gp, carry):
        (r0, i0), (r1, i1) = group(2 * gp, gp, 0), group(2 * gp + 1, gp, 1)
        pwr_ref[0, gp] = jnp.where(low_rows, r0, r1)
        pwi_ref[0, gp] = jnp.where(low_rows, i0, i1)
        return carry

    lax.fori_loop(0, n_groups // 2, pair, 0)


def _s5_prep(lam_re, lam_im, log_dt, b_re, b_im, c_re, c_im, n_scan):
    depth, g, p = lam_re.shape
    twice = lambda a: jnp.concatenate([a, a], axis=-1)
    row = lambda a: twice(a).reshape(depth, g, 1, 2 * p)
    b_t = lambda a: twice(jnp.swapaxes(a, 2, 3))
    c_t = twice
    assert n_scan <= SCAN_BLOCK
    n_rows = 2 * SCAN_BLOCK
    gb = 8
    spec = lambda *shape: pl.BlockSpec((1, gb) + shape, lambda l, i: (l, i, 0, 0))
    pair_spec = lambda *shape: pl.BlockSpec((1, gb // 2) + shape, lambda l, i: (l, i) + (0,) * len(shape))
    return pl.pallas_call(
        functools.partial(_s5_prep_kernel, n_scan=n_scan),
        out_shape=(jax.ShapeDtypeStruct((depth, g, CK + 2 * p, CK), BF16),
                   jax.ShapeDtypeStruct((depth, g // 2, 2, 2 * CK, 2 * p), BF16),
                   jax.ShapeDtypeStruct((depth, g // 2, n_rows, 2 * p), F32),
                   jax.ShapeDtypeStruct((depth, g // 2, n_rows, 2 * p), F32)),
        grid=(depth, g // gb),
        in_specs=[spec(1, 2 * p), spec(1, 2 * p), spec(1, 1),
                  spec(SSM_CH, 2 * p), spec(SSM_CH, 2 * p), spec(SSM_CH, 2 * p), spec(SSM_CH, 2 * p)],
        out_specs=(spec(CK + 2 * p, CK), pair_spec(2, 2 * CK, 2 * p), pair_spec(n_rows, 2 * p),
                   pair_spec(n_rows, 2 * p)),
        compiler_params=pltpu.CompilerParams(vmem_limit_bytes=VMEM_LIMIT),
        name="s5_prep",
    )(row(lam_re), row(lam_im), log_dt.reshape(depth, g, 1, 1),
      b_t(b_re), b_t(b_im), c_t(c_re), c_t(c_im))


def _col_rms(v):
    return lax.rsqrt(jnp.mean(v * v, axis=0, keepdims=True) + EPS)


def _ac_kernel(sinks_ref, x_ref, ob_ref, cos_ref, sin_ref, ng_ref, win_ref, gq_ref, gk_ref, mk_ref, mvt_ref,
               gxq_ref, wout_ref, woutb_ref, out_ref, k_ref, v_ref, *sub_refs, layer):
    i = pl.program_id(1)
    d_model = x_ref.shape[-1]
    half = HEAD_DIM // 2
    n_mem = mk_ref.shape[2]

    @pl.when(i == 0)
    def _():
        k_ref[0:BLOCK] = jnp.zeros((BLOCK, LANES), BF16)
        v_ref[:, 0:BLOCK] = jnp.zeros((KV_WIDTH, BLOCK), BF16)

    key_row = lax.broadcasted_iota(jnp.int32, (2 * BLOCK, BLOCK), 0)
    q_col = lax.broadcasted_iota(jnp.int32, (2 * BLOCK, BLOCK), 1)
    local_tok = lambda rho: (rho % ROWS) * CHUNK + rho // ROWS
    qi = local_tok(q_col)
    kj = (key_row // BLOCK) * BLOCK + local_tok(key_row % BLOCK)
    band = (kj >= qi + 1) & (kj <= qi + BLOCK)
    band_first = band & (kj >= jnp.where(i == 0, BLOCK, 0))
    ones_rows = jnp.ones((2 * ROWS, 2 * BLOCK), BF16)
    ones_mem = jnp.ones((2 * ROWS, n_mem), BF16)
    zeros_q = jnp.zeros((HEAD_DIM, GQA_GROUP * BLOCK), BF16)

    def tokens(blocks):
        return jnp.concatenate([x_ref[0, :, r * ROWS:(r + 1) * ROWS, :].reshape(BLOCK, d_model) for r in blocks],
                               axis=0)

    n_sub = TQ // SUB
    z_refs, q_refs, mix_refs, h_refs, ob_refs = (sub_refs[k * n_sub:(k + 1) * n_sub] for k in range(5))

    def part(refs, rows, lanes):
        sub = lanes.start // SUB
        return refs[sub].at[rows, lanes.start - sub * SUB:lanes.stop - sub * SUB]

    lanes_of = lambda sub: slice(sub * SUB, (sub + 1) * SUB)
    blocks_of = lambda sub: range(sub * SUB // BLOCK, (sub + 1) * SUB // BLOCK)
    proj_rows = (slice(0, _OFF_AG), slice(_OFF_AG, _OFF_XQ), slice(_OFF_XQ, _OFF_XG), slice(_OFF_XG, AC_WIDTH))
    out_cols = tuple(slice(c * d_model // OUT_CHUNKS, (c + 1) * d_model // OUT_CHUNKS) for c in range(OUT_CHUNKS))

    def hidden(sub):
        h_refs[sub][...] = _rms_rows(tokens(blocks_of(sub)), ng_ref[0]).astype(BF16)
        ob_refs[sub][...] = jnp.concatenate(
            [ob_ref[0, :, r * ROWS:(r + 1) * ROWS, :].reshape(BLOCK, SSM_WIDTH) for r in blocks_of(sub)],
            axis=0).astype(BF16)

    def project(sub, c):
        z_refs[sub][proj_rows[c], :] = _dot_nt(win_ref[0, proj_rows[c], :], h_refs[sub][...])

    def qkv(sub):
        lanes = lanes_of(sub)
        cos_t, sin_t = cos_ref[0, :, lanes], sin_ref[0, :, lanes]

        def head_norm_rope(v, gain):
            vn = v * _col_rms(v) * gain
            return vn * cos_t + jnp.concatenate([vn[half:], vn[:half]], axis=0) * sin_t

        kt = jnp.concatenate(
            [head_norm_rope(z_refs[sub][_OFF_K + hk * HEAD_DIM:_OFF_K + (hk + 1) * HEAD_DIM, :], gk_ref[0])
             for hk in range(N_KV_HEADS)], axis=0)
        k_ref[BLOCK + sub * SUB:BLOCK + (sub + 1) * SUB] = kt.T.astype(BF16)
        v_ref[:, BLOCK + sub * SUB:BLOCK + (sub + 1) * SUB] = z_refs[sub][_OFF_V:_OFF_V + KV_WIDTH, :].astype(BF16)
        for hq in range(N_Q_HEADS):
            rows = slice(hq * HEAD_DIM, (hq + 1) * HEAD_DIM)
            q_refs[sub][rows, :] = head_norm_rope(z_refs[sub][rows, :], gq_ref[0]).astype(BF16)

    def attend(r, hk):
        cols = slice(r * BLOCK, (r + 1) * BLOCK)
        keys = k_ref[r * BLOCK:(r + 2) * BLOCK]
        valid = band_first if r == 0 else band
        heads = range(hk * GQA_GROUP, (hk + 1) * GQA_GROUP)
        qt = jnp.concatenate([part(q_refs, slice(hq * HEAD_DIM, (hq + 1) * HEAD_DIM), cols)[...] for hq in heads],
                             axis=1)
        qt = jnp.concatenate([qt, zeros_q] if hk == 0 else [zeros_q, qt], axis=0)
        sc = jnp.dot(keys, qt, preferred_element_type=F32)
        pt, esink = [], []
        for g, hq in enumerate(heads):
            s_h = jnp.where(valid, sc[:, g * BLOCK:(g + 1) * BLOCK], NEG_INF)
            sink = sinks_ref[layer, hq] * LOG2E
            m = jnp.maximum(jnp.max(s_h, axis=0, keepdims=True), sink)
            pt.append(jnp.exp2(s_h - m).astype(BF16))
            esink.append(jnp.exp2(sink - m))
        vals = jnp.concatenate(
            [v_ref[hk * HEAD_DIM:(hk + 1) * HEAD_DIM, r * BLOCK:(r + 2) * BLOCK], ones_rows], axis=0)
        o = jnp.dot(vals, jnp.concatenate(pt, axis=1), preferred_element_type=F32)
        inv_den = 1.0 / (o[HEAD_DIM:HEAD_DIM + 1] + jnp.concatenate(esink, axis=1))
        o = o[:HEAD_DIM] * inv_den
        for g, hq in enumerate(heads):
            rows = slice(hq * HEAD_DIM, (hq + 1) * HEAD_DIM)
            gate = part(z_refs, slice(_OFF_AG + hq * HEAD_DIM, _OFF_AG + (hq + 1) * HEAD_DIM), cols)[...]
            part(mix_refs, rows, cols)[...] = (o[:, g * BLOCK:(g + 1) * BLOCK] * _silu(gate)).astype(BF16)

    def cross(sub, hd):
        lanes = lanes_of(sub)
        rows = slice(hd * X_HEAD_DIM, (hd + 1) * X_HEAD_DIM)
        xq = z_refs[sub][_OFF_XQ + hd * X_HEAD_DIM:_OFF_XQ + (hd + 1) * X_HEAD_DIM, :]
        xq = (xq * _col_rms(xq) * gxq_ref[0]).astype(BF16)
        sc = jnp.dot(mk_ref[0, 0, :, rows], xq, preferred_element_type=F32)
        p = jnp.exp2(sc - jnp.max(sc, axis=0, keepdims=True)).astype(BF16)
        vals = jnp.concatenate([mvt_ref[0, 0, rows, :], ones_mem], axis=0)
        o = jnp.dot(vals, p, preferred_element_type=F32)
        o = o[:X_HEAD_DIM] * (1.0 / o[X_HEAD_DIM:X_HEAD_DIM + 1])
        gate = z_refs[sub][_OFF_XG + hd * X_HEAD_DIM:_OFF_XG + (hd + 1) * X_HEAD_DIM, :]
        mix_refs[sub][ATTN_WIDTH + hd * X_HEAD_DIM:ATTN_WIDTH + (hd + 1) * X_HEAD_DIM, :] = (
            o * _silu(gate)).astype(BF16)

    def output(sub, c):
        cols = out_cols[c]
        width = d_model // OUT_CHUNKS
        y = lax.dot_general(mix_refs[sub][...], wout_ref[0, :, cols], (((0,), (0,)), ((), ())),
                            preferred_element_type=F32)
        y = y + jnp.dot(ob_refs[sub][...], woutb_ref[0, :, cols], preferred_element_type=F32)
        for n, r in enumerate(blocks_of(sub)):
            xr = x_ref[0, :, r * ROWS:(r + 1) * ROWS, cols]
            out_ref[0, :, r * ROWS:(r + 1) * ROWS, cols] = xr + y[n * BLOCK:(n + 1) * BLOCK].reshape(
                CHUNK, ROWS, width)

    hidden(0)
    project(0, 0)
    qkv(0)
    for c in range(1, len(proj_rows)):
        project(0, c)
    for sub in range(n_sub):
        vec = [functools.partial(attend, r, hk) for r in blocks_of(sub) for hk in range(N_KV_HEADS)]
        vec += [functools.partial(cross, sub, hd) for hd in range(X_HEADS)]
        mxu = []
        if sub + 1 < n_sub:
            hidden(sub + 1)
            mxu += [functools.partial(project, sub + 1, c) for c in range(len(proj_rows))]
        if sub >= 1:
            mxu += [functools.partial(output, sub - 1, c) for c in range(OUT_CHUNKS)]
        every = -(-len(vec) // max(len(mxu), 1))
        for n, task in enumerate(vec):
            if n % every == 0 and mxu:
                mxu.pop(0)()
            task()
        for task in mxu:
            task()
        if sub + 1 < n_sub:
            qkv(sub + 1)
    for c in range(OUT_CHUNKS):
        output(n_sub - 1, c)

    k_ref[0:BLOCK] = k_ref[TQ:TQ + BLOCK]
    v_ref[:, 0:BLOCK] = v_ref[:, TQ:TQ + BLOCK]


def _ac_layer(layer, x, ob, cos_t, sin_t, sinks, norm_g, w_in_act, gq, gk, mk, mvt, gxq, w_out_ac, w_out_b):
    b, _, n_chunk, d = x.shape
    n_mem = mk.shape[2]
    n_sub = TQ // SUB
    tab = pl.BlockSpec((1, HEAD_DIM, TQ), lambda bi, i: (bi, 0, i))
    xspec = pl.BlockSpec((1, CHUNK, TQ // CHUNK, d), lambda bi, i: (bi, 0, i, 0))
    obspec = pl.BlockSpec((1, CHUNK, TQ // CHUNK, SSM_WIDTH), lambda bi, i: (bi, 0, i, 0))
    per_layer = lambda *shape: pl.BlockSpec((1,) + shape, lambda bi, i: (layer,) + (0,) * len(shape),
                                            pipeline_mode=pl.Buffered(1))
    return pl.pallas_call(
        functools.partial(_ac_kernel, layer=layer),
        out_shape=jax.ShapeDtypeStruct(x.shape, F32),
        grid=(b, n_chunk * CHUNK // TQ),
        in_specs=[pl.BlockSpec(memory_space=pltpu.SMEM),
                  xspec, obspec, tab, tab,
                  per_layer(1, d), per_layer(AC_WIDTH, d), per_layer(HEAD_DIM, SUB), per_layer(HEAD_DIM, SUB),
                  pl.BlockSpec((1, 1, n_mem, X_WIDTH), lambda bi, i: (layer, bi, 0, 0)),
                  pl.BlockSpec((1, 1, X_WIDTH, n_mem), lambda bi, i: (layer, bi, 0, 0)),
                  per_layer(X_HEAD_DIM, SUB), per_layer(ATTN_WIDTH + X_WIDTH, d), per_layer(SSM_WIDTH, d)],
        out_specs=xspec,
        scratch_shapes=([pltpu.VMEM((BLOCK + TQ, LANES), BF16),
                         pltpu.VMEM((KV_WIDTH, BLOCK + TQ), BF16)]
                        + [pltpu.VMEM((AC_WIDTH, SUB), F32)] * n_sub
                        + [pltpu.VMEM((ATTN_WIDTH, SUB), BF16)] * n_sub
                        + [pltpu.VMEM((ATTN_WIDTH + X_WIDTH, SUB), BF16)] * n_sub
                        + [pltpu.VMEM((SUB, d), BF16)] * n_sub
                        + [pltpu.VMEM((SUB, SSM_WIDTH), BF16)] * n_sub),
        compiler_params=pltpu.CompilerParams(dimension_semantics=("arbitrary", "arbitrary"),
                                             vmem_limit_bytes=VMEM_LIMIT),
        name=f"attn_layer{layer}",
    )(sinks, x, ob, cos_t, sin_t, norm_g, w_in_act, gq, gk, mk, mvt, gxq, w_out_ac, w_out_b)


def _ssm_kernel(x_ref, ng_ref, winb_ref, wtws_ref, wcp_ref, pwr_ref, pwi_ref, dsk_ref, wglu_ref, bglu_ref,
                out_ref, z_ref, y_ref, sg_ref, hs_ref, es_ref, *, n_chunk, n_scan):
    step = pl.program_id(1)
    n_proj = CHUNK // SPS
    d_model = x_ref.shape[-1]

    @pl.when(step < n_proj)
    def _project():
        h = _rms_rows(x_ref[0].reshape(SPS * n_chunk, d_model), ng_ref[0]).astype(BF16)
        ut = _dot_nt(winb_ref[0], h)
        for e in range(SPS):
            s = SPS * step + e
            u = ut[:SSM_WIDTH, e * n_chunk:(e + 1) * n_chunk]
            z_ref[s] = u.astype(BF16).reshape(SSM_GROUPS, SSM_CH, n_chunk)
            y_ref[s] = (u * dsk_ref[0]).reshape(SSM_GROUPS, SSM_CH, n_chunk)
            sg_ref[s] = _silu(ut[SSM_WIDTH:, e * n_chunk:(e + 1) * n_chunk])

    @pl.when(step == n_proj - 1)
    def _chunks():
        n_blk = n_chunk // SCAN_BLOCK
        lvl1 = SCAN_BLOCK.bit_length() - 1
        hs_ref[:, :, 0:SCAN_BLOCK, :] = jnp.zeros((PAIRS, 2, SCAN_BLOCK, LANES), F32)
        es_ref[:, :, 0:n_blk, :] = jnp.zeros((PAIRS, 2, n_blk, LANES), F32)
        rows = pl.ds(SCAN_BLOCK, n_chunk)
        in_block = lax.broadcasted_iota(jnp.int32, (n_chunk, LANES), 0) % SCAN_BLOCK

        def mul_add(hr, hi, ar, ai, sr, si):
            return hr + ar * sr - ai * si, hi + ar * si + ai * sr

        def pairs(it, carry):
            gps = [PAIRS * it + k for k in range(PAIRS)]
            zs = [[z_ref[:, 2 * gp + e].reshape(CK, n_chunk) for e in range(2)] for gp in gps]
            power = lambda gp, j: (pwr_ref[0, gp, j:j + 1, :], pwi_ref[0, gp, j:j + 1, :])
            hr, hi = [], []
            for k, gp in enumerate(gps):
                s = [jnp.dot(wtws_ref[0, 2 * gp + e, CK:, :], zs[k][e], preferred_element_type=F32)
                     for e in range(2)]
                hr.append(jnp.concatenate([s[0][:SSM_STATE], s[1][:SSM_STATE]], axis=0).T)
                hi.append(jnp.concatenate([s[0][SSM_STATE:], s[1][SSM_STATE:]], axis=0).T)

            def put(k):
                hs_ref[k, 0, rows, :] = hr[k]
                hs_ref[k, 1, rows, :] = hi[k]

            for j in range(lvl1):
                for k in range(PAIRS):
                    put(k)
                for k, gp in enumerate(gps):
                    keep = in_block >= (1 << j)
                    sr = jnp.where(keep, hs_ref[k, 0, pl.ds(SCAN_BLOCK - (1 << j), n_chunk), :], 0.0)
                    si = jnp.where(keep, hs_ref[k, 1, pl.ds(SCAN_BLOCK - (1 << j), n_chunk), :], 0.0)
                    hr[k], hi[k] = mul_add(hr[k], hi[k], *power(gp, j), sr, si)
            for k in range(PAIRS):
                put(k)
            last = pl.ds(2 * SCAN_BLOCK - 1, n_blk, stride=SCAN_BLOCK)
            er = [hs_ref[k, 0, last, :] for k in range(PAIRS)]
            ei = [hs_ref[k, 1, last, :] for k in range(PAIRS)]
            blk = pl.ds(n_blk, n_blk)
            for j in range(lvl1, n_scan + 1):
                shift = (1 << (j - lvl1)) if j < n_scan else 1
                for k in range(PAIRS):
                    es_ref[k, 0, blk, :] = er[k]
                    es_ref[k, 1, blk, :] = ei[k]
                for k, gp in enumerate(gps):
                    sr, si = es_ref[k, 0, pl.ds(n_blk - shift, n_blk), :], es_ref[k, 1, pl.ds(n_blk - shift, n_blk), :]
                    if j < n_scan:
                        er[k], ei[k] = mul_add(er[k], ei[k], *power(gp, j), sr, si)
                    else:
                        spread = lambda v: jnp.broadcast_to(v[:, None, :], (n_blk, SCAN_BLOCK, LANES)).reshape(
                            n_chunk, LANES)
                        pr = jnp.concatenate([pwr_ref[0, gp, SCAN_BLOCK:2 * SCAN_BLOCK, :]] * n_blk, axis=0)
                        pi = jnp.concatenate([pwi_ref[0, gp, SCAN_BLOCK:2 * SCAN_BLOCK, :]] * n_blk, axis=0)
                        hr[k], hi[k] = mul_add(hr[k], hi[k], pr, pi, spread(sr), spread(si))
            for k in range(PAIRS):
                put(k)
            for k, gp in enumerate(gps):
                sr = hs_ref[k, 0, pl.ds(SCAN_BLOCK - 1, n_chunk), :]
                si = hs_ref[k, 1, pl.ds(SCAN_BLOCK - 1, n_chunk), :]
                yc = (_dot_nt(wcp_ref[0, gp, 0], sr.astype(BF16))
                      + _dot_nt(wcp_ref[0, gp, 1], si.astype(BF16)))
                for e in range(2):
                    y = jnp.dot(wtws_ref[0, 2 * gp + e, 0:CK, :], zs[k][e], preferred_element_type=F32)
                    y = y + yc[e * CK:(e + 1) * CK]
                    y_ref[:, 2 * gp + e] = y_ref[:, 2 * gp + e] + y.reshape(CHUNK, SSM_CH, n_chunk)
            return carry

        lax.fori_loop(0, SSM_GROUPS // (2 * PAIRS), pairs, 0)

    @pl.when(step >= n_proj)
    def _finish():
        t0 = SPS * (step - n_proj)
        y = jnp.concatenate([y_ref[t0 + e].reshape(SSM_WIDTH, n_chunk) for e in range(SPS)], axis=1)
        sg = jnp.concatenate([sg_ref[t0 + e] for e in range(SPS)], axis=1)
        y = jax.nn.gelu(y)
        gate = _sigmoid(jnp.dot(wglu_ref[0], y.astype(BF16), preferred_element_type=F32) + bglu_ref[0])
        out_ref[0] = (y * gate * sg).T.reshape(SPS, n_chunk, SSM_WIDTH)


def _ssm_layer(layer, x, norm_g, w_in_bt, wtws, wcp, pwr, pwi, d_skip, w_glu_t, b_glu, n_scan):
    b, _, n_chunk, d = x.shape
    n_proj = CHUNK // SPS
    per_layer = lambda *shape: pl.BlockSpec((1,) + shape, lambda bi, st: (layer,) + (0,) * len(shape),
                                            pipeline_mode=pl.Buffered(1))
    x_spec = pl.BlockSpec((1, SPS, n_chunk, d), lambda bi, st: (bi, jnp.minimum(st, n_proj - 1), 0, 0))
    o_spec = pl.BlockSpec((1, SPS, n_chunk, SSM_WIDTH), lambda bi, st: (bi, jnp.maximum(st - n_proj, 0), 0, 0))
    return pl.pallas_call(
        functools.partial(_ssm_kernel, n_chunk=n_chunk, n_scan=n_scan),
        out_shape=jax.ShapeDtypeStruct(x.shape[:3] + (SSM_WIDTH,), F32),
        grid=(b, 2 * n_proj),
        in_specs=[x_spec,
                  per_layer(1, d), per_layer(2 * SSM_WIDTH, d),
                  per_layer(SSM_GROUPS, CK + 2 * SSM_STATE, CK),
                  per_layer(SSM_GROUPS // 2, 2, 2 * CK, LANES),
                  per_layer(*pwr.shape[1:]), per_layer(*pwi.shape[1:]),
                  per_layer(SSM_WIDTH, 1), per_layer(SSM_WIDTH, SSM_WIDTH), per_layer(SSM_WIDTH, 1)],
        out_specs=o_spec,
        scratch_shapes=[pltpu.VMEM((CHUNK, SSM_GROUPS, SSM_CH, n_chunk), BF16),
                        pltpu.VMEM((CHUNK, SSM_GROUPS, SSM_CH, n_chunk), F32),
                        pltpu.VMEM((CHUNK, SSM_WIDTH, n_chunk), F32),
                        pltpu.VMEM((PAIRS, 2, SCAN_BLOCK + n_chunk, LANES), F32),
                        pltpu.VMEM((PAIRS, 2, 2 * n_chunk // SCAN_BLOCK, LANES), F32)],
        compiler_params=pltpu.CompilerParams(dimension_semantics=("arbitrary", "arbitrary"),
                                             vmem_limit_bytes=VMEM_LIMIT),
        name=f"ssm_layer{layer}",
    )(x, norm_g, w_in_bt, wtws, wcp, pwr, pwi, d_skip, w_glu_t, b_glu)


def kernel(x, mem, positions, norm_g, w_in, q_norm_g, k_norm_g, sinks, lam_re, lam_im, log_dt, b_re, b_im,
           c_re, c_im, d_skip, w_glu, b_glu, mem_norm_g, w_mem_kv, xq_norm_g, xk_norm_g, w_out):
    b, s, d = x.shape
    depth = w_in.shape[0]
    assert s % TQ == 0 and (s // CHUNK) % LANES == 0
    n_scan = (s // CHUNK - 1).bit_length()

    o_su = _OFF_AG + ATTN_WIDTH
    o_xq = o_su + 2 * SSM_WIDTH
    w_in_act = jnp.swapaxes(jnp.concatenate([w_in[:, :, :o_su], w_in[:, :, o_xq:]], axis=-1), 1, 2).astype(BF16)
    w_in_bt = jnp.swapaxes(w_in[:, :, o_su:o_xq], 1, 2).astype(BF16)
    w_out_ac = jnp.concatenate([w_out[:, :ATTN_WIDTH], w_out[:, ATTN_WIDTH + SSM_WIDTH:]], axis=1).astype(BF16)
    w_out_b = w_out[:, ATTN_WIDTH:ATTN_WIDTH + SSM_WIDTH].astype(BF16)
    w_glu_t = jnp.swapaxes(w_glu, 1, 2).astype(BF16)
    norm_g3 = norm_g.reshape(depth, 1, d)
    over_tokens = lambda g: jnp.broadcast_to(g[:, :, None], g.shape + (SUB,))
    gq = over_tokens(q_norm_g * (LOG2E / math.sqrt(HEAD_DIM)))
    gk = over_tokens(k_norm_g)
    gxq = over_tokens(xq_norm_g * (LOG2E / math.sqrt(X_HEAD_DIM)))
    d_skip3 = d_skip.reshape(depth, SSM_WIDTH, 1)
    b_glu3 = b_glu.reshape(depth, SSM_WIDTH, 1)

    n_chunk = s // CHUNK
    pos_blocks = positions.reshape(b, s // BLOCK, ROWS, CHUNK).swapaxes(2, 3).reshape(b, s)
    cos_t, sin_t = _rope_tables(pos_blocks)
    mk, mvt = _mem_kv(mem, mem_norm_g, w_mem_kv, xk_norm_g)
    wtws, wcp, pwr, pwi = _s5_prep(lam_re, lam_im, log_dt, b_re, b_im, c_re, c_im, n_scan)

    xp = x.reshape(b, n_chunk, CHUNK, d).swapaxes(1, 2)
    for layer in range(depth):
        ob = _ssm_layer(layer, xp, norm_g3, w_in_bt, wtws, wcp, pwr, pwi, d_skip3, w_glu_t, b_glu3, n_scan)
        xp = _ac_layer(layer, xp, ob, cos_t, sin_t, sinks, norm_g3, w_in_act, gq, gk, mk, mvt, gxq, w_out_ac,
                       w_out_b)
    return xp.swapaxes(1, 2).reshape(b, s, d)
```

```python
import functools
import math

import jax
import jax.numpy as jnp
from jax import lax
from jax.experimental import pallas as pl
from jax.experimental.pallas import tpu as pltpu

F32 = jnp.float32
BF16 = jnp.bfloat16

EPS = 1e-6
ROPE_THETA = 10000.0
NEG_INF = -1e30
LOG2E = math.log2(math.e)

HEAD_DIM = 64
N_Q_HEADS = 8
N_KV_HEADS = 2
GQA_GROUP = N_Q_HEADS // N_KV_HEADS
BLOCK = 128
ATTN_WIDTH = N_Q_HEADS * HEAD_DIM
KV_WIDTH = N_KV_HEADS * HEAD_DIM
SSM_CH = 16
SSM_GROUPS = 32
SSM_STATE = 64
SSM_WIDTH = SSM_GROUPS * SSM_CH
X_HEADS = 4
X_HEAD_DIM = 128
X_WIDTH = X_HEADS * X_HEAD_DIM

LANES = 128
CHUNK = 16
CK = CHUNK * SSM_CH
ROWS = BLOCK // CHUNK
TQ = 1024
SUB = 512
SPS = 4
OUT_CHUNKS = 4
SCAN_BLOCK = 8
PAIRS = 8
VMEM_LIMIT = 56 * 1024 * 1024

_OFF_Q = 0
_OFF_K = _OFF_Q + ATTN_WIDTH
_OFF_V = _OFF_K + KV_WIDTH
_OFF_AG = _OFF_V + KV_WIDTH
_OFF_XQ = _OFF_AG + ATTN_WIDTH
_OFF_XG = _OFF_XQ + X_WIDTH
AC_WIDTH = _OFF_XG + X_WIDTH


def _sigmoid(v):
    return 1.0 / (1.0 + jnp.exp(-v))


def _silu(v):
    return v * _sigmoid(v)


def _rms_rows(v, gain):
    return v * lax.rsqrt(jnp.mean(v * v, axis=-1, keepdims=True) + EPS) * gain


def _dot_nt(a, b):
    return lax.dot_general(a, b, (((1,), (1,)), ((), ())), preferred_element_type=F32)


def _rope_kernel(pos_ref, inv_ref, cos_ref, sin_ref):
    ang = inv_ref[...] * pos_ref[0].astype(F32)
    c, s = jnp.cos(ang), jnp.sin(ang)
    cos_ref[0] = jnp.concatenate([c, c], axis=0)
    sin_ref[0] = jnp.concatenate([-s, s], axis=0)


def _rope_tables(positions):
    b, s = positions.shape
    half = HEAD_DIM // 2
    inv = (ROPE_THETA ** (-jnp.arange(half, dtype=F32) / half)).reshape(half, 1)
    ts = min(s, 2048)
    spec = pl.BlockSpec((1, HEAD_DIM, ts), lambda i, j: (i, 0, j))
    return pl.pallas_call(
        _rope_kernel,
        out_shape=(jax.ShapeDtypeStruct((b, HEAD_DIM, s), F32),) * 2,
        grid=(b, s // ts),
        in_specs=[pl.BlockSpec((1, 1, ts), lambda i, j: (i, 0, j)),
                  pl.BlockSpec((HEAD_DIM // 2, 1), lambda i, j: (0, 0))],
        out_specs=(spec, spec),
        name="rope_tables",
    )(positions.reshape(b, 1, s), inv)


def _memkv_kernel(mem_ref, g_ref, w_ref, gk_ref, mk_ref, mv_ref):
    nb, n_mem, d = mem_ref.shape
    h = _rms_rows(mem_ref[...].reshape(nb * n_mem, d), g_ref[0]).astype(BF16)
    kv = jnp.dot(h, w_ref[0], preferred_element_type=F32)
    for hd in range(X_HEADS):
        sl = slice(hd * X_HEAD_DIM, (hd + 1) * X_HEAD_DIM)
        mk_ref[0, :, :, sl] = _rms_rows(kv[:, sl], gk_ref[0]).astype(BF16).reshape(nb, n_mem, X_HEAD_DIM)
    for bi in range(nb):
        mv_ref[0, bi] = kv[bi * n_mem:(bi + 1) * n_mem, X_WIDTH:].T.astype(BF16)


def _mem_kv(mem, mem_norm_g, w_mem_kv, xk_norm_g):
    b, n_mem, d = mem.shape
    depth = w_mem_kv.shape[0]
    out = jax.ShapeDtypeStruct((depth, b, n_mem, X_WIDTH), BF16)
    ospec = pl.BlockSpec((1, b, n_mem, X_WIDTH), lambda l: (l, 0, 0, 0))
    out_t = jax.ShapeDtypeStruct((depth, b, X_WIDTH, n_mem), BF16)
    ospec_t = pl.BlockSpec((1, b, X_WIDTH, n_mem), lambda l: (l, 0, 0, 0))
    return pl.pallas_call(
        _memkv_kernel,
        out_shape=(out, out_t),
        grid=(depth,),
        in_specs=[pl.BlockSpec((b, n_mem, d), lambda l: (0, 0, 0)),
                  pl.BlockSpec((1, 1, d), lambda l: (l, 0, 0)),
                  pl.BlockSpec((1, d, 2 * X_WIDTH), lambda l: (l, 0, 0)),
                  pl.BlockSpec((1, 1, X_HEAD_DIM), lambda l: (l, 0, 0))],
        out_specs=(ospec, ospec_t),
        compiler_params=pltpu.CompilerParams(vmem_limit_bytes=VMEM_LIMIT),
        name="mem_kv",
    )(mem, mem_norm_g.reshape(depth, 1, d), w_mem_kv.astype(BF16), xk_norm_g.reshape(depth, 1, X_HEAD_DIM))


def _s5_prep_kernel(lr_ref, li_ref, ldt_ref, brt_ref, bit_ref, crt_ref, cit_ref,
                    wtws_ref, wcp_ref, pwr_ref, pwi_ref, *, n_scan):
    n_groups = lr_ref.shape[1]
    tab_row = lax.broadcasted_iota(jnp.int32, (3 * CHUNK, LANES), 0)
    n_tab = jnp.where(tab_row < 2 * CHUNK, tab_row, 3 * CHUNK - 1 - tab_row).astype(F32)
    scan_row = lax.broadcasted_iota(jnp.int32, (2 * SCAN_BLOCK, LANES), 0)
    n_scan_rows = jnp.where(scan_row < SCAN_BLOCK,
                            jnp.left_shift(jnp.int32(CHUNK), jnp.minimum(scan_row, n_scan - 1)),
                            CHUNK * (scan_row - SCAN_BLOCK + 1)).astype(F32)
    low = lax.broadcasted_iota(jnp.int32, (CK, LANES), 1) < SSM_STATE
    lane_s = lax.broadcasted_iota(jnp.int32, (CK, CK), 1) // SSM_CH
    hi = lax.Precision.HIGHEST

    def per_chunk_row(tab, first):
        rows = tab[first:first + CHUNK]
        return jnp.broadcast_to(rows[:, None, :], (CHUNK, SSM_CH, LANES)).reshape(CK, LANES)

    low_rows = lax.broadcasted_iota(jnp.int32, pwr_ref.shape[2:], 1) < SSM_STATE

    def group(g, gp, e):
        own = low if e == 0 else jnp.logical_not(low)
        dt = jnp.exp(ldt_ref[0, g])
        lr, li = lr_ref[0, g], li_ref[0, g]

        def power(n):
            mag = jnp.exp(lr * dt * n)
            return mag * jnp.cos(li * dt * n), mag * jnp.sin(li * dt * n)

        tab_r, tab_i = power(n_tab)
        ar, ai = tab_r[1:2], tab_i[1:2]
        den = lr * lr + li * li
        fr = ((ar - 1.0) * lr + ai * li) / den
        fi = (ai * lr - (ar - 1.0) * li) / den
        every_chunk_row = lambda a: jnp.concatenate([a] * CHUNK, axis=0)
        brt, bit = every_chunk_row(brt_ref[0, g]), every_chunk_row(bit_ref[0, g])
        bbr = fr * brt - fi * bit
        bbi = fr * bit + fi * brt

        crt, cit = every_chunk_row(crt_ref[0, g]), every_chunk_row(cit_ref[0, g])

        def c_times_power(first):
            pr, pi = per_chunk_row(tab_r, first), per_chunk_row(tab_i, first)
            return crt * pr - cit * pi, -(crt * pi + cit * pr)

        g_re, g_mim = c_times_power(0)
        d = lax.dot_general(jnp.where(low, g_re, g_mim), jnp.where(low, bbr, bbi), (((1,), (1,)), ((), ())),
                            precision=hi, preferred_element_type=F32)
        wt = jnp.where(lane_s == 0, d, 0.0)
        for s in range(1, CHUNK):
            shifted = jnp.concatenate([jnp.zeros((s * SSM_CH, CK), F32), d[:CK - s * SSM_CH]], axis=0)
            wt = jnp.where(lane_s == s, shifted, wt)
        wtws_ref[0, g, 0:CK, :] = wt.astype(BF16)

        qr, qi = per_chunk_row(tab_r, 2 * CHUNK), per_chunk_row(tab_i, 2 * CHUNK)
        ws = jnp.where(low, qr * bbr - qi * bbi, qr * bbi + qi * bbr)
        wtws_ref[0, g, CK:, :] = ws.T.astype(BF16)

        c_re, c_mim = c_times_power(1)
        wcp_ref[0, gp, 0, e * CK:(e + 1) * CK, :] = jnp.where(own, c_re, 0.0).astype(BF16)
        wcp_ref[0, gp, 1, e * CK:(e + 1) * CK, :] = jnp.where(own, c_mim, 0.0).astype(BF16)

        return power(n_scan_rows)

    def pair(gp, carry):
        (r0, i0), (r1, i1) = group(2 * gp, gp, 0), group(2 * gp + 1, gp, 1)
        pwr_ref[0, gp] = jnp.where(low_rows, r0, r1)
        pwi_ref[0, gp] = jnp.where(low_rows, i0, i1)
        return carry

    lax.fori_loop(0, n_groups // 2, pair, 0)


def _s5_prep(lam_re, lam_im, log_dt, b_re, b_im, c_re, c_im, n_scan):
    depth, g, p = lam_re.shape
    twice = lambda a: jnp.concatenate([a, a], axis=-1)
    row = lambda a: twice(a).reshape(depth, g, 1, 2 * p)
    b_t = lambda a: twice(jnp.swapaxes(a, 2, 3))
    c_t = twice
    assert n_scan <= SCAN_BLOCK
    n_rows = 2 * SCAN_BLOCK
    gb = 8
    spec = lambda *shape: pl.BlockSpec((1, gb) + shape, lambda l, i: (l, i, 0, 0))
    pair_spec = lambda *shape: pl.BlockSpec((1, gb // 2) + shape, lambda l, i: (l, i) + (0,) * len(shape))
    return pl.pallas_call(
        functools.partial(_s5_prep_kernel, n_scan=n_scan),
        out_shape=(jax.ShapeDtypeStruct((depth, g, CK + 2 * p, CK), BF16),
                   jax.ShapeDtypeStruct((depth, g // 2, 2, 2 * CK, 2 * p), BF16),
                   jax.ShapeDtypeStruct((depth, g // 2, n_rows, 2 * p), F32),
                   jax.ShapeDtypeStruct((depth, g // 2, n_rows, 2 * p), F32)),
        grid=(depth, g // gb),
        in_specs=[spec(1, 2 * p), spec(1, 2 * p), spec(1, 1),
                  spec(SSM_CH, 2 * p), spec(SSM_CH, 2 * p), spec(SSM_CH, 2 * p), spec(SSM_CH, 2 * p)],
        out_specs=(spec(CK + 2 * p, CK), pair_spec(2, 2 * CK, 2 * p), pair_spec(n_rows, 2 * p),
                   pair_spec(n_rows, 2 * p)),
        compiler_params=pltpu.CompilerParams(vmem_limit_bytes=VMEM_LIMIT),
        name="s5_prep",
    )(row(lam_re), row(lam_im), log_dt.reshape(depth, g, 1, 1),
      b_t(b_re), b_t(b_im), c_t(c_re), c_t(c_im))


def _col_rms(v):
    return lax.rsqrt(jnp.mean(v * v, axis=0, keepdims=True) + EPS)


def _ac_kernel(sinks_ref, x_ref, ob_ref, cos_ref, sin_ref, ng_ref, win_ref, gq_ref, gk_ref, mk_ref, mvt_ref,
               gxq_ref, wout_ref, woutb_ref, out_ref, k_ref, v_ref, shuf_ref, *sub_refs, layer, natural_out):
    i = pl.program_id(1)
    d_model = x_ref.shape[-1]
    half = HEAD_DIM // 2
    n_mem = mk_ref.shape[2]

    @pl.when(i == 0)
    def _():
        k_ref[0:BLOCK] = jnp.zeros((BLOCK, LANES), BF16)
        v_ref[:, 0:BLOCK] = jnp.zeros((KV_WIDTH, BLOCK), BF16)

    key_row = lax.broadcasted_iota(jnp.int32, (2 * BLOCK, BLOCK), 0)
    q_col = lax.broadcasted_iota(jnp.int32, (2 * BLOCK, BLOCK), 1)
    local_tok = lambda rho: (rho % ROWS) * CHUNK + rho // ROWS
    qi = local_tok(q_col)
    kj = (key_row // BLOCK) * BLOCK + local_tok(key_row % BLOCK)
    band = (kj >= qi + 1) & (kj <= qi + BLOCK)
    band_first = band & (kj >= jnp.where(i == 0, BLOCK, 0))
    ones_rows = jnp.ones((2 * ROWS, 2 * BLOCK), BF16)
    ones_mem = jnp.ones((2 * ROWS, n_mem), BF16)
    zeros_q = jnp.zeros((HEAD_DIM, GQA_GROUP * BLOCK), BF16)

    def tokens(blocks):
        return jnp.concatenate([x_ref[0, :, r * ROWS:(r + 1) * ROWS, :].reshape(BLOCK, d_model) for r in blocks],
                               axis=0)

    n_sub = TQ // SUB
    z_refs, q_refs, mix_refs, h_refs, ob_refs = (sub_refs[k * n_sub:(k + 1) * n_sub] for k in range(5))

    def part(refs, rows, lanes):
        sub = lanes.start // SUB
        return refs[sub].at[rows, lanes.start - sub * SUB:lanes.stop - sub * SUB]

    lanes_of = lambda sub: slice(sub * SUB, (sub + 1) * SUB)
    blocks_of = lambda sub: range(sub * SUB // BLOCK, (sub + 1) * SUB // BLOCK)
    proj_rows = (slice(0, _OFF_AG), slice(_OFF_AG, _OFF_XQ), slice(_OFF_XQ, _OFF_XG), slice(_OFF_XG, AC_WIDTH))
    out_cols = tuple(slice(c * d_model // OUT_CHUNKS, (c + 1) * d_model // OUT_CHUNKS) for c in range(OUT_CHUNKS))

    def hidden(sub):
        h_refs[sub][...] = _rms_rows(tokens(blocks_of(sub)), ng_ref[0]).astype(BF16)
        ob_refs[sub][...] = jnp.concatenate(
            [ob_ref[0, :, r * ROWS:(r + 1) * ROWS, :].reshape(BLOCK, SSM_WIDTH) for r in blocks_of(sub)],
            axis=0).astype(BF16)

    def project(sub, c):
        z_refs[sub][proj_rows[c], :] = _dot_nt(win_ref[0, proj_rows[c], :], h_refs[sub][...])

    def qkv(sub):
        lanes = lanes_of(sub)
        cos_t, sin_t = cos_ref[0, :, lanes], sin_ref[0, :, lanes]

        def head_norm_rope(v, gain):
            vn = v * _col_rms(v) * gain
            return vn * cos_t + jnp.concatenate([vn[half:], vn[:half]], axis=0) * sin_t

        kt = jnp.concatenate(
            [head_norm_rope(z_refs[sub][_OFF_K + hk * HEAD_DIM:_OFF_K + (hk + 1) * HEAD_DIM, :], gk_ref[0])
             for hk in range(N_KV_HEADS)], axis=0)
        k_ref[BLOCK + sub * SUB:BLOCK + (sub + 1) * SUB] = kt.T.astype(BF16)
        v_ref[:, BLOCK + sub * SUB:BLOCK + (sub + 1) * SUB] = z_refs[sub][_OFF_V:_OFF_V + KV_WIDTH, :].astype(BF16)
        for hq in range(N_Q_HEADS):
            rows = slice(hq * HEAD_DIM, (hq + 1) * HEAD_DIM)
            q_refs[sub][rows, :] = head_norm_rope(z_refs[sub][rows, :], gq_ref[0]).astype(BF16)

    def attend(r, hk):
        cols = slice(r * BLOCK, (r + 1) * BLOCK)
        keys = k_ref[r * BLOCK:(r + 2) * BLOCK]
        valid = band_first if r == 0 else band
        heads = range(hk * GQA_GROUP, (hk + 1) * GQA_GROUP)
        qt = jnp.concatenate([part(q_refs, slice(hq * HEAD_DIM, (hq + 1) * HEAD_DIM), cols)[...] for hq in heads],
                             axis=1)
        qt = jnp.concatenate([qt, zeros_q] if hk == 0 else [zeros_q, qt], axis=0)
        sc = jnp.dot(keys, qt, preferred_element_type=F32)
        pt, esink = [], []
        for g, hq in enumerate(heads):
            s_h = jnp.where(valid, sc[:, g * BLOCK:(g + 1) * BLOCK], NEG_INF)
            sink = sinks_ref[layer, hq] * LOG2E
            m = jnp.maximum(jnp.max(s_h, axis=0, keepdims=True), sink)
            pt.append(jnp.exp2(s_h - m).astype(BF16))
            esink.append(jnp.exp2(sink - m))
        vals = jnp.concatenate(
            [v_ref[hk * HEAD_DIM:(hk + 1) * HEAD_DIM, r * BLOCK:(r + 2) * BLOCK], ones_rows], axis=0)
        o = jnp.dot(vals, jnp.concatenate(pt, axis=1), preferred_element_type=F32)
        inv_den = 1.0 / (o[HEAD_DIM:HEAD_DIM + 1] + jnp.concatenate(esink, axis=1))
        o = o[:HEAD_DIM] * inv_den
        for g, hq in enumerate(heads):
            rows = slice(hq * HEAD_DIM, (hq + 1) * HEAD_DIM)
            gate = part(z_refs, slice(_OFF_AG + hq * HEAD_DIM, _OFF_AG + (hq + 1) * HEAD_DIM), cols)[...]
            part(mix_refs, rows, cols)[...] = (o[:, g * BLOCK:(g + 1) * BLOCK] * _silu(gate)).astype(BF16)

    def cross(sub, hd):
        lanes = lanes_of(sub)
        rows = slice(hd * X_HEAD_DIM, (hd + 1) * X_HEAD_DIM)
        xq = z_refs[sub][_OFF_XQ + hd * X_HEAD_DIM:_OFF_XQ + (hd + 1) * X_HEAD_DIM, :]
        xq = (xq * _col_rms(xq) * gxq_ref[0]).astype(BF16)
        sc = jnp.dot(mk_ref[0, 0, :, rows], xq, preferred_element_type=F32)
        p = jnp.exp2(sc - jnp.max(sc, axis=0, keepdims=True)).astype(BF16)
        vals = jnp.concatenate([mvt_ref[0, 0, rows, :], ones_mem], axis=0)
        o = jnp.dot(vals, p, preferred_element_type=F32)
        o = o[:X_HEAD_DIM] * (1.0 / o[X_HEAD_DIM:X_HEAD_DIM + 1])
        gate = z_refs[sub][_OFF_XG + hd * X_HEAD_DIM:_OFF_XG + (hd + 1) * X_HEAD_DIM, :]
        mix_refs[sub][ATTN_WIDTH + hd * X_HEAD_DIM:ATTN_WIDTH + (hd + 1) * X_HEAD_DIM, :] = (
            o * _silu(gate)).astype(BF16)

    def output(sub, c):
        cols = out_cols[c]
        width = d_model // OUT_CHUNKS
        y = lax.dot_general(mix_refs[sub][...], wout_ref[0, :, cols], (((0,), (0,)), ((), ())),
                            preferred_element_type=F32)
        y = y + jnp.dot(ob_refs[sub][...], woutb_ref[0, :, cols], preferred_element_type=F32)
        for n, r in enumerate(blocks_of(sub)):
            xr = x_ref[0, :, r * ROWS:(r + 1) * ROWS, cols]
            res = xr + y[n * BLOCK:(n + 1) * BLOCK].reshape(CHUNK, ROWS, width)
            if not natural_out:
                out_ref[0, :, r * ROWS:(r + 1) * ROWS, cols] = res
                continue
            res = res.reshape(BLOCK, width)
            for slab in range(width // LANES):
                shuf_ref[c, n, slab] = res[:, slab * LANES:(slab + 1) * LANES]
                lanes = slice(cols.start + slab * LANES, cols.start + (slab + 1) * LANES)
                for ch in range(ROWS):
                    out_ref[0, r * BLOCK + ch * CHUNK:r * BLOCK + (ch + 1) * CHUNK, lanes] = (
                        shuf_ref[c, n, slab, pl.ds(ch, CHUNK, stride=ROWS), :])

    hidden(0)
    project(0, 0)
    qkv(0)
    for c in range(1, len(proj_rows)):
        project(0, c)
    for sub in range(n_sub):
        vec = [functools.partial(attend, r, hk) for r in blocks_of(sub) for hk in range(N_KV_HEADS)]
        vec += [functools.partial(cross, sub, hd) for hd in range(X_HEADS)]
        mxu = []
        if sub + 1 < n_sub:
            hidden(sub + 1)
            mxu += [functools.partial(project, sub + 1, c) for c in range(len(proj_rows))]
        if sub >= 1:
            mxu += [functools.partial(output, sub - 1, c) for c in range(OUT_CHUNKS)]
        every = -(-len(vec) // max(len(mxu), 1))
        for n, task in enumerate(vec):
            if n % every == 0 and mxu:
                mxu.pop(0)()
            task()
        for task in mxu:
            task()
        if sub + 1 < n_sub:
            qkv(sub + 1)
    for c in range(OUT_CHUNKS):
        output(n_sub - 1, c)

    k_ref[0:BLOCK] = k_ref[TQ:TQ + BLOCK]
    v_ref[:, 0:BLOCK] = v_ref[:, TQ:TQ + BLOCK]


def _ac_layer(layer, x, ob, cos_t, sin_t, sinks, norm_g, w_in_act, gq, gk, mk, mvt, gxq, w_out_ac, w_out_b,
              natural_out):
    b, _, n_chunk, d = x.shape
    n_mem = mk.shape[2]
    n_sub = TQ // SUB
    if natural_out:
        out_shape = jax.ShapeDtypeStruct((b, n_chunk * CHUNK, d), F32)
        out_spec = pl.BlockSpec((1, TQ, d), lambda bi, i: (bi, i, 0))
    else:
        out_shape = jax.ShapeDtypeStruct(x.shape, F32)
        out_spec = pl.BlockSpec((1, CHUNK, TQ // CHUNK, d), lambda bi, i: (bi, 0, i, 0))
    tab = pl.BlockSpec((1, HEAD_DIM, TQ), lambda bi, i: (bi, 0, i))
    xspec = pl.BlockSpec((1, CHUNK, TQ // CHUNK, d), lambda bi, i: (bi, 0, i, 0))
    obspec = pl.BlockSpec((1, CHUNK, TQ // CHUNK, SSM_WIDTH), lambda bi, i: (bi, 0, i, 0))
    per_layer = lambda *shape: pl.BlockSpec((1,) + shape, lambda bi, i: (layer,) + (0,) * len(shape),
                                            pipeline_mode=pl.Buffered(1))
    return pl.pallas_call(
        functools.partial(_ac_kernel, layer=layer, natural_out=natural_out),
        out_shape=out_shape,
        grid=(b, n_chunk * CHUNK // TQ),
        in_specs=[pl.BlockSpec(memory_space=pltpu.SMEM),
                  xspec, obspec, tab, tab,
                  per_layer(1, d), per_layer(AC_WIDTH, d), per_layer(HEAD_DIM, SUB), per_layer(HEAD_DIM, SUB),
                  pl.BlockSpec((1, 1, n_mem, X_WIDTH), lambda bi, i: (layer, bi, 0, 0)),
                  pl.BlockSpec((1, 1, X_WIDTH, n_mem), lambda bi, i: (layer, bi, 0, 0)),
                  per_layer(X_HEAD_DIM, SUB), per_layer(ATTN_WIDTH + X_WIDTH, d), per_layer(SSM_WIDTH, d)],
        out_specs=out_spec,
        scratch_shapes=([pltpu.VMEM((BLOCK + TQ, LANES), BF16),
                         pltpu.VMEM((KV_WIDTH, BLOCK + TQ), BF16),
                         pltpu.VMEM((OUT_CHUNKS, SUB // BLOCK, d // OUT_CHUNKS // LANES, BLOCK, LANES), F32)]
                        + [pltpu.VMEM((AC_WIDTH, SUB), F32)] * n_sub
                        + [pltpu.VMEM((ATTN_WIDTH, SUB), BF16)] * n_sub
                        + [pltpu.VMEM((ATTN_WIDTH + X_WIDTH, SUB), BF16)] * n_sub
                        + [pltpu.VMEM((SUB, d), BF16)] * n_sub
                        + [pltpu.VMEM((SUB, SSM_WIDTH), BF16)] * n_sub),
        compiler_params=pltpu.CompilerParams(dimension_semantics=("arbitrary", "arbitrary"),
                                             vmem_limit_bytes=VMEM_LIMIT),
        name=f"attn_layer{layer}",
    )(sinks, x, ob, cos_t, sin_t, norm_g, w_in_act, gq, gk, mk, mvt, gxq, w_out_ac, w_out_b)


def _ssm_kernel(x_ref, ng_ref, winb_ref, wtws_ref, wcp_ref, pwr_ref, pwi_ref, dsk_ref, wglu_ref, bglu_ref,
                out_ref, z_ref, y_ref, sg_ref, hs_ref, es_ref, *, n_chunk, n_scan):
    step = pl.program_id(1)
    n_proj = CHUNK // SPS
    d_model = x_ref.shape[-1]

    @pl.when(step < n_proj)
    def _project():
        h = _rms_rows(x_ref[0].reshape(SPS * n_chunk, d_model), ng_ref[0]).astype(BF16)
        ut = _dot_nt(winb_ref[0], h)
        for e in range(SPS):
            s = SPS * step + e
            u = ut[:SSM_WIDTH, e * n_chunk:(e + 1) * n_chunk]
            z_ref[s] = u.astype(BF16).reshape(SSM_GROUPS, SSM_CH, n_chunk)
            y_ref[s] = (u * dsk_ref[0]).reshape(SSM_GROUPS, SSM_CH, n_chunk)
            sg_ref[s] = _silu(ut[SSM_WIDTH:, e * n_chunk:(e + 1) * n_chunk])

    @pl.when(step == n_proj - 1)
    def _chunks():
        n_blk = n_chunk // SCAN_BLOCK
        lvl1 = SCAN_BLOCK.bit_length() - 1
        hs_ref[:, :, 0:SCAN_BLOCK, :] = jnp.zeros((PAIRS, 2, SCAN_BLOCK, LANES), F32)
        es_ref[:, :, 0:n_blk, :] = jnp.zeros((PAIRS, 2, n_blk, LANES), F32)
        rows = pl.ds(SCAN_BLOCK, n_chunk)
        in_block = lax.broadcasted_iota(jnp.int32, (n_chunk, LANES), 0) % SCAN_BLOCK

        def mul_add(hr, hi, ar, ai, sr, si):
            return hr + ar * sr - ai * si, hi + ar * si + ai * sr

        def pairs(it, carry):
            gps = [PAIRS * it + k for k in range(PAIRS)]
            zs = [[z_ref[:, 2 * gp + e].reshape(CK, n_chunk) for e in range(2)] for gp in gps]
            power = lambda gp, j: (pwr_ref[0, gp, j:j + 1, :], pwi_ref[0, gp, j:j + 1, :])
            hr, hi = [], []
            for k, gp in enumerate(gps):
                s = [jnp.dot(wtws_ref[0, 2 * gp + e, CK:, :], zs[k][e], preferred_element_type=F32)
                     for e in range(2)]
                hr.append(jnp.concatenate([s[0][:SSM_STATE], s[1][:SSM_STATE]], axis=0).T)
                hi.append(jnp.concatenate([s[0][SSM_STATE:], s[1][SSM_STATE:]], axis=0).T)

            def put(k):
                hs_ref[k, 0, rows, :] = hr[k]
                hs_ref[k, 1, rows, :] = hi[k]

            for j in range(lvl1):
                for k in range(PAIRS):
                    put(k)
                for k, gp in enumerate(gps):
                    keep = in_block >= (1 << j)
                    sr = jnp.where(keep, hs_ref[k, 0, pl.ds(SCAN_BLOCK - (1 << j), n_chunk), :], 0.0)
                    si = jnp.where(keep, hs_ref[k, 1, pl.ds(SCAN_BLOCK - (1 << j), n_chunk), :], 0.0)
                    hr[k], hi[k] = mul_add(hr[k], hi[k], *power(gp, j), sr, si)
            for k in range(PAIRS):
                put(k)
            last = pl.ds(2 * SCAN_BLOCK - 1, n_blk, stride=SCAN_BLOCK)
            er = [hs_ref[k, 0, last, :] for k in range(PAIRS)]
            ei = [hs_ref[k, 1, last, :] for k in range(PAIRS)]
            blk = pl.ds(n_blk, n_blk)
            for j in range(lvl1, n_scan + 1):
                shift = (1 << (j - lvl1)) if j < n_scan else 1
                for k in range(PAIRS):
                    es_ref[k, 0, blk, :] = er[k]
                    es_ref[k, 1, blk, :] = ei[k]
                for k, gp in enumerate(gps):
                    sr, si = es_ref[k, 0, pl.ds(n_blk - shift, n_blk), :], es_ref[k, 1, pl.ds(n_blk - shift, n_blk), :]
                    if j < n_scan:
                        er[k], ei[k] = mul_add(er[k], ei[k], *power(gp, j), sr, si)
                    else:
                        spread = lambda v: jnp.broadcast_to(v[:, None, :], (n_blk, SCAN_BLOCK, LANES)).reshape(
                            n_chunk, LANES)
                        pr = jnp.concatenate([pwr_ref[0, gp, SCAN_BLOCK:2 * SCAN_BLOCK, :]] * n_blk, axis=0)
                        pi = jnp.concatenate([pwi_ref[0, gp, SCAN_BLOCK:2 * SCAN_BLOCK, :]] * n_blk, axis=0)
                        hr[k], hi[k] = mul_add(hr[k], hi[k], pr, pi, spread(sr), spread(si))
            for k in range(PAIRS):
                put(k)
            for k, gp in enumerate(gps):
                sr = hs_ref[k, 0, pl.ds(SCAN_BLOCK - 1, n_chunk), :]
                si = hs_ref[k, 1, pl.ds(SCAN_BLOCK - 1, n_chunk), :]
                yc = (_dot_nt(wcp_ref[0, gp, 0], sr.astype(BF16))
                      + _dot_nt(wcp_ref[0, gp, 1], si.astype(BF16)))
                for e in range(2):
                    y = jnp.dot(wtws_ref[0, 2 * gp + e, 0:CK, :], zs[k][e], preferred_element_type=F32)
                    y = y + yc[e * CK:(e + 1) * CK]
                    y_ref[:, 2 * gp + e] = y_ref[:, 2 * gp + e] + y.reshape(CHUNK, SSM_CH, n_chunk)
            return carry

        lax.fori_loop(0, SSM_GROUPS // (2 * PAIRS), pairs, 0)

    @pl.when(step >= n_proj)
    def _finish():
        t0 = SPS * (step - n_proj)
        y = jnp.concatenate([y_ref[t0 + e].reshape(SSM_WIDTH, n_chunk) for e in range(SPS)], axis=1)
        sg = jnp.concatenate([sg_ref[t0 + e] for e in range(SPS)], axis=1)
        y = jax.nn.gelu(y)
        gate = _sigmoid(jnp.dot(wglu_ref[0], y.astype(BF16), preferred_element_type=F32) + bglu_ref[0])
        out_ref[0] = (y * gate * sg).T.reshape(SPS, n_chunk, SSM_WIDTH)


def _ssm_layer(layer, x, norm_g, w_in_bt, wtws, wcp, pwr, pwi, d_skip, w_glu_t, b_glu, n_scan):
    b, _, n_chunk, d = x.shape
    n_proj = CHUNK // SPS
    per_layer = lambda *shape: pl.BlockSpec((1,) + shape, lambda bi, st: (layer,) + (0,) * len(shape),
                                            pipeline_mode=pl.Buffered(1))
    x_spec = pl.BlockSpec((1, SPS, n_chunk, d), lambda bi, st: (bi, jnp.minimum(st, n_proj - 1), 0, 0))
    o_spec = pl.BlockSpec((1, SPS, n_chunk, SSM_WIDTH), lambda bi, st: (bi, jnp.maximum(st - n_proj, 0), 0, 0))
    return pl.pallas_call(
        functools.partial(_ssm_kernel, n_chunk=n_chunk, n_scan=n_scan),
        out_shape=jax.ShapeDtypeStruct(x.shape[:3] + (SSM_WIDTH,), F32),
        grid=(b, 2 * n_proj),
        in_specs=[x_spec,
                  per_layer(1, d), per_layer(2 * SSM_WIDTH, d),
                  per_layer(SSM_GROUPS, CK + 2 * SSM_STATE, CK),
                  per_layer(SSM_GROUPS // 2, 2, 2 * CK, LANES),
                  per_layer(*pwr.shape[1:]), per_layer(*pwi.shape[1:]),
                  per_layer(SSM_WIDTH, 1), per_layer(SSM_WIDTH, SSM_WIDTH), per_layer(SSM_WIDTH, 1)],
        out_specs=o_spec,
        scratch_shapes=[pltpu.VMEM((CHUNK, SSM_GROUPS, SSM_CH, n_chunk), BF16),
                        pltpu.VMEM((CHUNK, SSM_GROUPS, SSM_CH, n_chunk), F32),
                        pltpu.VMEM((CHUNK, SSM_WIDTH, n_chunk), F32),
                        pltpu.VMEM((PAIRS, 2, SCAN_BLOCK + n_chunk, LANES), F32),
                        pltpu.VMEM((PAIRS, 2, 2 * n_chunk // SCAN_BLOCK, LANES), F32)],
        compiler_params=pltpu.CompilerParams(dimension_semantics=("arbitrary", "arbitrary"),
                                             vmem_limit_bytes=VMEM_LIMIT),
        name=f"ssm_layer{layer}",
    )(x, norm_g, w_in_bt, wtws, wcp, pwr, pwi, d_skip, w_glu_t, b_glu)


def kernel(x, mem, positions, norm_g, w_in, q_norm_g, k_norm_g, sinks, lam_re, lam_im, log_dt, b_re, b_im,
           c_re, c_im, d_skip, w_glu, b_glu, mem_norm_g, w_mem_kv, xq_norm_g, xk_norm_g, w_out):
    b, s, d = x.shape
    depth = w_in.shape[0]
    assert s % TQ == 0 and (s // CHUNK) % LANES == 0
    n_scan = (s // CHUNK - 1).bit_length()

    o_su = _OFF_AG + ATTN_WIDTH
    o_xq = o_su + 2 * SSM_WIDTH
    w_in_act = jnp.swapaxes(jnp.concatenate([w_in[:, :, :o_su], w_in[:, :, o_xq:]], axis=-1), 1, 2).astype(BF16)
    w_in_bt = jnp.swapaxes(w_in[:, :, o_su:o_xq], 1, 2).astype(BF16)
    w_out_ac = jnp.concatenate([w_out[:, :ATTN_WIDTH], w_out[:, ATTN_WIDTH + SSM_WIDTH:]], axis=1).astype(BF16)
    w_out_b = w_out[:, ATTN_WIDTH:ATTN_WIDTH + SSM_WIDTH].astype(BF16)
    w_glu_t = jnp.swapaxes(w_glu, 1, 2).astype(BF16)
    norm_g3 = norm_g.reshape(depth, 1, d)
    over_tokens = lambda g: jnp.broadcast_to(g[:, :, None], g.shape + (SUB,))
    gq = over_tokens(q_norm_g * (LOG2E / math.sqrt(HEAD_DIM)))
    gk = over_tokens(k_norm_g)
    gxq = over_tokens(xq_norm_g * (LOG2E / math.sqrt(X_HEAD_DIM)))
    d_skip3 = d_skip.reshape(depth, SSM_WIDTH, 1)
    b_glu3 = b_glu.reshape(depth, SSM_WIDTH, 1)

    n_chunk = s // CHUNK
    pos_blocks = positions.reshape(b, s // BLOCK, ROWS, CHUNK).swapaxes(2, 3).reshape(b, s)
    cos_t, sin_t = _rope_tables(pos_blocks)
    mk, mvt = _mem_kv(mem, mem_norm_g, w_mem_kv, xk_norm_g)
    wtws, wcp, pwr, pwi = _s5_prep(lam_re, lam_im, log_dt, b_re, b_im, c_re, c_im, n_scan)

    xp = x.reshape(b, n_chunk, CHUNK, d).swapaxes(1, 2)
    for layer in range(depth):
        ob = _ssm_layer(layer, xp, norm_g3, w_in_bt, wtws, wcp, pwr, pwi, d_skip3, w_glu_t, b_glu3, n_scan)
        xp = _ac_layer(layer, xp, ob, cos_t, sin_t, sinks, norm_g3, w_in_act, gq, gk, mk, mvt, gxq, w_out_ac,
                       w_out_b, natural_out=layer == depth - 1)
    return xp
```

```python
import functools
import math

import jax
import jax.numpy as jnp
from jax import lax
from jax.experimental import pallas as pl
from jax.experimental.pallas import tpu as pltpu

F32 = jnp.float32
BF16 = jnp.bfloat16

EPS = 1e-6
ROPE_THETA = 10000.0
NEG_INF = -1e30
LOG2E = math.log2(math.e)

HEAD_DIM = 64
N_Q_HEADS = 8
N_KV_HEADS = 2
GQA_GROUP = N_Q_HEADS // N_KV_HEADS
BLOCK = 128
ATTN_WIDTH = N_Q_HEADS * HEAD_DIM
KV_WIDTH = N_KV_HEADS * HEAD_DIM
SSM_CH = 16
SSM_GROUPS = 32
SSM_STATE = 64
SSM_WIDTH = SSM_GROUPS * SSM_CH
X_HEADS = 4
X_HEAD_DIM = 128
X_WIDTH = X_HEADS * X_HEAD_DIM

LANES = 128
CHUNK = 16
CK = CHUNK * SSM_CH
ROWS = BLOCK // CHUNK
TQ = 1024
SUB = 512
SPS = 4
SM_ROWS = 32
OUT_CHUNKS = 4
SCAN_BLOCK = 8
PAIRS = 8
VMEM_LIMIT = 56 * 1024 * 1024

_OFF_Q = 0
_OFF_K = _OFF_Q + ATTN_WIDTH
_OFF_V = _OFF_K + KV_WIDTH
_OFF_AG = _OFF_V + KV_WIDTH
_OFF_XQ = _OFF_AG + ATTN_WIDTH
_OFF_XG = _OFF_XQ + X_WIDTH
AC_WIDTH = _OFF_XG + X_WIDTH


def _sigmoid(v):
    return 1.0 / (1.0 + jnp.exp(-v))


def _silu(v):
    return v * _sigmoid(v)


def _rms_rows(v, gain):
    return v * lax.rsqrt(jnp.mean(v * v, axis=-1, keepdims=True) + EPS) * gain


def _dot_nt(a, b):
    return lax.dot_general(a, b, (((1,), (1,)), ((), ())), preferred_element_type=F32)


def _rope_kernel(pos_ref, inv_ref, cos_ref, sin_ref):
    ang = inv_ref[...] * pos_ref[0].astype(F32)
    c, s = jnp.cos(ang), jnp.sin(ang)
    cos_ref[0] = jnp.concatenate([c, c], axis=0)
    sin_ref[0] = jnp.concatenate([-s, s], axis=0)


def _rope_tables(positions):
    b, s = positions.shape
    half = HEAD_DIM // 2
    inv = (ROPE_THETA ** (-jnp.arange(half, dtype=F32) / half)).reshape(half, 1)
    ts = min(s, 2048)
    spec = pl.BlockSpec((1, HEAD_DIM, ts), lambda i, j: (i, 0, j))
    return pl.pallas_call(
        _rope_kernel,
        out_shape=(jax.ShapeDtypeStruct((b, HEAD_DIM, s), F32),) * 2,
        grid=(b, s // ts),
        in_specs=[pl.BlockSpec((1, 1, ts), lambda i, j: (i, 0, j)),
                  pl.BlockSpec((HEAD_DIM // 2, 1), lambda i, j: (0, 0))],
        out_specs=(spec, spec),
        name="rope_tables",
    )(positions.reshape(b, 1, s), inv)


def _memkv_kernel(mem_ref, g_ref, w_ref, gk_ref, mk_ref, mv_ref):
    nb, n_mem, d = mem_ref.shape
    h = _rms_rows(mem_ref[...].reshape(nb * n_mem, d), g_ref[0]).astype(BF16)
    kv = jnp.dot(h, w_ref[0], preferred_element_type=F32)
    for hd in range(X_HEADS):
        sl = slice(hd * X_HEAD_DIM, (hd + 1) * X_HEAD_DIM)
        mk_ref[0, :, :, sl] = _rms_rows(kv[:, sl], gk_ref[0]).astype(BF16).reshape(nb, n_mem, X_HEAD_DIM)
    for bi in range(nb):
        mv_ref[0, bi] = kv[bi * n_mem:(bi + 1) * n_mem, X_WIDTH:].T.astype(BF16)


def _mem_kv(mem, mem_norm_g, w_mem_kv, xk_norm_g):
    b, n_mem, d = mem.shape
    depth = w_mem_kv.shape[0]
    out = jax.ShapeDtypeStruct((depth, b, n_mem, X_WIDTH), BF16)
    ospec = pl.BlockSpec((1, b, n_mem, X_WIDTH), lambda l: (l, 0, 0, 0))
    out_t = jax.ShapeDtypeStruct((depth, b, X_WIDTH, n_mem), BF16)
    ospec_t = pl.BlockSpec((1, b, X_WIDTH, n_mem), lambda l: (l, 0, 0, 0))
    return pl.pallas_call(
        _memkv_kernel,
        out_shape=(out, out_t),
        grid=(depth,),
        in_specs=[pl.BlockSpec((b, n_mem, d), lambda l: (0, 0, 0)),
                  pl.BlockSpec((1, 1, d), lambda l: (l, 0, 0)),
                  pl.BlockSpec((1, d, 2 * X_WIDTH), lambda l: (l, 0, 0)),
                  pl.BlockSpec((1, 1, X_HEAD_DIM), lambda l: (l, 0, 0))],
        out_specs=(ospec, ospec_t),
        compiler_params=pltpu.CompilerParams(vmem_limit_bytes=VMEM_LIMIT),
        name="mem_kv",
    )(mem, mem_norm_g.reshape(depth, 1, d), w_mem_kv.astype(BF16), xk_norm_g.reshape(depth, 1, X_HEAD_DIM))


def _s5_prep_kernel(lr_ref, li_ref, ldt_ref, brt_ref, bit_ref, crt_ref, cit_ref,
                    wtws_ref, wcp_ref, pwr_ref, pwi_ref, *, n_scan):
    n_groups = lr_ref.shape[1]
    tab_row = lax.broadcasted_iota(jnp.int32, (3 * CHUNK, LANES), 0)
    n_tab = jnp.where(tab_row < 2 * CHUNK, tab_row, 3 * CHUNK - 1 - tab_row).astype(F32)
    scan_row = lax.broadcasted_iota(jnp.int32, (2 * SCAN_BLOCK, LANES), 0)
    n_scan_rows = jnp.where(scan_row < SCAN_BLOCK,
                            jnp.left_shift(jnp.int32(CHUNK), jnp.minimum(scan_row, n_scan - 1)),
                            CHUNK * (scan_row - SCAN_BLOCK + 1)).astype(F32)
    low = lax.broadcasted_iota(jnp.int32, (CK, LANES), 1) < SSM_STATE
    lane_s = lax.broadcasted_iota(jnp.int32, (CK, LANES), 1) // SSM_CH
    hi = lax.Precision.HIGHEST

    def per_chunk_row(tab, first):
        rows = tab[first:first + CHUNK]
        return jnp.broadcast_to(rows[:, None, :], (CHUNK, SSM_CH, LANES)).reshape(CK, LANES)

    low_rows = lax.broadcasted_iota(jnp.int32, pwr_ref.shape[2:], 1) < SSM_STATE

    def group(g, gp, e):
        own = low if e == 0 else jnp.logical_not(low)
        dt = jnp.exp(ldt_ref[0, g])
        lr, li = lr_ref[0, g], li_ref[0, g]

        def power(n):
            mag = jnp.exp(lr * dt * n)
            return mag * jnp.cos(li * dt * n), mag * jnp.sin(li * dt * n)

        tab_r, tab_i = power(n_tab)
        ar, ai = tab_r[1:2], tab_i[1:2]
        den = lr * lr + li * li
        fr = ((ar - 1.0) * lr + ai * li) / den
        fi = (ai * lr - (ar - 1.0) * li) / den
        every_chunk_row = lambda a: jnp.concatenate([a] * CHUNK, axis=0)
        brt, bit = every_chunk_row(brt_ref[0, g]), every_chunk_row(bit_ref[0, g])
        bbr = fr * brt - fi * bit
        bbi = fr * bit + fi * brt

        crt, cit = every_chunk_row(crt_ref[0, g]), every_chunk_row(cit_ref[0, g])

        def c_times_power(first):
            pr, pi = per_chunk_row(tab_r, first), per_chunk_row(tab_i, first)
            return crt * pr - cit * pi, -(crt * pi + cit * pr)

        g_re, g_mim = c_times_power(0)
        d = lax.dot_general(jnp.where(low, g_re, g_mim), jnp.where(low, bbr, bbi), (((1,), (1,)), ((), ())),
                            precision=hi, preferred_element_type=F32)
        per_tile = LANES // SSM_CH
        for tile in range(CK // LANES):
            d_t = d[:, tile * LANES:(tile + 1) * LANES]
            wt = jnp.zeros((CK, LANES), F32)
            for s in range(tile * per_tile, (tile + 1) * per_tile):
                shifted = d_t if s == 0 else jnp.concatenate(
                    [jnp.zeros((s * SSM_CH, LANES), F32), d_t[:CK - s * SSM_CH]], axis=0)
                wt = jnp.where(lane_s == s - tile * per_tile, shifted, wt)
            wtws_ref[0, g, 0:CK, tile * LANES:(tile + 1) * LANES] = wt.astype(BF16)

        qr, qi = per_chunk_row(tab_r, 2 * CHUNK), per_chunk_row(tab_i, 2 * CHUNK)
        ws = jnp.where(low, qr * bbr - qi * bbi, qr * bbi + qi * bbr)
        wtws_ref[0, g, CK:, :] = ws.T.astype(BF16)

        c_re, c_mim = c_times_power(1)
        wcp_ref[0, gp, 0, e * CK:(e + 1) * CK, :] = jnp.where(own, c_re, 0.0).astype(BF16)
        wcp_ref[0, gp, 1, e * CK:(e + 1) * CK, :] = jnp.where(own, c_mim, 0.0).astype(BF16)

        return power(n_scan_rows)

    def pair(gp, carry):
        (r0, i0), (r1, i1) = group(2 * gp, gp, 0), group(2 * gp + 1, gp, 1)
        pwr_ref[0, gp] = jnp.where(low_rows, r0, r1)
        pwi_ref[0, gp] = jnp.where(low_rows, i0, i1)
        return carry

    lax.fori_loop(0, n_groups // 2, pair, 0)


def _s5_prep(lam_re, lam_im, log_dt, b_re, b_im, c_re, c_im, n_scan):
    depth, g, p = lam_re.shape
    twice = lambda a: jnp.concatenate([a, a], axis=-1)
    row = lambda a: twice(a).reshape(depth, g, 1, 2 * p)
    b_t = lambda a: twice(jnp.swapaxes(a, 2, 3))
    c_t = twice
    assert n_scan <= SCAN_BLOCK
    n_rows = 2 * SCAN_BLOCK
    gb = 8
    spec = lambda *shape: pl.BlockSpec((1, gb) + shape, lambda l, i: (l, i, 0, 0))
    pair_spec = lambda *shape: pl.BlockSpec((1, gb // 2) + shape, lambda l, i: (l, i) + (0,) * len(shape))
    return pl.pallas_call(
        functools.partial(_s5_prep_kernel, n_scan=n_scan),
        out_shape=(jax.ShapeDtypeStruct((depth, g, CK + 2 * p, CK), BF16),
                   jax.ShapeDtypeStruct((depth, g // 2, 2, 2 * CK, 2 * p), BF16),
                   jax.ShapeDtypeStruct((depth, g // 2, n_rows, 2 * p), F32),
                   jax.ShapeDtypeStruct((depth, g // 2, n_rows, 2 * p), F32)),
        grid=(depth, g // gb),
        in_specs=[spec(1, 2 * p), spec(1, 2 * p), spec(1, 1),
                  spec(SSM_CH, 2 * p), spec(SSM_CH, 2 * p), spec(SSM_CH, 2 * p), spec(SSM_CH, 2 * p)],
        out_specs=(spec(CK + 2 * p, CK), pair_spec(2, 2 * CK, 2 * p), pair_spec(n_rows, 2 * p),
                   pair_spec(n_rows, 2 * p)),
        compiler_params=pltpu.CompilerParams(vmem_limit_bytes=VMEM_LIMIT),
        name="s5_prep",
    )(row(lam_re), row(lam_im), log_dt.reshape(depth, g, 1, 1),
      b_t(b_re), b_t(b_im), c_t(c_re), c_t(c_im))


def _col_rms(v):
    return lax.rsqrt(jnp.mean(v * v, axis=0, keepdims=True) + EPS)


def _ac_kernel(sinks_ref, x_ref, ob_ref, cos_ref, sin_ref, ng_ref, win_ref, gq_ref, gk_ref, mk_ref, mvt_ref,
               gxq_ref, wout_ref, woutb_ref, out_ref, k_ref, v_ref, shuf_ref, sc_ref, pt_ref, bias_ref, *sub_refs,
               layer, natural_out):
    i = pl.program_id(1)
    d_model = x_ref.shape[-1]
    half = HEAD_DIM // 2
    n_mem = mk_ref.shape[2]

    @pl.when(i == 0)
    def _():
        k_ref[0:BLOCK] = jnp.zeros((BLOCK, LANES), BF16)
        v_ref[:, 0:BLOCK] = jnp.zeros((KV_WIDTH, BLOCK), BF16)

    key_row = lax.broadcasted_iota(jnp.int32, (2 * BLOCK, BLOCK), 0)
    q_col = lax.broadcasted_iota(jnp.int32, (2 * BLOCK, BLOCK), 1)
    local_tok = lambda rho: (rho % ROWS) * CHUNK + rho // ROWS
    qi = local_tok(q_col)
    kj = (key_row // BLOCK) * BLOCK + local_tok(key_row % BLOCK)
    band = (kj >= qi + 1) & (kj <= qi + BLOCK)
    band_first = band & (kj >= jnp.where(i == 0, BLOCK, 0))
    bias_ref[0] = jnp.where(band_first, 0.0, NEG_INF)
    bias_ref[1] = jnp.where(band, 0.0, NEG_INF)
    ones_rows = jnp.ones((2 * ROWS, 2 * BLOCK), BF16)
    ones_mem = jnp.ones((2 * ROWS, n_mem), BF16)
    zeros_q = jnp.zeros((HEAD_DIM, GQA_GROUP * BLOCK), BF16)

    def tokens(blocks):
        return jnp.concatenate([x_ref[0, :, r * ROWS:(r + 1) * ROWS, :].reshape(BLOCK, d_model) for r in blocks],
                               axis=0)

    n_sub = TQ // SUB
    z_refs, q_refs, mix_refs, h_refs, ob_refs = (sub_refs[k * n_sub:(k + 1) * n_sub] for k in range(5))

    def part(refs, rows, lanes):
        sub = lanes.start // SUB
        return refs[sub].at[rows, lanes.start - sub * SUB:lanes.stop - sub * SUB]

    lanes_of = lambda sub: slice(sub * SUB, (sub + 1) * SUB)
    blocks_of = lambda sub: range(sub * SUB // BLOCK, (sub + 1) * SUB // BLOCK)
    proj_rows = (slice(0, _OFF_AG), slice(_OFF_AG, _OFF_XQ), slice(_OFF_XQ, _OFF_XG), slice(_OFF_XG, AC_WIDTH))
    out_cols = tuple(slice(c * d_model // OUT_CHUNKS, (c + 1) * d_model // OUT_CHUNKS) for c in range(OUT_CHUNKS))

    def hidden(sub):
        h_refs[sub][...] = _rms_rows(tokens(blocks_of(sub)), ng_ref[0]).astype(BF16)
        ob_refs[sub][...] = jnp.concatenate(
            [ob_ref[0, :, r * ROWS:(r + 1) * ROWS, :].reshape(BLOCK, SSM_WIDTH) for r in blocks_of(sub)],
            axis=0).astype(BF16)

    def project(sub, c):
        z_refs[sub][proj_rows[c], :] = _dot_nt(win_ref[0, proj_rows[c], :], h_refs[sub][...])

    def qkv(sub):
        lanes = lanes_of(sub)
        cos_t, sin_t = cos_ref[0, :, lanes], sin_ref[0, :, lanes]

        def head_norm_rope(v, gain):
            vn = v * _col_rms(v) * gain
            return vn * cos_t + jnp.concatenate([vn[half:], vn[:half]], axis=0) * sin_t

        kt = jnp.concatenate(
            [head_norm_rope(z_refs[sub][_OFF_K + hk * HEAD_DIM:_OFF_K + (hk + 1) * HEAD_DIM, :], gk_ref[0])
             for hk in range(N_KV_HEADS)], axis=0)
        k_ref[BLOCK + sub * SUB:BLOCK + (sub + 1) * SUB] = kt.T.astype(BF16)
        v_ref[:, BLOCK + sub * SUB:BLOCK + (sub + 1) * SUB] = z_refs[sub][_OFF_V:_OFF_V + KV_WIDTH, :].astype(BF16)
        for hq in range(N_Q_HEADS):
            rows = slice(hq * HEAD_DIM, (hq + 1) * HEAD_DIM)
            q_refs[sub][rows, :] = head_norm_rope(z_refs[sub][rows, :], gq_ref[0]).astype(BF16)

    def column_softmax(slot, bias_idx, floor):
        n_rows, width = sc_ref.shape[1:]

        def chunk(c):
            rows = slice(c * SM_ROWS, (c + 1) * SM_ROWS)
            s = sc_ref[slot, rows, :]
            if bias_idx is not None:
                s = s + jnp.concatenate([bias_ref[bias_idx, rows, :]] * (width // BLOCK), axis=1)
            return rows, s

        m = None
        for c in range(n_rows // SM_ROWS):
            _, s = chunk(c)
            cm = jnp.max(s.reshape(SM_ROWS // 8, 8, width), axis=0)
            m = cm if m is None else jnp.maximum(m, cm)
        m = jnp.max(m, axis=0, keepdims=True)
        if floor is not None:
            m = jnp.maximum(m, floor)
        for c in range(n_rows // SM_ROWS):
            rows, s = chunk(c)
            pt_ref[slot, rows, :] = jnp.exp2(s - m).astype(BF16)
        return m

    def attend(r, hk, slot):
        cols = slice(r * BLOCK, (r + 1) * BLOCK)
        keys = k_ref[r * BLOCK:(r + 2) * BLOCK]
        heads = range(hk * GQA_GROUP, (hk + 1) * GQA_GROUP)
        qt = jnp.concatenate([part(q_refs, slice(hq * HEAD_DIM, (hq + 1) * HEAD_DIM), cols)[...] for hq in heads],
                             axis=1)
        qt = jnp.concatenate([qt, zeros_q] if hk == 0 else [zeros_q, qt], axis=0)
        sc_ref[slot] = jnp.dot(keys, qt, preferred_element_type=F32)
        sink = jnp.concatenate([jnp.full((1, BLOCK), sinks_ref[layer, hq] * LOG2E, F32) for hq in heads], axis=1)
        m = column_softmax(slot, 0 if r == 0 else 1, sink)
        return jnp.exp2(sink - m)

    def attend_finish(r, hk, slot, esink):
        cols = slice(r * BLOCK, (r + 1) * BLOCK)
        vals = jnp.concatenate(
            [v_ref[hk * HEAD_DIM:(hk + 1) * HEAD_DIM, r * BLOCK:(r + 2) * BLOCK], ones_rows], axis=0)
        o = jnp.dot(vals, pt_ref[slot], preferred_element_type=F32)
        o = o[:HEAD_DIM] * (1.0 / (o[HEAD_DIM:HEAD_DIM + 1] + esink))
        for g, hq in enumerate(range(hk * GQA_GROUP, (hk + 1) * GQA_GROUP)):
            rows = slice(hq * HEAD_DIM, (hq + 1) * HEAD_DIM)
            gate = part(z_refs, slice(_OFF_AG + hq * HEAD_DIM, _OFF_AG + (hq + 1) * HEAD_DIM), cols)[...]
            part(mix_refs, rows, cols)[...] = (o[:, g * BLOCK:(g + 1) * BLOCK] * _silu(gate)).astype(BF16)

    def cross(sub, hd, slot):
        rows = slice(hd * X_HEAD_DIM, (hd + 1) * X_HEAD_DIM)
        xq = z_refs[sub][_OFF_XQ + hd * X_HEAD_DIM:_OFF_XQ + (hd + 1) * X_HEAD_DIM, :]
        xq = (xq * _col_rms(xq) * gxq_ref[0]).astype(BF16)
        sc_ref[slot] = jnp.dot(mk_ref[0, 0, :, rows], xq, preferred_element_type=F32)
        column_softmax(slot, None, None)

    def cross_finish(sub, hd, slot, _):
        rows = slice(hd * X_HEAD_DIM, (hd + 1) * X_HEAD_DIM)
        vals = jnp.concatenate([mvt_ref[0, 0, rows, :], ones_mem], axis=0)
        o = jnp.dot(vals, pt_ref[slot], preferred_element_type=F32)
        o = o[:X_HEAD_DIM] * (1.0 / o[X_HEAD_DIM:X_HEAD_DIM + 1])
        gate = z_refs[sub][_OFF_XG + hd * X_HEAD_DIM:_OFF_XG + (hd + 1) * X_HEAD_DIM, :]
        mix_refs[sub][ATTN_WIDTH + hd * X_HEAD_DIM:ATTN_WIDTH + (hd + 1) * X_HEAD_DIM, :] = (
            o * _silu(gate)).astype(BF16)

    def output(sub, c):
        cols = out_cols[c]
        width = d_model // OUT_CHUNKS
        y = lax.dot_general(mix_refs[sub][...], wout_ref[0, :, cols], (((0,), (0,)), ((), ())),
                            preferred_element_type=F32)
        y = y + jnp.dot(ob_refs[sub][...], woutb_ref[0, :, cols], preferred_element_type=F32)
        for n, r in enumerate(blocks_of(sub)):
            xr = x_ref[0, :, r * ROWS:(r + 1) * ROWS, cols]
            res = xr + y[n * BLOCK:(n + 1) * BLOCK].reshape(CHUNK, ROWS, width)
            if not natural_out:
                out_ref[0, :, r * ROWS:(r + 1) * ROWS, cols] = res
                continue
            res = res.reshape(BLOCK, width)
            for slab in range(width // LANES):
                shuf_ref[c, n, slab] = res[:, slab * LANES:(slab + 1) * LANES]
                lanes = slice(cols.start + slab * LANES, cols.start + (slab + 1) * LANES)
                for ch in range(ROWS):
                    out_ref[0, r * BLOCK + ch * CHUNK:r * BLOCK + (ch + 1) * CHUNK, lanes] = (
                        shuf_ref[c, n, slab, pl.ds(ch, CHUNK, stride=ROWS), :])

    hidden(0)
    project(0, 0)
    qkv(0)
    for c in range(1, len(proj_rows)):
        project(0, c)
    for sub in range(n_sub):
        units = [(attend, attend_finish, (r, hk)) for r in blocks_of(sub) for hk in range(N_KV_HEADS)]
        units += [(cross, cross_finish, (sub, hd)) for hd in range(X_HEADS)]
        carried = {}

        def stage(n):
            if n < len(units):
                carried[n] = units[n][0](*units[n][2], n % 2)
            if n >= 1:
                units[n - 1][1](*units[n - 1][2], (n - 1) % 2, carried.pop(n - 1))

        vec = [functools.partial(stage, n) for n in range(len(units) + 1)]
        mxu = []
        if sub + 1 < n_sub:
            hidden(sub + 1)
            mxu += [functools.partial(project, sub + 1, c) for c in range(len(proj_rows))]
        if sub >= 1:
            mxu += [functools.partial(output, sub - 1, c) for c in range(OUT_CHUNKS)]
        every = -(-len(vec) // max(len(mxu), 1))
        for n, task in enumerate(vec):
            if n % every == 0 and mxu:
                mxu.pop(0)()
            task()
        for task in mxu:
            task()
        if sub + 1 < n_sub:
            qkv(sub + 1)
    for c in range(OUT_CHUNKS):
        output(n_sub - 1, c)

    k_ref[0:BLOCK] = k_ref[TQ:TQ + BLOCK]
    v_ref[:, 0:BLOCK] = v_ref[:, TQ:TQ + BLOCK]


def _ac_layer(layer, x, ob, cos_t, sin_t, sinks, norm_g, w_in_act, gq, gk, mk, mvt, gxq, w_out_ac, w_out_b,
              natural_out):
    b, _, n_chunk, d = x.shape
    n_mem = mk.shape[2]
    n_sub = TQ // SUB
    assert n_mem == 2 * BLOCK and SUB == GQA_GROUP * BLOCK
    if natural_out:
        out_shape = jax.ShapeDtypeStruct((b, n_chunk * CHUNK, d), F32)
        out_spec = pl.BlockSpec((1, TQ, d), lambda bi, i: (bi, i, 0))
    else:
        out_shape = jax.ShapeDtypeStruct(x.shape, F32)
        out_spec = pl.BlockSpec((1, CHUNK, TQ // CHUNK, d), lambda bi, i: (bi, 0, i, 0))
    tab = pl.BlockSpec((1, HEAD_DIM, TQ), lambda bi, i: (bi, 0, i))
    xspec = pl.BlockSpec((1, CHUNK, TQ // CHUNK, d), lambda bi, i: (bi, 0, i, 0))
    obspec = pl.BlockSpec((1, CHUNK, TQ // CHUNK, SSM_WIDTH), lambda bi, i: (bi, 0, i, 0))
    per_layer = lambda *shape: pl.BlockSpec((1,) + shape, lambda bi, i: (layer,) + (0,) * len(shape),
                                            pipeline_mode=pl.Buffered(1))
    return pl.pallas_call(
        functools.partial(_ac_kernel, layer=layer, natural_out=natural_out),
        out_shape=out_shape,
        grid=(b, n_chunk * CHUNK // TQ),
        in_specs=[pl.BlockSpec(memory_space=pltpu.SMEM),
                  xspec, obspec, tab, tab,
                  per_layer(1, d), per_layer(AC_WIDTH, d), per_layer(HEAD_DIM, SUB), per_layer(HEAD_DIM, SUB),
                  pl.BlockSpec((1, 1, n_mem, X_WIDTH), lambda bi, i: (layer, bi, 0, 0)),
                  pl.BlockSpec((1, 1, X_WIDTH, n_mem), lambda bi, i: (layer, bi, 0, 0)),
                  per_layer(X_HEAD_DIM, SUB), per_layer(ATTN_WIDTH + X_WIDTH, d), per_layer(SSM_WIDTH, d)],
        out_specs=out_spec,
        scratch_shapes=([pltpu.VMEM((BLOCK + TQ, LANES), BF16),
                         pltpu.VMEM((KV_WIDTH, BLOCK + TQ), BF16),
                         pltpu.VMEM((OUT_CHUNKS, SUB // BLOCK, d // OUT_CHUNKS // LANES, BLOCK, LANES), F32),
                         pltpu.VMEM((2, 2 * BLOCK, SUB), F32),
                         pltpu.VMEM((2, 2 * BLOCK, SUB), BF16),
                         pltpu.VMEM((2, 2 * BLOCK, BLOCK), F32)]
                        + [pltpu.VMEM((AC_WIDTH, SUB), F32)] * n_sub
                        + [pltpu.VMEM((ATTN_WIDTH, SUB), BF16)] * n_sub
                        + [pltpu.VMEM((ATTN_WIDTH + X_WIDTH, SUB), BF16)] * n_sub
                        + [pltpu.VMEM((SUB, d), BF16)] * n_sub
                        + [pltpu.VMEM((SUB, SSM_WIDTH), BF16)] * n_sub),
        compiler_params=pltpu.CompilerParams(dimension_semantics=("arbitrary", "arbitrary"),
                                             vmem_limit_bytes=VMEM_LIMIT),
        name=f"attn_layer{layer}",
    )(sinks, x, ob, cos_t, sin_t, norm_g, w_in_act, gq, gk, mk, mvt, gxq, w_out_ac, w_out_b)


def _ssm_kernel(x_ref, ng_ref, winb_ref, wtws_ref, wcp_ref, pwr_ref, pwi_ref, dsk_ref, wglu_ref, bglu_ref,
                out_ref, z_ref, y_ref, sg_ref, hs_ref, es_ref, *, n_chunk, n_scan):
    step = pl.program_id(1)
    n_proj = CHUNK // SPS
    d_model = x_ref.shape[-1]

    @pl.when(step < n_proj)
    def _project():
        h = _rms_rows(x_ref[0].reshape(SPS * n_chunk, d_model), ng_ref[0]).astype(BF16)
        ut = _dot_nt(winb_ref[0], h)
        for e in range(SPS):
            s = SPS * step + e
            u = ut[:SSM_WIDTH, e * n_chunk:(e + 1) * n_chunk]
            z_ref[s] = u.astype(BF16).reshape(SSM_GROUPS, SSM_CH, n_chunk)
            y_ref[s] = (u * dsk_ref[0]).reshape(SSM_GROUPS, SSM_CH, n_chunk)
            sg_ref[s] = _silu(ut[SSM_WIDTH:, e * n_chunk:(e + 1) * n_chunk])

    @pl.when(step == n_proj - 1)
    def _chunks():
        n_blk = n_chunk // SCAN_BLOCK
        lvl1 = SCAN_BLOCK.bit_length() - 1
        hs_ref[:, :, 0:SCAN_BLOCK, :] = jnp.zeros((PAIRS, 2, SCAN_BLOCK, LANES), F32)
        es_ref[:, :, 0:n_blk, :] = jnp.zeros((PAIRS, 2, n_blk, LANES), F32)
        rows = pl.ds(SCAN_BLOCK, n_chunk)
        in_block = lax.broadcasted_iota(jnp.int32, (n_chunk, LANES), 0) % SCAN_BLOCK

        def mul_add(hr, hi, ar, ai, sr, si):
            return hr + ar * sr - ai * si, hi + ar * si + ai * sr

        def pairs(it, carry):
            gps = [PAIRS * it + k for k in range(PAIRS)]
            zs = [[z_ref[:, 2 * gp + e].reshape(CK, n_chunk) for e in range(2)] for gp in gps]
            power = lambda gp, j: (pwr_ref[0, gp, j:j + 1, :], pwi_ref[0, gp, j:j + 1, :])
            hr, hi = [], []
            for k, gp in enumerate(gps):
                s = [jnp.dot(wtws_ref[0, 2 * gp + e, CK:, :], zs[k][e], preferred_element_type=F32)
                     for e in range(2)]
                hr.append(jnp.concatenate([s[0][:SSM_STATE], s[1][:SSM_STATE]], axis=0).T)
                hi.append(jnp.concatenate([s[0][SSM_STATE:], s[1][SSM_STATE:]], axis=0).T)

            def put(k):
                hs_ref[k, 0, rows, :] = hr[k]
                hs_ref[k, 1, rows, :] = hi[k]

            for j in range(lvl1):
                for k in range(PAIRS):
                    put(k)
                for k, gp in enumerate(gps):
                    keep = in_block >= (1 << j)
                    sr = jnp.where(keep, hs_ref[k, 0, pl.ds(SCAN_BLOCK - (1 << j), n_chunk), :], 0.0)
                    si = jnp.where(keep, hs_ref[k, 1, pl.ds(SCAN_BLOCK - (1 << j), n_chunk), :], 0.0)
                    hr[k], hi[k] = mul_add(hr[k], hi[k], *power(gp, j), sr, si)
            for k in range(PAIRS):
                put(k)
            last = pl.ds(2 * SCAN_BLOCK - 1, n_blk, stride=SCAN_BLOCK)
            er = [hs_ref[k, 0, last, :] for k in range(PAIRS)]
            ei = [hs_ref[k, 1, last, :] for k in range(PAIRS)]
            blk = pl.ds(n_blk, n_blk)
            for j in range(lvl1, n_scan + 1):
                shift = (1 << (j - lvl1)) if j < n_scan else 1
                for k in range(PAIRS):
                    es_ref[k, 0, blk, :] = er[k]
                    es_ref[k, 1, blk, :] = ei[k]
                for k, gp in enumerate(gps):
                    sr, si = es_ref[k, 0, pl.ds(n_blk - shift, n_blk), :], es_ref[k, 1, pl.ds(n_blk - shift, n_blk), :]
                    if j < n_scan:
                        er[k], ei[k] = mul_add(er[k], ei[k], *power(gp, j), sr, si)
                    else:
                        spread = lambda v: jnp.broadcast_to(v[:, None, :], (n_blk, SCAN_BLOCK, LANES)).reshape(
                            n_chunk, LANES)
                        pr = jnp.concatenate([pwr_ref[0, gp, SCAN_BLOCK:2 * SCAN_BLOCK, :]] * n_blk, axis=0)
                        pi = jnp.concatenate([pwi_ref[0, gp, SCAN_BLOCK:2 * SCAN_BLOCK, :]] * n_blk, axis=0)
                        hr[k], hi[k] = mul_add(hr[k], hi[k], pr, pi, spread(sr), spread(si))
            for k in range(PAIRS):
                put(k)
            for k, gp in enumerate(gps):
                sr = hs_ref[k, 0, pl.ds(SCAN_BLOCK - 1, n_chunk), :]
                si = hs_ref[k, 1, pl.ds(SCAN_BLOCK - 1, n_chunk), :]
                yc = (_dot_nt(wcp_ref[0, gp, 0], sr.astype(BF16))
                      + _dot_nt(wcp_ref[0, gp, 1], si.astype(BF16)))
                for e in range(2):
                    y = jnp.dot(wtws_ref[0, 2 * gp + e, 0:CK, :], zs[k][e], preferred_element_type=F32)
                    y = y + yc[e * CK:(e + 1) * CK]
                    y_ref[:, 2 * gp + e] = y_ref[:, 2 * gp + e] + y.reshape(CHUNK, SSM_CH, n_chunk)
            return carry

        lax.fori_loop(0, SSM_GROUPS // (2 * PAIRS), pairs, 0)

    @pl.when(step >= n_proj)
    def _finish():
        t0 = SPS * (step - n_proj)
        y = jnp.concatenate([y_ref[t0 + e].reshape(SSM_WIDTH, n_chunk) for e in range(SPS)], axis=1)
        sg = jnp.concatenate([sg_ref[t0 + e] for e in range(SPS)], axis=1)
        y = jax.nn.gelu(y)
        gate = _sigmoid(jnp.dot(wglu_ref[0], y.astype(BF16), preferred_element_type=F32) + bglu_ref[0])
        out_ref[0] = (y * gate * sg).T.reshape(SPS, n_chunk, SSM_WIDTH)


def _ssm_layer(layer, x, norm_g, w_in_bt, wtws, wcp, pwr, pwi, d_skip, w_glu_t, b_glu, n_scan):
    b, _, n_chunk, d = x.shape
    n_proj = CHUNK // SPS
    per_layer = lambda *shape: pl.BlockSpec((1,) + shape, lambda bi, st: (layer,) + (0,) * len(shape),
                                            pipeline_mode=pl.Buffered(1))
    x_spec = pl.BlockSpec((1, SPS, n_chunk, d), lambda bi, st: (bi, jnp.minimum(st, n_proj - 1), 0, 0))
    o_spec = pl.BlockSpec((1, SPS, n_chunk, SSM_WIDTH), lambda bi, st: (bi, jnp.maximum(st - n_proj, 0), 0, 0))
    return pl.pallas_call(
        functools.partial(_ssm_kernel, n_chunk=n_chunk, n_scan=n_scan),
        out_shape=jax.ShapeDtypeStruct(x.shape[:3] + (SSM_WIDTH,), F32),
        grid=(b, 2 * n_proj),
        in_specs=[x_spec,
                  per_layer(1, d), per_layer(2 * SSM_WIDTH, d),
                  per_layer(SSM_GROUPS, CK + 2 * SSM_STATE, CK),
                  per_layer(SSM_GROUPS // 2, 2, 2 * CK, LANES),
                  per_layer(*pwr.shape[1:]), per_layer(*pwi.shape[1:]),
                  per_layer(SSM_WIDTH, 1), per_layer(SSM_WIDTH, SSM_WIDTH), per_layer(SSM_WIDTH, 1)],
        out_specs=o_spec,
        scratch_shapes=[pltpu.VMEM((CHUNK, SSM_GROUPS, SSM_CH, n_chunk), BF16),
                        pltpu.VMEM((CHUNK, SSM_GROUPS, SSM_CH, n_chunk), F32),
                        pltpu.VMEM((CHUNK, SSM_WIDTH, n_chunk), F32),
                        pltpu.VMEM((PAIRS, 2, SCAN_BLOCK + n_chunk, LANES), F32),
                        pltpu.VMEM((PAIRS, 2, 2 * n_chunk // SCAN_BLOCK, LANES), F32)],
        compiler_params=pltpu.CompilerParams(dimension_semantics=("arbitrary", "arbitrary"),
                                             vmem_limit_bytes=VMEM_LIMIT),
        name=f"ssm_layer{layer}",
    )(x, norm_g, w_in_bt, wtws, wcp, pwr, pwi, d_skip, w_glu_t, b_glu)


def kernel(x, mem, positions, norm_g, w_in, q_norm_g, k_norm_g, sinks, lam_re, lam_im, log_dt, b_re, b_im,
           c_re, c_im, d_skip, w_glu, b_glu, mem_norm_g, w_mem_kv, xq_norm_g, xk_norm_g, w_out):
    b, s, d = x.shape
    depth = w_in.shape[0]
    assert s % TQ == 0 and (s // CHUNK) % LANES == 0
    n_scan = (s // CHUNK - 1).bit_length()

    o_su = _OFF_AG + ATTN_WIDTH
    o_xq = o_su + 2 * SSM_WIDTH
    w_in_act = jnp.swapaxes(jnp.concatenate([w_in[:, :, :o_su], w_in[:, :, o_xq:]], axis=-1), 1, 2).astype(BF16)
    w_in_bt = jnp.swapaxes(w_in[:, :, o_su:o_xq], 1, 2).astype(BF16)
    w_out_ac = jnp.concatenate([w_out[:, :ATTN_WIDTH], w_out[:, ATTN_WIDTH + SSM_WIDTH:]], axis=1).astype(BF16)
    w_out_b = w_out[:, ATTN_WIDTH:ATTN_WIDTH + SSM_WIDTH].astype(BF16)
    w_glu_t = jnp.swapaxes(w_glu, 1, 2).astype(BF16)
    norm_g3 = norm_g.reshape(depth, 1, d)
    over_tokens = lambda g: jnp.broadcast_to(g[:, :, None], g.shape + (SUB,))
    gq = over_tokens(q_norm_g * (LOG2E / math.sqrt(HEAD_DIM)))
    gk = over_tokens(k_norm_g)
    gxq = over_tokens(xq_norm_g * (LOG2E / math.sqrt(X_HEAD_DIM)))
    d_skip3 = d_skip.reshape(depth, SSM_WIDTH, 1)
    b_glu3 = b_glu.reshape(depth, SSM_WIDTH, 1)

    n_chunk = s // CHUNK
    pos_blocks = positions.reshape(b, s // BLOCK, ROWS, CHUNK).swapaxes(2, 3).reshape(b, s)
    cos_t, sin_t = _rope_tables(pos_blocks)
    mk, mvt = _mem_kv(mem, mem_norm_g, w_mem_kv, xk_norm_g)
    wtws, wcp, pwr, pwi = _s5_prep(lam_re, lam_im, log_dt, b_re, b_im, c_re, c_im, n_scan)

    xp = x.reshape(b, n_chunk, CHUNK, d).swapaxes(1, 2)
    for layer in range(depth):
        ob = _ssm_layer(layer, xp, norm_g3, w_in_bt, wtws, wcp, pwr, pwi, d_skip3, w_glu_t, b_glu3, n_scan)
        xp = _ac_layer(layer, xp, ob, cos_t, sin_t, sinks, norm_g3, w_in_act, gq, gk, mk, mvt, gxq, w_out_ac,
                       w_out_b, natural_out=layer == depth - 1)
    return xp
```

```python
import functools
import math

import jax
import jax.numpy as jnp
from jax import lax
from jax.experimental import pallas as pl
from jax.experimental.pallas import tpu as pltpu

F32 = jnp.float32
BF16 = jnp.bfloat16

EPS = 1e-6
ROPE_THETA = 10000.0
NEG_INF = -1e30
LOG2E = math.log2(math.e)

HEAD_DIM = 64
N_Q_HEADS = 8
N_KV_HEADS = 2
GQA_GROUP = N_Q_HEADS // N_KV_HEADS
BLOCK = 128
ATTN_WIDTH = N_Q_HEADS * HEAD_DIM
KV_WIDTH = N_KV_HEADS * HEAD_DIM
SSM_CH = 16
SSM_GROUPS = 32
SSM_STATE = 64
SSM_WIDTH = SSM_GROUPS * SSM_CH
X_HEADS = 4
X_HEAD_DIM = 128
X_WIDTH = X_HEADS * X_HEAD_DIM

LANES = 128
CHUNK = 16
CK = CHUNK * SSM_CH
ROWS = BLOCK // CHUNK
TQ = 1024
SUB = 512
SPS = 4
SPS_IN = 8
SM_ROWS = 32
OUT_CHUNKS = 4
SCAN_BLOCK = 8
PAIRS = 8
VMEM_LIMIT = 56 * 1024 * 1024

_OFF_Q = 0
_OFF_K = _OFF_Q + ATTN_WIDTH
_OFF_V = _OFF_K + KV_WIDTH
_OFF_AG = _OFF_V + KV_WIDTH
_OFF_XQ = _OFF_AG + ATTN_WIDTH
_OFF_XG = _OFF_XQ + X_WIDTH
AC_WIDTH = _OFF_XG + X_WIDTH


def _sigmoid(v):
    return 1.0 / (1.0 + jnp.exp(-v))


def _silu(v):
    return v * _sigmoid(v)


def _rms_rows(v, gain):
    return v * lax.rsqrt(jnp.mean(v * v, axis=-1, keepdims=True) + EPS) * gain


def _dot_nt(a, b):
    return lax.dot_general(a, b, (((1,), (1,)), ((), ())), preferred_element_type=F32)


def _rope_kernel(pos_ref, inv_ref, cos_ref, sin_ref):
    ang = inv_ref[...] * pos_ref[0].astype(F32)
    c, s = jnp.cos(ang), jnp.sin(ang)
    cos_ref[0] = jnp.concatenate([c, c], axis=0)
    sin_ref[0] = jnp.concatenate([-s, s], axis=0)


def _rope_tables(positions):
    b, s = positions.shape
    half = HEAD_DIM // 2
    inv = (ROPE_THETA ** (-jnp.arange(half, dtype=F32) / half)).reshape(half, 1)
    ts = min(s, 2048)
    spec = pl.BlockSpec((1, HEAD_DIM, ts), lambda i, j: (i, 0, j))
    return pl.pallas_call(
        _rope_kernel,
        out_shape=(jax.ShapeDtypeStruct((b, HEAD_DIM, s), F32),) * 2,
        grid=(b, s // ts),
        in_specs=[pl.BlockSpec((1, 1, ts), lambda i, j: (i, 0, j)),
                  pl.BlockSpec((HEAD_DIM // 2, 1), lambda i, j: (0, 0))],
        out_specs=(spec, spec),
        name="rope_tables",
    )(positions.reshape(b, 1, s), inv)


def _memkv_kernel(mem_ref, g_ref, w_ref, gk_ref, mk_ref, mv_ref):
    nb, n_mem, d = mem_ref.shape
    h = _rms_rows(mem_ref[...].reshape(nb * n_mem, d), g_ref[0]).astype(BF16)
    kv = jnp.dot(h, w_ref[0], preferred_element_type=F32)
    for hd in range(X_HEADS):
        sl = slice(hd * X_HEAD_DIM, (hd + 1) * X_HEAD_DIM)
        mk_ref[0, :, :, sl] = _rms_rows(kv[:, sl], gk_ref[0]).astype(BF16).reshape(nb, n_mem, X_HEAD_DIM)
    for bi in range(nb):
        mv_ref[0, bi] = kv[bi * n_mem:(bi + 1) * n_mem, X_WIDTH:].T.astype(BF16)


def _mem_kv(mem, mem_norm_g, w_mem_kv, xk_norm_g):
    b, n_mem, d = mem.shape
    depth = w_mem_kv.shape[0]
    out = jax.ShapeDtypeStruct((depth, b, n_mem, X_WIDTH), BF16)
    ospec = pl.BlockSpec((1, b, n_mem, X_WIDTH), lambda l: (l, 0, 0, 0))
    out_t = jax.ShapeDtypeStruct((depth, b, X_WIDTH, n_mem), BF16)
    ospec_t = pl.BlockSpec((1, b, X_WIDTH, n_mem), lambda l: (l, 0, 0, 0))
    return pl.pallas_call(
        _memkv_kernel,
        out_shape=(out, out_t),
        grid=(depth,),
        in_specs=[pl.BlockSpec((b, n_mem, d), lambda l: (0, 0, 0)),
                  pl.BlockSpec((1, 1, d), lambda l: (l, 0, 0)),
                  pl.BlockSpec((1, d, 2 * X_WIDTH), lambda l: (l, 0, 0)),
                  pl.BlockSpec((1, 1, X_HEAD_DIM), lambda l: (l, 0, 0))],
        out_specs=(ospec, ospec_t),
        compiler_params=pltpu.CompilerParams(vmem_limit_bytes=VMEM_LIMIT),
        name="mem_kv",
    )(mem, mem_norm_g.reshape(depth, 1, d), w_mem_kv.astype(BF16), xk_norm_g.reshape(depth, 1, X_HEAD_DIM))


def _s5_prep_kernel(lr_ref, li_ref, ldt_ref, brt_ref, bit_ref, crt_ref, cit_ref,
                    wtws_ref, wcp_ref, pwr_ref, pwi_ref, *, n_scan):
    n_groups = lr_ref.shape[1]
    tab_row = lax.broadcasted_iota(jnp.int32, (3 * CHUNK, LANES), 0)
    n_tab = jnp.where(tab_row < 2 * CHUNK, tab_row, 3 * CHUNK - 1 - tab_row).astype(F32)
    scan_row = lax.broadcasted_iota(jnp.int32, (2 * SCAN_BLOCK, LANES), 0)
    n_scan_rows = jnp.where(scan_row < SCAN_BLOCK,
                            jnp.left_shift(jnp.int32(CHUNK), jnp.minimum(scan_row, n_scan - 1)),
                            CHUNK * (scan_row - SCAN_BLOCK + 1)).astype(F32)
    low = lax.broadcasted_iota(jnp.int32, (CK, LANES), 1) < SSM_STATE
    lane_s = lax.broadcasted_iota(jnp.int32, (CK, LANES), 1) // SSM_CH
    hi = lax.Precision.HIGHEST

    def per_chunk_row(tab, first):
        rows = tab[first:first + CHUNK]
        return jnp.broadcast_to(rows[:, None, :], (CHUNK, SSM_CH, LANES)).reshape(CK, LANES)

    low_rows = lax.broadcasted_iota(jnp.int32, pwr_ref.shape[2:], 1) < SSM_STATE

    def group(g, gp, e):
        own = low if e == 0 else jnp.logical_not(low)
        dt = jnp.exp(ldt_ref[0, g])
        lr, li = lr_ref[0, g], li_ref[0, g]

        def power(n):
            mag = jnp.exp(lr * dt * n)
            return mag * jnp.cos(li * dt * n), mag * jnp.sin(li * dt * n)

        tab_r, tab_i = power(n_tab)
        ar, ai = tab_r[1:2], tab_i[1:2]
        den = lr * lr + li * li
        fr = ((ar - 1.0) * lr + ai * li) / den
        fi = (ai * lr - (ar - 1.0) * li) / den
        every_chunk_row = lambda a: jnp.concatenate([a] * CHUNK, axis=0)
        brt, bit = every_chunk_row(brt_ref[0, g]), every_chunk_row(bit_ref[0, g])
        bbr = fr * brt - fi * bit
        bbi = fr * bit + fi * brt

        crt, cit = every_chunk_row(crt_ref[0, g]), every_chunk_row(cit_ref[0, g])

        def c_times_power(first):
            pr, pi = per_chunk_row(tab_r, first), per_chunk_row(tab_i, first)
            return crt * pr - cit * pi, -(crt * pi + cit * pr)

        g_re, g_mim = c_times_power(0)
        d = lax.dot_general(jnp.where(low, g_re, g_mim), jnp.where(low, bbr, bbi), (((1,), (1,)), ((), ())),
                            precision=hi, preferred_element_type=F32)
        per_tile = LANES // SSM_CH
        for tile in range(CK // LANES):
            d_t = d[:, tile * LANES:(tile + 1) * LANES]
            wt = jnp.zeros((CK, LANES), F32)
            for s in range(tile * per_tile, (tile + 1) * per_tile):
                shifted = d_t if s == 0 else jnp.concatenate(
                    [jnp.zeros((s * SSM_CH, LANES), F32), d_t[:CK - s * SSM_CH]], axis=0)
                wt = jnp.where(lane_s == s - tile * per_tile, shifted, wt)
            wtws_ref[0, g, 0:CK, tile * LANES:(tile + 1) * LANES] = wt.astype(BF16)

        qr, qi = per_chunk_row(tab_r, 2 * CHUNK), per_chunk_row(tab_i, 2 * CHUNK)
        ws = jnp.where(low, qr * bbr - qi * bbi, qr * bbi + qi * bbr)
        wtws_ref[0, g, CK:, :] = ws.T.astype(BF16)

        c_re, c_mim = c_times_power(1)
        wcp_ref[0, gp, 0, e * CK:(e + 1) * CK, :] = jnp.where(own, c_re, 0.0).astype(BF16)
        wcp_ref[0, gp, 1, e * CK:(e + 1) * CK, :] = jnp.where(own, c_mim, 0.0).astype(BF16)

        return power(n_scan_rows)

    def pair(gp, carry):
        (r0, i0), (r1, i1) = group(2 * gp, gp, 0), group(2 * gp + 1, gp, 1)
        pwr_ref[0, gp] = jnp.where(low_rows, r0, r1)
        pwi_ref[0, gp] = jnp.where(low_rows, i0, i1)
        return carry

    lax.fori_loop(0, n_groups // 2, pair, 0)


def _s5_prep(lam_re, lam_im, log_dt, b_re, b_im, c_re, c_im, n_scan):
    depth, g, p = lam_re.shape
    twice = lambda a: jnp.concatenate([a, a], axis=-1)
    row = lambda a: twice(a).reshape(depth, g, 1, 2 * p)
    b_t = lambda a: twice(jnp.swapaxes(a, 2, 3))
    c_t = twice
    assert n_scan <= SCAN_BLOCK
    n_rows = 2 * SCAN_BLOCK
    gb = 8
    spec = lambda *shape: pl.BlockSpec((1, gb) + shape, lambda l, i: (l, i, 0, 0))
    pair_spec = lambda *shape: pl.BlockSpec((1, gb // 2) + shape, lambda l, i: (l, i) + (0,) * len(shape))
    return pl.pallas_call(
        functools.partial(_s5_prep_kernel, n_scan=n_scan),
        out_shape=(jax.ShapeDtypeStruct((depth, g, CK + 2 * p, CK), BF16),
                   jax.ShapeDtypeStruct((depth, g // 2, 2, 2 * CK, 2 * p), BF16),
                   jax.ShapeDtypeStruct((depth, g // 2, n_rows, 2 * p), F32),
                   jax.ShapeDtypeStruct((depth, g // 2, n_rows, 2 * p), F32)),
        grid=(depth, g // gb),
        in_specs=[spec(1, 2 * p), spec(1, 2 * p), spec(1, 1),
                  spec(SSM_CH, 2 * p), spec(SSM_CH, 2 * p), spec(SSM_CH, 2 * p), spec(SSM_CH, 2 * p)],
        out_specs=(spec(CK + 2 * p, CK), pair_spec(2, 2 * CK, 2 * p), pair_spec(n_rows, 2 * p),
                   pair_spec(n_rows, 2 * p)),
        compiler_params=pltpu.CompilerParams(vmem_limit_bytes=VMEM_LIMIT),
        name="s5_prep",
    )(row(lam_re), row(lam_im), log_dt.reshape(depth, g, 1, 1),
      b_t(b_re), b_t(b_im), c_t(c_re), c_t(c_im))


def _col_rms(v):
    return lax.rsqrt(jnp.mean(v * v, axis=0, keepdims=True) + EPS)


def _ac_kernel(sinks_ref, x_ref, ob_ref, cos_ref, sin_ref, ng_ref, win_ref, gq_ref, gk_ref, mk_ref, mvt_ref,
               gxq_ref, wout_ref, woutb_ref, out_ref, k_ref, v_ref, shuf_ref, sc_ref, pt_ref, bias_ref, *sub_refs,
               layer, natural_out):
    i = pl.program_id(1)
    d_model = x_ref.shape[-1]
    half = HEAD_DIM // 2
    n_mem = mk_ref.shape[2]

    @pl.when(i == 0)
    def _():
        k_ref[0:BLOCK] = jnp.zeros((BLOCK, LANES), BF16)
        v_ref[:, 0:BLOCK] = jnp.zeros((KV_WIDTH, BLOCK), BF16)

    key_row = lax.broadcasted_iota(jnp.int32, (2 * BLOCK, BLOCK), 0)
    q_col = lax.broadcasted_iota(jnp.int32, (2 * BLOCK, BLOCK), 1)
    local_tok = lambda rho: (rho % ROWS) * CHUNK + rho // ROWS
    qi = local_tok(q_col)
    kj = (key_row // BLOCK) * BLOCK + local_tok(key_row % BLOCK)
    band = (kj >= qi + 1) & (kj <= qi + BLOCK)
    band_first = band & (kj >= jnp.where(i == 0, BLOCK, 0))
    bias_ref[0] = jnp.where(band_first, 0.0, NEG_INF)
    bias_ref[1] = jnp.where(band, 0.0, NEG_INF)
    ones_rows = jnp.ones((2 * ROWS, 2 * BLOCK), BF16)
    ones_mem = jnp.ones((2 * ROWS, n_mem), BF16)
    zeros_q = jnp.zeros((HEAD_DIM, GQA_GROUP * BLOCK), BF16)

    def tokens(blocks):
        return jnp.concatenate([x_ref[0, :, r * ROWS:(r + 1) * ROWS, :].reshape(BLOCK, d_model) for r in blocks],
                               axis=0)

    n_sub = TQ // SUB
    z_refs, q_refs, mix_refs, h_refs, ob_refs = (sub_refs[k * n_sub:(k + 1) * n_sub] for k in range(5))

    def part(refs, rows, lanes):
        sub = lanes.start // SUB
        return refs[sub].at[rows, lanes.start - sub * SUB:lanes.stop - sub * SUB]

    lanes_of = lambda sub: slice(sub * SUB, (sub + 1) * SUB)
    blocks_of = lambda sub: range(sub * SUB // BLOCK, (sub + 1) * SUB // BLOCK)
    proj_rows = (slice(0, _OFF_AG), slice(_OFF_AG, _OFF_XQ), slice(_OFF_XQ, _OFF_XG), slice(_OFF_XG, AC_WIDTH))
    out_cols = tuple(slice(c * d_model // OUT_CHUNKS, (c + 1) * d_model // OUT_CHUNKS) for c in range(OUT_CHUNKS))

    def hidden(sub):
        h_refs[sub][...] = _rms_rows(tokens(blocks_of(sub)), ng_ref[0]).astype(BF16)
        ob_refs[sub][...] = jnp.concatenate(
            [ob_ref[0, :, r * ROWS:(r + 1) * ROWS, :].reshape(BLOCK, SSM_WIDTH) for r in blocks_of(sub)],
            axis=0).astype(BF16)

    def project(sub, c):
        z_refs[sub][proj_rows[c], :] = _dot_nt(win_ref[0, proj_rows[c], :], h_refs[sub][...])

    def qkv(sub):
        lanes = lanes_of(sub)
        cos_t, sin_t = cos_ref[0, :, lanes], sin_ref[0, :, lanes]

        def head_norm_rope(v, gain):
            vn = v * _col_rms(v) * gain
            return vn * cos_t + jnp.concatenate([vn[half:], vn[:half]], axis=0) * sin_t

        kt = jnp.concatenate(
            [head_norm_rope(z_refs[sub][_OFF_K + hk * HEAD_DIM:_OFF_K + (hk + 1) * HEAD_DIM, :], gk_ref[0])
             for hk in range(N_KV_HEADS)], axis=0)
        k_ref[BLOCK + sub * SUB:BLOCK + (sub + 1) * SUB] = kt.T.astype(BF16)
        v_ref[:, BLOCK + sub * SUB:BLOCK + (sub + 1) * SUB] = z_refs[sub][_OFF_V:_OFF_V + KV_WIDTH, :].astype(BF16)
        for hq in range(N_Q_HEADS):
            rows = slice(hq * HEAD_DIM, (hq + 1) * HEAD_DIM)
            q_refs[sub][rows, :] = head_norm_rope(z_refs[sub][rows, :], gq_ref[0]).astype(BF16)

    def column_softmax(slot, bias_idx, floor):
        n_rows, width = sc_ref.shape[1:]

        def chunk(c):
            rows = slice(c * SM_ROWS, (c + 1) * SM_ROWS)
            s = sc_ref[slot, rows, :]
            if bias_idx is not None:
                s = s + jnp.concatenate([bias_ref[bias_idx, rows, :]] * (width // BLOCK), axis=1)
            return rows, s

        m = None
        for c in range(n_rows // SM_ROWS):
            _, s = chunk(c)
            cm = jnp.max(s.reshape(SM_ROWS // 8, 8, width), axis=0)
            m = cm if m is None else jnp.maximum(m, cm)
        m = jnp.max(m, axis=0, keepdims=True)
        if floor is not None:
            m = jnp.maximum(m, floor)
        for c in range(n_rows // SM_ROWS):
            rows, s = chunk(c)
            pt_ref[slot, rows, :] = jnp.exp2(s - m).astype(BF16)
        return m

    def attend(r, hk, slot):
        cols = slice(r * BLOCK, (r + 1) * BLOCK)
        keys = k_ref[r * BLOCK:(r + 2) * BLOCK]
        heads = range(hk * GQA_GROUP, (hk + 1) * GQA_GROUP)
        qt = jnp.concatenate([part(q_refs, slice(hq * HEAD_DIM, (hq + 1) * HEAD_DIM), cols)[...] for hq in heads],
                             axis=1)
        qt = jnp.concatenate([qt, zeros_q] if hk == 0 else [zeros_q, qt], axis=0)
        sc_ref[slot] = jnp.dot(keys, qt, preferred_element_type=F32)
        sink = jnp.concatenate([jnp.full((1, BLOCK), sinks_ref[layer, hq] * LOG2E, F32) for hq in heads], axis=1)
        m = column_softmax(slot, 0 if r == 0 else 1, sink)
        return jnp.exp2(sink - m)

    def attend_finish(r, hk, slot, esink):
        cols = slice(r * BLOCK, (r + 1) * BLOCK)
        vals = jnp.concatenate(
            [v_ref[hk * HEAD_DIM:(hk + 1) * HEAD_DIM, r * BLOCK:(r + 2) * BLOCK], ones_rows], axis=0)
        o = jnp.dot(vals, pt_ref[slot], preferred_element_type=F32)
        o = o[:HEAD_DIM] * (1.0 / (o[HEAD_DIM:HEAD_DIM + 1] + esink))
        for g, hq in enumerate(range(hk * GQA_GROUP, (hk + 1) * GQA_GROUP)):
            rows = slice(hq * HEAD_DIM, (hq + 1) * HEAD_DIM)
            gate = part(z_refs, slice(_OFF_AG + hq * HEAD_DIM, _OFF_AG + (hq + 1) * HEAD_DIM), cols)[...]
            part(mix_refs, rows, cols)[...] = (o[:, g * BLOCK:(g + 1) * BLOCK] * _silu(gate)).astype(BF16)

    def cross(sub, hd, slot):
        rows = slice(hd * X_HEAD_DIM, (hd + 1) * X_HEAD_DIM)
        xq = z_refs[sub][_OFF_XQ + hd * X_HEAD_DIM:_OFF_XQ + (hd + 1) * X_HEAD_DIM, :]
        xq = (xq * _col_rms(xq) * gxq_ref[0]).astype(BF16)
        sc_ref[slot] = jnp.dot(mk_ref[0, 0, :, rows], xq, preferred_element_type=F32)
        column_softmax(slot, None, None)

    def cross_finish(sub, hd, slot, _):
        rows = slice(hd * X_HEAD_DIM, (hd + 1) * X_HEAD_DIM)
        vals = jnp.concatenate([mvt_ref[0, 0, rows, :], ones_mem], axis=0)
        o = jnp.dot(vals, pt_ref[slot], preferred_element_type=F32)
        o = o[:X_HEAD_DIM] * (1.0 / o[X_HEAD_DIM:X_HEAD_DIM + 1])
        gate = z_refs[sub][_OFF_XG + hd * X_HEAD_DIM:_OFF_XG + (hd + 1) * X_HEAD_DIM, :]
        mix_refs[sub][ATTN_WIDTH + hd * X_HEAD_DIM:ATTN_WIDTH + (hd + 1) * X_HEAD_DIM, :] = (
            o * _silu(gate)).astype(BF16)

    def output(sub, c):
        cols = out_cols[c]
        width = d_model // OUT_CHUNKS
        y = lax.dot_general(mix_refs[sub][...], wout_ref[0, :, cols], (((0,), (0,)), ((), ())),
                            preferred_element_type=F32)
        y = y + jnp.dot(ob_refs[sub][...], woutb_ref[0, :, cols], preferred_element_type=F32)
        for n, r in enumerate(blocks_of(sub)):
            xr = x_ref[0, :, r * ROWS:(r + 1) * ROWS, cols]
            res = xr + y[n * BLOCK:(n + 1) * BLOCK].reshape(CHUNK, ROWS, width)
            if not natural_out:
                out_ref[0, :, r * ROWS:(r + 1) * ROWS, cols] = res
                continue
            res = res.reshape(BLOCK, width)
            for slab in range(width // LANES):
                shuf_ref[c, n, slab] = res[:, slab * LANES:(slab + 1) * LANES]
                lanes = slice(cols.start + slab * LANES, cols.start + (slab + 1) * LANES)
                for ch in range(ROWS):
                    out_ref[0, r * BLOCK + ch * CHUNK:r * BLOCK + (ch + 1) * CHUNK, lanes] = (
                        shuf_ref[c, n, slab, pl.ds(ch, CHUNK, stride=ROWS), :])

    hidden(0)
    project(0, 0)
    qkv(0)
    for c in range(1, len(proj_rows)):
        project(0, c)
    for sub in range(n_sub):
        units = [(attend, attend_finish, (r, hk)) for r in blocks_of(sub) for hk in range(N_KV_HEADS)]
        units += [(cross, cross_finish, (sub, hd)) for hd in range(X_HEADS)]
        carried = {}

        def stage(n):
            if n < len(units):
                carried[n] = units[n][0](*units[n][2], n % 2)
            if n >= 1:
                units[n - 1][1](*units[n - 1][2], (n - 1) % 2, carried.pop(n - 1))

        vec = [functools.partial(stage, n) for n in range(len(units) + 1)]
        mxu = []
        if sub + 1 < n_sub:
            hidden(sub + 1)
            mxu += [functools.partial(project, sub + 1, c) for c in range(len(proj_rows))]
        if sub >= 1:
            mxu += [functools.partial(output, sub - 1, c) for c in range(OUT_CHUNKS)]
        every = -(-len(vec) // max(len(mxu), 1))
        for n, task in enumerate(vec):
            if n % every == 0 and mxu:
                mxu.pop(0)()
            task()
        for task in mxu:
            task()
        if sub + 1 < n_sub:
            qkv(sub + 1)
    for c in range(OUT_CHUNKS):
        output(n_sub - 1, c)

    k_ref[0:BLOCK] = k_ref[TQ:TQ + BLOCK]
    v_ref[:, 0:BLOCK] = v_ref[:, TQ:TQ + BLOCK]


def _ac_layer(layer, x, ob, cos_t, sin_t, sinks, norm_g, w_in_act, gq, gk, mk, mvt, gxq, w_out_ac, w_out_b,
              natural_out):
    b, _, n_chunk, d = x.shape
    n_mem = mk.shape[2]
    n_sub = TQ // SUB
    assert n_mem == 2 * BLOCK and SUB == GQA_GROUP * BLOCK
    if natural_out:
        out_shape = jax.ShapeDtypeStruct((b, n_chunk * CHUNK, d), F32)
        out_spec = pl.BlockSpec((1, TQ, d), lambda bi, i: (bi, i, 0))
    else:
        out_shape = jax.ShapeDtypeStruct(x.shape, F32)
        out_spec = pl.BlockSpec((1, CHUNK, TQ // CHUNK, d), lambda bi, i: (bi, 0, i, 0))
    tab = pl.BlockSpec((1, HEAD_DIM, TQ), lambda bi, i: (bi, 0, i))
    xspec = pl.BlockSpec((1, CHUNK, TQ // CHUNK, d), lambda bi, i: (bi, 0, i, 0))
    obspec = pl.BlockSpec((1, CHUNK, TQ // CHUNK, SSM_WIDTH), lambda bi, i: (bi, 0, i, 0))
    per_layer = lambda *shape: pl.BlockSpec((1,) + shape, lambda bi, i: (layer,) + (0,) * len(shape),
                                            pipeline_mode=pl.Buffered(1))
    return pl.pallas_call(
        functools.partial(_ac_kernel, layer=layer, natural_out=natural_out),
        out_shape=out_shape,
        grid=(b, n_chunk * CHUNK // TQ),
        in_specs=[pl.BlockSpec(memory_space=pltpu.SMEM),
                  xspec, obspec, tab, tab,
                  per_layer(1, d), per_layer(AC_WIDTH, d), per_layer(HEAD_DIM, SUB), per_layer(HEAD_DIM, SUB),
                  pl.BlockSpec((1, 1, n_mem, X_WIDTH), lambda bi, i: (layer, bi, 0, 0)),
                  pl.BlockSpec((1, 1, X_WIDTH, n_mem), lambda bi, i: (layer, bi, 0, 0)),
                  per_layer(X_HEAD_DIM, SUB), per_layer(ATTN_WIDTH + X_WIDTH, d), per_layer(SSM_WIDTH, d)],
        out_specs=out_spec,
        scratch_shapes=([pltpu.VMEM((BLOCK + TQ, LANES), BF16),
                         pltpu.VMEM((KV_WIDTH, BLOCK + TQ), BF16),
                         pltpu.VMEM((OUT_CHUNKS, SUB // BLOCK, d // OUT_CHUNKS // LANES, BLOCK, LANES), F32),
                         pltpu.VMEM((2, 2 * BLOCK, SUB), F32),
                         pltpu.VMEM((2, 2 * BLOCK, SUB), BF16),
                         pltpu.VMEM((2, 2 * BLOCK, BLOCK), F32)]
                        + [pltpu.VMEM((AC_WIDTH, SUB), F32)] * n_sub
                        + [pltpu.VMEM((ATTN_WIDTH, SUB), BF16)] * n_sub
                        + [pltpu.VMEM((ATTN_WIDTH + X_WIDTH, SUB), BF16)] * n_sub
                        + [pltpu.VMEM((SUB, d), BF16)] * n_sub
                        + [pltpu.VMEM((SUB, SSM_WIDTH), BF16)] * n_sub),
        compiler_params=pltpu.CompilerParams(dimension_semantics=("arbitrary", "arbitrary"),
                                             vmem_limit_bytes=VMEM_LIMIT),
        name=f"attn_layer{layer}",
    )(sinks, x, ob, cos_t, sin_t, norm_g, w_in_act, gq, gk, mk, mvt, gxq, w_out_ac, w_out_b)


def _ssm_kernel(x_ref, ng_ref, winb_ref, wtws_ref, wcp_ref, pwr_ref, pwi_ref, dsk_ref, wglu_ref, bglu_ref,
                out_ref, z_ref, y_ref, sg_ref, hs_ref, es_ref, *, n_chunk, n_scan):
    step = pl.program_id(1)
    n_proj = CHUNK // SPS_IN
    d_model = x_ref.shape[-1]

    @pl.when(step < n_proj)
    def _project():
        for part in range(SPS_IN // SPS):
            x = x_ref[0, part * SPS:(part + 1) * SPS].reshape(SPS * n_chunk, d_model)
            h = _rms_rows(x, ng_ref[0]).astype(BF16)
            ut = _dot_nt(winb_ref[0], h)
            for e in range(SPS):
                s = SPS_IN * step + part * SPS + e
                u = ut[:SSM_WIDTH, e * n_chunk:(e + 1) * n_chunk]
                z_ref[s] = u.astype(BF16).reshape(SSM_GROUPS, SSM_CH, n_chunk)
                y_ref[s] = (u * dsk_ref[0]).reshape(SSM_GROUPS, SSM_CH, n_chunk)
                sg_ref[s] = _silu(ut[SSM_WIDTH:, e * n_chunk:(e + 1) * n_chunk]).astype(BF16)

    @pl.when(step == n_proj - 1)
    def _chunks():
        n_blk = n_chunk // SCAN_BLOCK
        lvl1 = SCAN_BLOCK.bit_length() - 1
        hs_ref[:, :, 0:SCAN_BLOCK, :] = jnp.zeros((PAIRS, 2, SCAN_BLOCK, LANES), F32)
        es_ref[:, :, 0:n_blk, :] = jnp.zeros((PAIRS, 2, n_blk, LANES), F32)
        rows = pl.ds(SCAN_BLOCK, n_chunk)
        in_block = lax.broadcasted_iota(jnp.int32, (n_chunk, LANES), 0) % SCAN_BLOCK

        def mul_add(hr, hi, ar, ai, sr, si):
            return hr + ar * sr - ai * si, hi + ar * si + ai * sr

        def pairs(it, carry):
            gps = [PAIRS * it + k for k in range(PAIRS)]
            zs = [[z_ref[:, 2 * gp + e].reshape(CK, n_chunk) for e in range(2)] for gp in gps]
            power = lambda gp, j: (pwr_ref[0, gp, j:j + 1, :], pwi_ref[0, gp, j:j + 1, :])
            hr, hi = [], []
            for k, gp in enumerate(gps):
                s = [jnp.dot(wtws_ref[0, 2 * gp + e, CK:, :], zs[k][e], preferred_element_type=F32)
                     for e in range(2)]
                hr.append(jnp.concatenate([s[0][:SSM_STATE], s[1][:SSM_STATE]], axis=0).T)
                hi.append(jnp.concatenate([s[0][SSM_STATE:], s[1][SSM_STATE:]], axis=0).T)

            def put(k):
                hs_ref[k, 0, rows, :] = hr[k]
                hs_ref[k, 1, rows, :] = hi[k]

            for j in range(lvl1):
                for k in range(PAIRS):
                    put(k)
                for k, gp in enumerate(gps):
                    keep = in_block >= (1 << j)
                    sr = jnp.where(keep, hs_ref[k, 0, pl.ds(SCAN_BLOCK - (1 << j), n_chunk), :], 0.0)
                    si = jnp.where(keep, hs_ref[k, 1, pl.ds(SCAN_BLOCK - (1 << j), n_chunk), :], 0.0)
                    hr[k], hi[k] = mul_add(hr[k], hi[k], *power(gp, j), sr, si)
            for k in range(PAIRS):
                put(k)
            last = pl.ds(2 * SCAN_BLOCK - 1, n_blk, stride=SCAN_BLOCK)
            er = [hs_ref[k, 0, last, :] for k in range(PAIRS)]
            ei = [hs_ref[k, 1, last, :] for k in range(PAIRS)]
            blk = pl.ds(n_blk, n_blk)
            for j in range(lvl1, n_scan + 1):
                shift = (1 << (j - lvl1)) if j < n_scan else 1
                for k in range(PAIRS):
                    es_ref[k, 0, blk, :] = er[k]
                    es_ref[k, 1, blk, :] = ei[k]
                for k, gp in enumerate(gps):
                    sr, si = es_ref[k, 0, pl.ds(n_blk - shift, n_blk), :], es_ref[k, 1, pl.ds(n_blk - shift, n_blk), :]
                    if j < n_scan:
                        er[k], ei[k] = mul_add(er[k], ei[k], *power(gp, j), sr, si)
                    else:
                        spread = lambda v: jnp.broadcast_to(v[:, None, :], (n_blk, SCAN_BLOCK, LANES)).reshape(
                            n_chunk, LANES)
                        pr = jnp.concatenate([pwr_ref[0, gp, SCAN_BLOCK:2 * SCAN_BLOCK, :]] * n_blk, axis=0)
                        pi = jnp.concatenate([pwi_ref[0, gp, SCAN_BLOCK:2 * SCAN_BLOCK, :]] * n_blk, axis=0)
                        hr[k], hi[k] = mul_add(hr[k], hi[k], pr, pi, spread(sr), spread(si))
            for k in range(PAIRS):
                put(k)
            for k, gp in enumerate(gps):
                sr = hs_ref[k, 0, pl.ds(SCAN_BLOCK - 1, n_chunk), :]
                si = hs_ref[k, 1, pl.ds(SCAN_BLOCK - 1, n_chunk), :]
                yc = (_dot_nt(wcp_ref[0, gp, 0], sr.astype(BF16))
                      + _dot_nt(wcp_ref[0, gp, 1], si.astype(BF16)))
                for e in range(2):
                    y = jnp.dot(wtws_ref[0, 2 * gp + e, 0:CK, :], zs[k][e], preferred_element_type=F32)
                    y = y + yc[e * CK:(e + 1) * CK]
                    y_ref[:, 2 * gp + e] = y_ref[:, 2 * gp + e] + y.reshape(CHUNK, SSM_CH, n_chunk)
            return carry

        lax.fori_loop(0, SSM_GROUPS // (2 * PAIRS), pairs, 0)

    @pl.when(step >= n_proj)
    def _finish():
        t0 = SPS * (step - n_proj)
        y = jnp.concatenate([y_ref[t0 + e].reshape(SSM_WIDTH, n_chunk) for e in range(SPS)], axis=1)
        sg = jnp.concatenate([sg_ref[t0 + e] for e in range(SPS)], axis=1).astype(F32)
        y = jax.nn.gelu(y)
        gate = _sigmoid(jnp.dot(wglu_ref[0], y.astype(BF16), preferred_element_type=F32) + bglu_ref[0])
        out_ref[0] = (y * gate * sg).T.reshape(SPS, n_chunk, SSM_WIDTH)


def _ssm_layer(layer, x, norm_g, w_in_bt, wtws, wcp, pwr, pwi, d_skip, w_glu_t, b_glu, n_scan):
    b, _, n_chunk, d = x.shape
    n_proj, n_fin = CHUNK // SPS_IN, CHUNK // SPS
    per_layer = lambda *shape: pl.BlockSpec((1,) + shape, lambda bi, st: (layer,) + (0,) * len(shape),
                                            pipeline_mode=pl.Buffered(1))
    x_spec = pl.BlockSpec((1, SPS_IN, n_chunk, d), lambda bi, st: (bi, jnp.minimum(st, n_proj - 1), 0, 0))
    o_spec = pl.BlockSpec((1, SPS, n_chunk, SSM_WIDTH), lambda bi, st: (bi, jnp.maximum(st - n_proj, 0), 0, 0))
    return pl.pallas_call(
        functools.partial(_ssm_kernel, n_chunk=n_chunk, n_scan=n_scan),
        out_shape=jax.ShapeDtypeStruct(x.shape[:3] + (SSM_WIDTH,), F32),
        grid=(b, n_proj + n_fin),
        in_specs=[x_spec,
                  per_layer(1, d), per_layer(2 * SSM_WIDTH, d),
                  per_layer(SSM_GROUPS, CK + 2 * SSM_STATE, CK),
                  per_layer(SSM_GROUPS // 2, 2, 2 * CK, LANES),
                  per_layer(*pwr.shape[1:]), per_layer(*pwi.shape[1:]),
                  per_layer(SSM_WIDTH, 1), per_layer(SSM_WIDTH, SSM_WIDTH), per_layer(SSM_WIDTH, 1)],
        out_specs=o_spec,
        scratch_shapes=[pltpu.VMEM((CHUNK, SSM_GROUPS, SSM_CH, n_chunk), BF16),
                        pltpu.VMEM((CHUNK, SSM_GROUPS, SSM_CH, n_chunk), F32),
                        pltpu.VMEM((CHUNK, SSM_WIDTH, n_chunk), BF16),
                        pltpu.VMEM((PAIRS, 2, SCAN_BLOCK + n_chunk, LANES), F32),
                        pltpu.VMEM((PAIRS, 2, 2 * n_chunk // SCAN_BLOCK, LANES), F32)],
        compiler_params=pltpu.CompilerParams(dimension_semantics=("arbitrary", "arbitrary"),
                                             vmem_limit_bytes=VMEM_LIMIT),
        name=f"ssm_layer{layer}",
    )(x, norm_g, w_in_bt, wtws, wcp, pwr, pwi, d_skip, w_glu_t, b_glu)


def kernel(x, mem, positions, norm_g, w_in, q_norm_g, k_norm_g, sinks, lam_re, lam_im, log_dt, b_re, b_im,
           c_re, c_im, d_skip, w_glu, b_glu, mem_norm_g, w_mem_kv, xq_norm_g, xk_norm_g, w_out):
    b, s, d = x.shape
    depth = w_in.shape[0]
    assert s % TQ == 0 and (s // CHUNK) % LANES == 0
    n_scan = (s // CHUNK - 1).bit_length()

    o_su = _OFF_AG + ATTN_WIDTH
    o_xq = o_su + 2 * SSM_WIDTH
    w_in_act = jnp.swapaxes(jnp.concatenate([w_in[:, :, :o_su], w_in[:, :, o_xq:]], axis=-1), 1, 2).astype(BF16)
    w_in_bt = jnp.swapaxes(w_in[:, :, o_su:o_xq], 1, 2).astype(BF16)
    w_out_ac = jnp.concatenate([w_out[:, :ATTN_WIDTH], w_out[:, ATTN_WIDTH + SSM_WIDTH:]], axis=1).astype(BF16)
    w_out_b = w_out[:, ATTN_WIDTH:ATTN_WIDTH + SSM_WIDTH].astype(BF16)
    w_glu_t = jnp.swapaxes(w_glu, 1, 2).astype(BF16)
    norm_g3 = norm_g.reshape(depth, 1, d)
    over_tokens = lambda g: jnp.broadcast_to(g[:, :, None], g.shape + (SUB,))
    gq = over_tokens(q_norm_g * (LOG2E / math.sqrt(HEAD_DIM)))
    gk = over_tokens(k_norm_g)
    gxq = over_tokens(xq_norm_g * (LOG2E / math.sqrt(X_HEAD_DIM)))
    d_skip3 = d_skip.reshape(depth, SSM_WIDTH, 1)
    b_glu3 = b_glu.reshape(depth, SSM_WIDTH, 1)

    n_chunk = s // CHUNK
    pos_blocks = positions.reshape(b, s // BLOCK, ROWS, CHUNK).swapaxes(2, 3).reshape(b, s)
    cos_t, sin_t = _rope_tables(pos_blocks)
    mk, mvt = _mem_kv(mem, mem_norm_g, w_mem_kv, xk_norm_g)
    wtws, wcp, pwr, pwi = _s5_prep(lam_re, lam_im, log_dt, b_re, b_im, c_re, c_im, n_scan)

    xp = x.reshape(b, n_chunk, CHUNK, d).swapaxes(1, 2)
    for layer in range(depth):
        ob = _ssm_layer(layer, xp, norm_g3, w_in_bt, wtws, wcp, pwr, pwi, d_skip3, w_glu_t, b_glu3, n_scan)
        xp = _ac_layer(layer, xp, ob, cos_t, sin_t, sinks, norm_g3, w_in_act, gq, gk, mk, mvt, gxq, w_out_ac,
                       w_out_b, natural_out=layer == depth - 1)
    return xp
```

```python
import functools
import math

import jax
import jax.numpy as jnp
from jax import lax
from jax.experimental import pallas as pl
from jax.experimental.pallas import tpu as pltpu

F32 = jnp.float32
BF16 = jnp.bfloat16

EPS = 1e-6
ROPE_THETA = 10000.0
NEG_INF = -1e30
LOG2E = math.log2(math.e)

HEAD_DIM = 64
N_Q_HEADS = 8
N_KV_HEADS = 2
GQA_GROUP = N_Q_HEADS // N_KV_HEADS
BLOCK = 128
ATTN_WIDTH = N_Q_HEADS * HEAD_DIM
KV_WIDTH = N_KV_HEADS * HEAD_DIM
SSM_CH = 16
SSM_GROUPS = 32
SSM_STATE = 64
SSM_WIDTH = SSM_GROUPS * SSM_CH
X_HEADS = 4
X_HEAD_DIM = 128
X_WIDTH = X_HEADS * X_HEAD_DIM

LANES = 128
CHUNK = 16
CK = CHUNK * SSM_CH
ROWS = BLOCK // CHUNK
TQ = 1024
SUB = 512
SPS = 4
SPS_IN = 4
SM_ROWS = 32
OUT_CHUNKS = 4
SCAN_BLOCK = 8
PAIRS = 8
VMEM_LIMIT = 56 * 1024 * 1024

_OFF_Q = 0
_OFF_K = _OFF_Q + ATTN_WIDTH
_OFF_V = _OFF_K + KV_WIDTH
_OFF_AG = _OFF_V + KV_WIDTH
_OFF_XQ = _OFF_AG + ATTN_WIDTH
_OFF_XG = _OFF_XQ + X_WIDTH
AC_WIDTH = _OFF_XG + X_WIDTH


def _sigmoid(v):
    return 1.0 / (1.0 + jnp.exp(-v))


def _silu(v):
    return v * _sigmoid(v)


def _rms_rows(v, gain):
    return v * lax.rsqrt(jnp.mean(v * v, axis=-1, keepdims=True) + EPS) * gain


def _dot_nt(a, b):
    return lax.dot_general(a, b, (((1,), (1,)), ((), ())), preferred_element_type=F32)


def _rope_kernel(pos_ref, inv_ref, cos_ref, sin_ref):
    ang = inv_ref[...] * pos_ref[0].astype(F32)
    c, s = jnp.cos(ang), jnp.sin(ang)
    cos_ref[0] = jnp.concatenate([c, c], axis=0)
    sin_ref[0] = jnp.concatenate([-s, s], axis=0)


def _rope_tables(positions):
    b, s = positions.shape
    half = HEAD_DIM // 2
    inv = (ROPE_THETA ** (-jnp.arange(half, dtype=F32) / half)).reshape(half, 1)
    ts = min(s, 2048)
    spec = pl.BlockSpec((1, HEAD_DIM, ts), lambda i, j: (i, 0, j))
    return pl.pallas_call(
        _rope_kernel,
        out_shape=(jax.ShapeDtypeStruct((b, HEAD_DIM, s), F32),) * 2,
        grid=(b, s // ts),
        in_specs=[pl.BlockSpec((1, 1, ts), lambda i, j: (i, 0, j)),
                  pl.BlockSpec((HEAD_DIM // 2, 1), lambda i, j: (0, 0))],
        out_specs=(spec, spec),
        name="rope_tables",
    )(positions.reshape(b, 1, s), inv)


def _memkv_kernel(mem_ref, g_ref, w_ref, gk_ref, mk_ref, mv_ref):
    nb, n_mem, d = mem_ref.shape
    h = _rms_rows(mem_ref[...].reshape(nb * n_mem, d), g_ref[0]).astype(BF16)
    kv = jnp.dot(h, w_ref[0], preferred_element_type=F32)
    for hd in range(X_HEADS):
        sl = slice(hd * X_HEAD_DIM, (hd + 1) * X_HEAD_DIM)
        mk_ref[0, :, :, sl] = _rms_rows(kv[:, sl], gk_ref[0]).astype(BF16).reshape(nb, n_mem, X_HEAD_DIM)
    for bi in range(nb):
        mv_ref[0, bi] = kv[bi * n_mem:(bi + 1) * n_mem, X_WIDTH:].T.astype(BF16)


def _mem_kv(mem, mem_norm_g, w_mem_kv, xk_norm_g):
    b, n_mem, d = mem.shape
    depth = w_mem_kv.shape[0]
    out = jax.ShapeDtypeStruct((depth, b, n_mem, X_WIDTH), BF16)
    ospec = pl.BlockSpec((1, b, n_mem, X_WIDTH), lambda l: (l, 0, 0, 0))
    out_t = jax.ShapeDtypeStruct((depth, b, X_WIDTH, n_mem), BF16)
    ospec_t = pl.BlockSpec((1, b, X_WIDTH, n_mem), lambda l: (l, 0, 0, 0))
    return pl.pallas_call(
        _memkv_kernel,
        out_shape=(out, out_t),
        grid=(depth,),
        in_specs=[pl.BlockSpec((b, n_mem, d), lambda l: (0, 0, 0)),
                  pl.BlockSpec((1, 1, d), lambda l: (l, 0, 0)),
                  pl.BlockSpec((1, d, 2 * X_WIDTH), lambda l: (l, 0, 0)),
                  pl.BlockSpec((1, 1, X_HEAD_DIM), lambda l: (l, 0, 0))],
        out_specs=(ospec, ospec_t),
        compiler_params=pltpu.CompilerParams(vmem_limit_bytes=VMEM_LIMIT),
        name="mem_kv",
    )(mem, mem_norm_g.reshape(depth, 1, d), w_mem_kv.astype(BF16), xk_norm_g.reshape(depth, 1, X_HEAD_DIM))


def _s5_prep_kernel(lr_ref, li_ref, ldt_ref, brt_ref, bit_ref, crt_ref, cit_ref,
                    wtws_ref, wcp_ref, pwr_ref, pwi_ref, *, n_scan):
    n_groups = lr_ref.shape[1]
    tab_row = lax.broadcasted_iota(jnp.int32, (3 * CHUNK, LANES), 0)
    n_tab = jnp.where(tab_row < 2 * CHUNK, tab_row, 3 * CHUNK - 1 - tab_row).astype(F32)
    scan_row = lax.broadcasted_iota(jnp.int32, (2 * SCAN_BLOCK, LANES), 0)
    n_scan_rows = jnp.where(scan_row < SCAN_BLOCK,
                            jnp.left_shift(jnp.int32(CHUNK), jnp.minimum(scan_row, n_scan - 1)),
                            CHUNK * (scan_row - SCAN_BLOCK + 1)).astype(F32)
    low = lax.broadcasted_iota(jnp.int32, (CK, LANES), 1) < SSM_STATE
    lane_s = lax.broadcasted_iota(jnp.int32, (CK, LANES), 1) // SSM_CH
    hi = lax.Precision.HIGHEST

    def per_chunk_row(tab, first):
        rows = tab[first:first + CHUNK]
        return jnp.broadcast_to(rows[:, None, :], (CHUNK, SSM_CH, LANES)).reshape(CK, LANES)

    low_rows = lax.broadcasted_iota(jnp.int32, pwr_ref.shape[2:], 1) < SSM_STATE

    def group(g, gp, e):
        own = low if e == 0 else jnp.logical_not(low)
        dt = jnp.exp(ldt_ref[0, g])
        lr, li = lr_ref[0, g], li_ref[0, g]

        def power(n):
            mag = jnp.exp(lr * dt * n)
            return mag * jnp.cos(li * dt * n), mag * jnp.sin(li * dt * n)

        tab_r, tab_i = power(n_tab)
        ar, ai = tab_r[1:2], tab_i[1:2]
        den = lr * lr + li * li
        fr = ((ar - 1.0) * lr + ai * li) / den
        fi = (ai * lr - (ar - 1.0) * li) / den
        every_chunk_row = lambda a: jnp.concatenate([a] * CHUNK, axis=0)
        brt, bit = every_chunk_row(brt_ref[0, g]), every_chunk_row(bit_ref[0, g])
        bbr = fr * brt - fi * bit
        bbi = fr * bit + fi * brt

        crt, cit = every_chunk_row(crt_ref[0, g]), every_chunk_row(cit_ref[0, g])

        def c_times_power(first):
            pr, pi = per_chunk_row(tab_r, first), per_chunk_row(tab_i, first)
            return crt * pr - cit * pi, -(crt * pi + cit * pr)

        g_re, g_mim = c_times_power(0)
        d = lax.dot_general(jnp.where(low, g_re, g_mim), jnp.where(low, bbr, bbi), (((1,), (1,)), ((), ())),
                            precision=hi, preferred_element_type=F32)
        per_tile = LANES // SSM_CH
        for tile in range(CK // LANES):
            d_t = d[:, tile * LANES:(tile + 1) * LANES]
            wt = jnp.zeros((CK, LANES), F32)
            for s in range(tile * per_tile, (tile + 1) * per_tile):
                shifted = d_t if s == 0 else jnp.concatenate(
                    [jnp.zeros((s * SSM_CH, LANES), F32), d_t[:CK - s * SSM_CH]], axis=0)
                wt = jnp.where(lane_s == s - tile * per_tile, shifted, wt)
            wtws_ref[0, g, 0:CK, tile * LANES:(tile + 1) * LANES] = wt.astype(BF16)

        qr, qi = per_chunk_row(tab_r, 2 * CHUNK), per_chunk_row(tab_i, 2 * CHUNK)
        ws = jnp.where(low, qr * bbr - qi * bbi, qr * bbi + qi * bbr)
        wtws_ref[0, g, CK:, :] = ws.T.astype(BF16)

        c_re, c_mim = c_times_power(1)
        wcp_ref[0, gp, 0, e * CK:(e + 1) * CK, :] = jnp.where(own, c_re, 0.0).astype(BF16)
        wcp_ref[0, gp, 1, e * CK:(e + 1) * CK, :] = jnp.where(own, c_mim, 0.0).astype(BF16)

        return power(n_scan_rows)

    def pair(gp, carry):
        (r0, i0), (r1, i1) = group(2 * gp, gp, 0), group(2 * gp + 1, gp, 1)
        pwr_ref[0, gp] = jnp.where(low_rows, r0, r1)
        pwi_ref[0, gp] = jnp.where(low_rows, i0, i1)
        return carry

    lax.fori_loop(0, n_groups // 2, pair, 0)


def _s5_prep(lam_re, lam_im, log_dt, b_re, b_im, c_re, c_im, n_scan):
    depth, g, p = lam_re.shape
    twice = lambda a: jnp.concatenate([a, a], axis=-1)
    row = lambda a: twice(a).reshape(depth, g, 1, 2 * p)
    b_t = lambda a: twice(jnp.swapaxes(a, 2, 3))
    c_t = twice
    assert n_scan <= SCAN_BLOCK
    n_rows = 2 * SCAN_BLOCK
    gb = 8
    spec = lambda *shape: pl.BlockSpec((1, gb) + shape, lambda l, i: (l, i, 0, 0))
    pair_spec = lambda *shape: pl.BlockSpec((1, gb // 2) + shape, lambda l, i: (l, i) + (0,) * len(shape))
    return pl.pallas_call(
        functools.partial(_s5_prep_kernel, n_scan=n_scan),
        out_shape=(jax.ShapeDtypeStruct((depth, g, CK + 2 * p, CK), BF16),
                   jax.ShapeDtypeStruct((depth, g // 2, 2, 2 * CK, 2 * p), BF16),
                   jax.ShapeDtypeStruct((depth, g // 2, n_rows, 2 * p), F32),
                   jax.ShapeDtypeStruct((depth, g // 2, n_rows, 2 * p), F32)),
        grid=(depth, g // gb),
        in_specs=[spec(1, 2 * p), spec(1, 2 * p), spec(1, 1),
                  spec(SSM_CH, 2 * p), spec(SSM_CH, 2 * p), spec(SSM_CH, 2 * p), spec(SSM_CH, 2 * p)],
        out_specs=(spec(CK + 2 * p, CK), pair_spec(2, 2 * CK, 2 * p), pair_spec(n_rows, 2 * p),
                   pair_spec(n_rows, 2 * p)),
        compiler_params=pltpu.CompilerParams(vmem_limit_bytes=VMEM_LIMIT),
        name="s5_prep",
    )(row(lam_re), row(lam_im), log_dt.reshape(depth, g, 1, 1),
      b_t(b_re), b_t(b_im), c_t(c_re), c_t(c_im))


def _col_rms(v):
    return lax.rsqrt(jnp.mean(v * v, axis=0, keepdims=True) + EPS)


def _ac_kernel(sinks_ref, x_ref, ob_ref, cos_ref, sin_ref, ng_ref, win_ref, gq_ref, gk_ref, mk_ref, mvt_ref,
               gxq_ref, wout_ref, woutb_ref, out_ref, k_ref, v_ref, shuf_ref, sc_ref, pt_ref, bias_ref, *sub_refs,
               layer, natural_out):
    i = pl.program_id(1)
    d_model = x_ref.shape[-1]
    half = HEAD_DIM // 2
    n_mem = mk_ref.shape[2]

    @pl.when(i == 0)
    def _():
        k_ref[0:BLOCK] = jnp.zeros((BLOCK, LANES), BF16)
        v_ref[:, 0:BLOCK] = jnp.zeros((KV_WIDTH, BLOCK), BF16)

    key_row = lax.broadcasted_iota(jnp.int32, (2 * BLOCK, BLOCK), 0)
    q_col = lax.broadcasted_iota(jnp.int32, (2 * BLOCK, BLOCK), 1)
    local_tok = lambda rho: (rho % ROWS) * CHUNK + rho // ROWS
    qi = local_tok(q_col)
    kj = (key_row // BLOCK) * BLOCK + local_tok(key_row % BLOCK)
    band = (kj >= qi + 1) & (kj <= qi + BLOCK)
    band_first = band & (kj >= jnp.where(i == 0, BLOCK, 0))
    bias_ref[0] = jnp.where(band_first, 0.0, NEG_INF)
    bias_ref[1] = jnp.where(band, 0.0, NEG_INF)
    ones_rows = jnp.ones((2 * ROWS, 2 * BLOCK), BF16)
    ones_mem = jnp.ones((2 * ROWS, n_mem), BF16)
    zeros_q = jnp.zeros((HEAD_DIM, GQA_GROUP * BLOCK), BF16)

    def tokens(blocks):
        return jnp.concatenate([x_ref[0, :, r * ROWS:(r + 1) * ROWS, :].reshape(BLOCK, d_model) for r in blocks],
                               axis=0)

    n_sub = TQ // SUB
    z_refs, q_refs, mix_refs, h_refs, ob_refs = (sub_refs[k * n_sub:(k + 1) * n_sub] for k in range(5))

    def part(refs, rows, lanes):
        sub = lanes.start // SUB
        return refs[sub].at[rows, lanes.start - sub * SUB:lanes.stop - sub * SUB]

    lanes_of = lambda sub: slice(sub * SUB, (sub + 1) * SUB)
    blocks_of = lambda sub: range(sub * SUB // BLOCK, (sub + 1) * SUB // BLOCK)
    proj_rows = (slice(0, _OFF_AG), slice(_OFF_AG, _OFF_XQ), slice(_OFF_XQ, _OFF_XG), slice(_OFF_XG, AC_WIDTH))
    out_cols = tuple(slice(c * d_model // OUT_CHUNKS, (c + 1) * d_model // OUT_CHUNKS) for c in range(OUT_CHUNKS))

    def hidden(sub):
        h_refs[sub][...] = _rms_rows(tokens(blocks_of(sub)), ng_ref[0]).astype(BF16).T
        ob_refs[sub][...] = jnp.concatenate(
            [ob_ref[0, :, r * ROWS:(r + 1) * ROWS, :].reshape(BLOCK, SSM_WIDTH) for r in blocks_of(sub)],
            axis=0).astype(BF16)

    def project(sub, c):
        z_refs[sub][proj_rows[c], :] = jnp.dot(win_ref[0, proj_rows[c], :], h_refs[sub][...],
                                               preferred_element_type=F32)

    def qkv(sub):
        lanes = lanes_of(sub)
        cos_t, sin_t = cos_ref[0, :, lanes], sin_ref[0, :, lanes]

        def head_norm_rope(v, gain):
            vn = v * _col_rms(v) * gain
            return vn * cos_t + jnp.concatenate([vn[half:], vn[:half]], axis=0) * sin_t

        kt = jnp.concatenate(
            [head_norm_rope(z_refs[sub][_OFF_K + hk * HEAD_DIM:_OFF_K + (hk + 1) * HEAD_DIM, :], gk_ref[0])
             for hk in range(N_KV_HEADS)], axis=0)
        k_ref[BLOCK + sub * SUB:BLOCK + (sub + 1) * SUB] = kt.T.astype(BF16)
        v_ref[:, BLOCK + sub * SUB:BLOCK + (sub + 1) * SUB] = z_refs[sub][_OFF_V:_OFF_V + KV_WIDTH, :].astype(BF16)
        for hq in range(N_Q_HEADS):
            rows = slice(hq * HEAD_DIM, (hq + 1) * HEAD_DIM)
            q_refs[sub][rows, :] = head_norm_rope(z_refs[sub][rows, :], gq_ref[0]).astype(BF16)

    def column_softmax(slot, bias_idx, floor):
        n_rows, width = sc_ref.shape[1:]

        def chunk(c):
            rows = slice(c * SM_ROWS, (c + 1) * SM_ROWS)
            s = sc_ref[slot, rows, :]
            if bias_idx is not None:
                s = s + jnp.concatenate([bias_ref[bias_idx, rows, :]] * (width // BLOCK), axis=1)
            return rows, s

        m = None
        for c in range(n_rows // SM_ROWS):
            _, s = chunk(c)
            cm = jnp.max(s.reshape(SM_ROWS // 8, 8, width), axis=0)
            m = cm if m is None else jnp.maximum(m, cm)
        m = jnp.max(m, axis=0, keepdims=True)
        if floor is not None:
            m = jnp.maximum(m, floor)
        for c in range(n_rows // SM_ROWS):
            rows, s = chunk(c)
            pt_ref[slot, rows, :] = jnp.exp2(s - m).astype(BF16)
        return m

    def attend(r, hk, slot):
        cols = slice(r * BLOCK, (r + 1) * BLOCK)
        keys = k_ref[r * BLOCK:(r + 2) * BLOCK]
        heads = range(hk * GQA_GROUP, (hk + 1) * GQA_GROUP)
        qt = jnp.concatenate([part(q_refs, slice(hq * HEAD_DIM, (hq + 1) * HEAD_DIM), cols)[...] for hq in heads],
                             axis=1)
        qt = jnp.concatenate([qt, zeros_q] if hk == 0 else [zeros_q, qt], axis=0)
        sc_ref[slot] = jnp.dot(keys, qt, preferred_element_type=F32)
        sink = jnp.concatenate([jnp.full((1, BLOCK), sinks_ref[layer, hq] * LOG2E, F32) for hq in heads], axis=1)
        m = column_softmax(slot, 0 if r == 0 else 1, sink)
        return jnp.exp2(sink - m)

    def attend_finish(r, hk, slot, esink):
        cols = slice(r * BLOCK, (r + 1) * BLOCK)
        vals = jnp.concatenate(
            [v_ref[hk * HEAD_DIM:(hk + 1) * HEAD_DIM, r * BLOCK:(r + 2) * BLOCK], ones_rows], axis=0)
        o = jnp.dot(vals, pt_ref[slot], preferred_element_type=F32)
        o = o[:HEAD_DIM] * (1.0 / (o[HEAD_DIM:HEAD_DIM + 1] + esink))
        for g, hq in enumerate(range(hk * GQA_GROUP, (hk + 1) * GQA_GROUP)):
            rows = slice(hq * HEAD_DIM, (hq + 1) * HEAD_DIM)
            gate = part(z_refs, slice(_OFF_AG + hq * HEAD_DIM, _OFF_AG + (hq + 1) * HEAD_DIM), cols)[...]
            part(mix_refs, rows, cols)[...] = (o[:, g * BLOCK:(g + 1) * BLOCK] * _silu(gate)).astype(BF16)

    def cross(sub, hd, slot):
        rows = slice(hd * X_HEAD_DIM, (hd + 1) * X_HEAD_DIM)
        xq = z_refs[sub][_OFF_XQ + hd * X_HEAD_DIM:_OFF_XQ + (hd + 1) * X_HEAD_DIM, :]
        xq = (xq * _col_rms(xq) * gxq_ref[0]).astype(BF16)
        sc_ref[slot] = jnp.dot(mk_ref[0, 0, :, rows], xq, preferred_element_type=F32)
        column_softmax(slot, None, None)

    def cross_finish(sub, hd, slot, _):
        rows = slice(hd * X_HEAD_DIM, (hd + 1) * X_HEAD_DIM)
        vals = jnp.concatenate([mvt_ref[0, 0, rows, :], ones_mem], axis=0)
        o = jnp.dot(vals, pt_ref[slot], preferred_element_type=F32)
        o = o[:X_HEAD_DIM] * (1.0 / o[X_HEAD_DIM:X_HEAD_DIM + 1])
        gate = z_refs[sub][_OFF_XG + hd * X_HEAD_DIM:_OFF_XG + (hd + 1) * X_HEAD_DIM, :]
        mix_refs[sub][ATTN_WIDTH + hd * X_HEAD_DIM:ATTN_WIDTH + (hd + 1) * X_HEAD_DIM, :] = (
            o * _silu(gate)).astype(BF16)

    def output(sub, c):
        cols = out_cols[c]
        width = d_model // OUT_CHUNKS
        y = lax.dot_general(mix_refs[sub][...], wout_ref[0, :, cols], (((0,), (0,)), ((), ())),
                            preferred_element_type=F32)
        y = y + jnp.dot(ob_refs[sub][...], woutb_ref[0, :, cols], preferred_element_type=F32)
        for n, r in enumerate(blocks_of(sub)):
            xr = x_ref[0, :, r * ROWS:(r + 1) * ROWS, cols]
            res = xr + y[n * BLOCK:(n + 1) * BLOCK].reshape(CHUNK, ROWS, width)
            if not natural_out:
                out_ref[0, :, r * ROWS:(r + 1) * ROWS, cols] = res
                continue
            res = res.reshape(BLOCK, width)
            for slab in range(width // LANES):
                shuf_ref[c, n, slab] = res[:, slab * LANES:(slab + 1) * LANES]
                lanes = slice(cols.start + slab * LANES, cols.start + (slab + 1) * LANES)
                for ch in range(ROWS):
                    out_ref[0, r * BLOCK + ch * CHUNK:r * BLOCK + (ch + 1) * CHUNK, lanes] = (
                        shuf_ref[c, n, slab, pl.ds(ch, CHUNK, stride=ROWS), :])

    hidden(0)
    project(0, 0)
    qkv(0)
    for c in range(1, len(proj_rows)):
        project(0, c)
    for sub in range(n_sub):
        units = [(attend, attend_finish, (r, hk)) for r in blocks_of(sub) for hk in range(N_KV_HEADS)]
        units += [(cross, cross_finish, (sub, hd)) for hd in range(X_HEADS)]
        carried = {}

        def stage(n):
            if n < len(units):
                carried[n] = units[n][0](*units[n][2], n % 2)
            if n >= 1:
                units[n - 1][1](*units[n - 1][2], (n - 1) % 2, carried.pop(n - 1))

        vec = [functools.partial(stage, n) for n in range(len(units) + 1)]
        mxu = []
        if sub + 1 < n_sub:
            hidden(sub + 1)
            mxu += [functools.partial(project, sub + 1, c) for c in range(len(proj_rows))]
        if sub >= 1:
            mxu += [functools.partial(output, sub - 1, c) for c in range(OUT_CHUNKS)]
        every = -(-len(vec) // max(len(mxu), 1))
        for n, task in enumerate(vec):
            if n % every == 0 and mxu:
                mxu.pop(0)()
            task()
        for task in mxu:
            task()
        if sub + 1 < n_sub:
            qkv(sub + 1)
    for c in range(OUT_CHUNKS):
        output(n_sub - 1, c)

    k_ref[0:BLOCK] = k_ref[TQ:TQ + BLOCK]
    v_ref[:, 0:BLOCK] = v_ref[:, TQ:TQ + BLOCK]


def _ac_layer(layer, x, ob, cos_t, sin_t, sinks, norm_g, w_in_act, gq, gk, mk, mvt, gxq, w_out_ac, w_out_b,
              natural_out):
    b, _, n_chunk, d = x.shape
    n_mem = mk.shape[2]
    n_sub = TQ // SUB
    assert n_mem == 2 * BLOCK and SUB == GQA_GROUP * BLOCK
    if natural_out:
        out_shape = jax.ShapeDtypeStruct((b, n_chunk * CHUNK, d), F32)
        out_spec = pl.BlockSpec((1, TQ, d), lambda bi, i: (bi, i, 0))
    else:
        out_shape = jax.ShapeDtypeStruct(x.shape, F32)
        out_spec = pl.BlockSpec((1, CHUNK, TQ // CHUNK, d), lambda bi, i: (bi, 0, i, 0))
    tab = pl.BlockSpec((1, HEAD_DIM, TQ), lambda bi, i: (bi, 0, i))
    xspec = pl.BlockSpec((1, CHUNK, TQ // CHUNK, d), lambda bi, i: (bi, 0, i, 0))
    obspec = pl.BlockSpec((1, CHUNK, TQ // CHUNK, SSM_WIDTH), lambda bi, i: (bi, 0, i, 0))
    per_layer = lambda *shape: pl.BlockSpec((1,) + shape, lambda bi, i: (layer,) + (0,) * len(shape),
                                            pipeline_mode=pl.Buffered(1))
    return pl.pallas_call(
        functools.partial(_ac_kernel, layer=layer, natural_out=natural_out),
        out_shape=out_shape,
        grid=(b, n_chunk * CHUNK // TQ),
        in_specs=[pl.BlockSpec(memory_space=pltpu.SMEM),
                  xspec, obspec, tab, tab,
                  per_layer(1, d), per_layer(AC_WIDTH, d), per_layer(HEAD_DIM, SUB), per_layer(HEAD_DIM, SUB),
                  pl.BlockSpec((1, 1, n_mem, X_WIDTH), lambda bi, i: (layer, bi, 0, 0)),
                  pl.BlockSpec((1, 1, X_WIDTH, n_mem), lambda bi, i: (layer, bi, 0, 0)),
                  per_layer(X_HEAD_DIM, SUB), per_layer(ATTN_WIDTH + X_WIDTH, d), per_layer(SSM_WIDTH, d)],
        out_specs=out_spec,
        scratch_shapes=([pltpu.VMEM((BLOCK + TQ, LANES), BF16),
                         pltpu.VMEM((KV_WIDTH, BLOCK + TQ), BF16),
                         pltpu.VMEM((OUT_CHUNKS, SUB // BLOCK, d // OUT_CHUNKS // LANES, BLOCK, LANES), F32),
                         pltpu.VMEM((2, 2 * BLOCK, SUB), F32),
                         pltpu.VMEM((2, 2 * BLOCK, SUB), BF16),
                         pltpu.VMEM((2, 2 * BLOCK, BLOCK), F32)]
                        + [pltpu.VMEM((AC_WIDTH, SUB), F32)] * n_sub
                        + [pltpu.VMEM((ATTN_WIDTH, SUB), BF16)] * n_sub
                        + [pltpu.VMEM((ATTN_WIDTH + X_WIDTH, SUB), BF16)] * n_sub
                        + [pltpu.VMEM((d, SUB), BF16)] * n_sub
                        + [pltpu.VMEM((SUB, SSM_WIDTH), BF16)] * n_sub),
        compiler_params=pltpu.CompilerParams(dimension_semantics=("arbitrary", "arbitrary"),
                                             vmem_limit_bytes=VMEM_LIMIT),
        name=f"attn_layer{layer}",
    )(sinks, x, ob, cos_t, sin_t, norm_g, w_in_act, gq, gk, mk, mvt, gxq, w_out_ac, w_out_b)


def _ssm_kernel(x_ref, ng_ref, winb_ref, wtws_ref, wcp_ref, pwr_ref, pwi_ref, dsk_ref, wglu_ref, bglu_ref,
                out_ref, z_ref, y_ref, sg_ref, hs_ref, es_ref, ht_ref, *, n_chunk, n_scan):
    step = pl.program_id(1)
    n_proj = CHUNK // SPS_IN
    d_model = x_ref.shape[-1]

    @pl.when(step < n_proj)
    def _project():
        for part in range(SPS_IN // SPS):
            x = x_ref[0, part * SPS:(part + 1) * SPS].reshape(SPS * n_chunk, d_model)
            ht_ref[...] = _rms_rows(x, ng_ref[0]).astype(BF16).T
            ut = jnp.dot(winb_ref[0], ht_ref[...], preferred_element_type=F32)
            for e in range(SPS):
                s = SPS_IN * step + part * SPS + e
                u = ut[:SSM_WIDTH, e * n_chunk:(e + 1) * n_chunk]
                z_ref[s] = u.astype(BF16).reshape(SSM_GROUPS, SSM_CH, n_chunk)
                y_ref[s] = (u * dsk_ref[0]).reshape(SSM_GROUPS, SSM_CH, n_chunk)
                sg_ref[s] = _silu(ut[SSM_WIDTH:, e * n_chunk:(e + 1) * n_chunk])

    @pl.when(step == n_proj - 1)
    def _chunks():
        n_blk = n_chunk // SCAN_BLOCK
        lvl1 = SCAN_BLOCK.bit_length() - 1
        hs_ref[:, :, 0:SCAN_BLOCK, :] = jnp.zeros((PAIRS, 2, SCAN_BLOCK, LANES), F32)
        es_ref[:, :, 0:n_blk, :] = jnp.zeros((PAIRS, 2, n_blk, LANES), F32)
        rows = pl.ds(SCAN_BLOCK, n_chunk)
        in_block = lax.broadcasted_iota(jnp.int32, (n_chunk, LANES), 0) % SCAN_BLOCK

        def mul_add(hr, hi, ar, ai, sr, si):
            return hr + ar * sr - ai * si, hi + ar * si + ai * sr

        def pairs(it, carry):
            gps = [PAIRS * it + k for k in range(PAIRS)]
            zs = [[z_ref[:, 2 * gp + e].reshape(CK, n_chunk) for e in range(2)] for gp in gps]
            power = lambda gp, j: (pwr_ref[0, gp, j:j + 1, :], pwi_ref[0, gp, j:j + 1, :])
            hr, hi = [], []
            for k, gp in enumerate(gps):
                s = [jnp.dot(wtws_ref[0, 2 * gp + e, CK:, :], zs[k][e], preferred_element_type=F32)
                     for e in range(2)]
                hr.append(jnp.concatenate([s[0][:SSM_STATE], s[1][:SSM_STATE]], axis=0).T)
                hi.append(jnp.concatenate([s[0][SSM_STATE:], s[1][SSM_STATE:]], axis=0).T)

            def put(k):
                hs_ref[k, 0, rows, :] = hr[k]
                hs_ref[k, 1, rows, :] = hi[k]

            for j in range(lvl1):
                for k in range(PAIRS):
                    put(k)
                for k, gp in enumerate(gps):
                    keep = in_block >= (1 << j)
                    sr = jnp.where(keep, hs_ref[k, 0, pl.ds(SCAN_BLOCK - (1 << j), n_chunk), :], 0.0)
                    si = jnp.where(keep, hs_ref[k, 1, pl.ds(SCAN_BLOCK - (1 << j), n_chunk), :], 0.0)
                    hr[k], hi[k] = mul_add(hr[k], hi[k], *power(gp, j), sr, si)
            for k in range(PAIRS):
                put(k)
            last = pl.ds(2 * SCAN_BLOCK - 1, n_blk, stride=SCAN_BLOCK)
            er = [hs_ref[k, 0, last, :] for k in range(PAIRS)]
            ei = [hs_ref[k, 1, last, :] for k in range(PAIRS)]
            blk = pl.ds(n_blk, n_blk)
            for j in range(lvl1, n_scan + 1):
                shift = (1 << (j - lvl1)) if j < n_scan else 1
                for k in range(PAIRS):
                    es_ref[k, 0, blk, :] = er[k]
                    es_ref[k, 1, blk, :] = ei[k]
                for k, gp in enumerate(gps):
                    sr, si = es_ref[k, 0, pl.ds(n_blk - shift, n_blk), :], es_ref[k, 1, pl.ds(n_blk - shift, n_blk), :]
                    if j < n_scan:
                        er[k], ei[k] = mul_add(er[k], ei[k], *power(gp, j), sr, si)
                    else:
                        spread = lambda v: jnp.broadcast_to(v[:, None, :], (n_blk, SCAN_BLOCK, LANES)).reshape(
                            n_chunk, LANES)
                        pr = jnp.concatenate([pwr_ref[0, gp, SCAN_BLOCK:2 * SCAN_BLOCK, :]] * n_blk, axis=0)
                        pi = jnp.concatenate([pwi_ref[0, gp, SCAN_BLOCK:2 * SCAN_BLOCK, :]] * n_blk, axis=0)
                        hr[k], hi[k] = mul_add(hr[k], hi[k], pr, pi, spread(sr), spread(si))
            for k in range(PAIRS):
                put(k)
            for k, gp in enumerate(gps):
                sr = hs_ref[k, 0, pl.ds(SCAN_BLOCK - 1, n_chunk), :]
                si = hs_ref[k, 1, pl.ds(SCAN_BLOCK - 1, n_chunk), :]
                yc = (_dot_nt(wcp_ref[0, gp, 0], sr.astype(BF16))
                      + _dot_nt(wcp_ref[0, gp, 1], si.astype(BF16)))
                for e in range(2):
                    y = jnp.dot(wtws_ref[0, 2 * gp + e, 0:CK, :], zs[k][e], preferred_element_type=F32)
                    y = y + yc[e * CK:(e + 1) * CK]
                    y_ref[:, 2 * gp + e] = y_ref[:, 2 * gp + e] + y.reshape(CHUNK, SSM_CH, n_chunk)
            return carry

        lax.fori_loop(0, SSM_GROUPS // (2 * PAIRS), pairs, 0)

    @pl.when(step >= n_proj)
    def _finish():
        t0 = SPS * (step - n_proj)
        y = jnp.concatenate([y_ref[t0 + e].reshape(SSM_WIDTH, n_chunk) for e in range(SPS)], axis=1)
        sg = jnp.concatenate([sg_ref[t0 + e] for e in range(SPS)], axis=1)
        y = jax.nn.gelu(y)
        gate = _sigmoid(jnp.dot(wglu_ref[0], y.astype(BF16), preferred_element_type=F32) + bglu_ref[0])
        out_ref[0] = (y * gate * sg).T.reshape(SPS, n_chunk, SSM_WIDTH)


def _ssm_layer(layer, x, norm_g, w_in_bt, wtws, wcp, pwr, pwi, d_skip, w_glu_t, b_glu, n_scan):
    b, _, n_chunk, d = x.shape
    n_proj, n_fin = CHUNK // SPS_IN, CHUNK // SPS
    per_layer = lambda *shape: pl.BlockSpec((1,) + shape, lambda bi, st: (layer,) + (0,) * len(shape),
                                            pipeline_mode=pl.Buffered(1))
    x_spec = pl.BlockSpec((1, SPS_IN, n_chunk, d),
                          lambda bi, st: (jnp.where(st < n_proj, bi, jnp.minimum(bi + 1, b - 1)),
                                          jnp.where(st < n_proj, st, 0), 0, 0))
    o_spec = pl.BlockSpec((1, SPS, n_chunk, SSM_WIDTH), lambda bi, st: (bi, jnp.maximum(st - n_proj, 0), 0, 0))
    return pl.pallas_call(
        functools.partial(_ssm_kernel, n_chunk=n_chunk, n_scan=n_scan),
        out_shape=jax.ShapeDtypeStruct(x.shape[:3] + (SSM_WIDTH,), F32),
        grid=(b, n_proj + n_fin),
        in_specs=[x_spec,
                  per_layer(1, d), per_layer(2 * SSM_WIDTH, d),
                  per_layer(SSM_GROUPS, CK + 2 * SSM_STATE, CK),
                  per_layer(SSM_GROUPS // 2, 2, 2 * CK, LANES),
                  per_layer(*pwr.shape[1:]), per_layer(*pwi.shape[1:]),
                  per_layer(SSM_WIDTH, 1), per_layer(SSM_WIDTH, SSM_WIDTH), per_layer(SSM_WIDTH, 1)],
        out_specs=o_spec,
        scratch_shapes=[pltpu.VMEM((CHUNK, SSM_GROUPS, SSM_CH, n_chunk), BF16),
                        pltpu.VMEM((CHUNK, SSM_GROUPS, SSM_CH, n_chunk), F32),
                        pltpu.VMEM((CHUNK, SSM_WIDTH, n_chunk), F32),
                        pltpu.VMEM((PAIRS, 2, SCAN_BLOCK + n_chunk, LANES), F32),
                        pltpu.VMEM((PAIRS, 2, 2 * n_chunk // SCAN_BLOCK, LANES), F32),
                        pltpu.VMEM((d, SPS * n_chunk), BF16)],
        compiler_params=pltpu.CompilerParams(dimension_semantics=("arbitrary", "arbitrary"),
                                             vmem_limit_bytes=VMEM_LIMIT),
        name=f"ssm_layer{layer}",
    )(x, norm_g, w_in_bt, wtws, wcp, pwr, pwi, d_skip, w_glu_t, b_glu)


def kernel(x, mem, positions, norm_g, w_in, q_norm_g, k_norm_g, sinks, lam_re, lam_im, log_dt, b_re, b_im,
           c_re, c_im, d_skip, w_glu, b_glu, mem_norm_g, w_mem_kv, xq_norm_g, xk_norm_g, w_out):
    b, s, d = x.shape
    depth = w_in.shape[0]
    assert s % TQ == 0 and (s // CHUNK) % LANES == 0
    n_scan = (s // CHUNK - 1).bit_length()

    o_su = _OFF_AG + ATTN_WIDTH
    o_xq = o_su + 2 * SSM_WIDTH
    w_in_act = jnp.swapaxes(jnp.concatenate([w_in[:, :, :o_su], w_in[:, :, o_xq:]], axis=-1), 1, 2).astype(BF16)
    w_in_bt = jnp.swapaxes(w_in[:, :, o_su:o_xq], 1, 2).astype(BF16)
    w_out_ac = jnp.concatenate([w_out[:, :ATTN_WIDTH], w_out[:, ATTN_WIDTH + SSM_WIDTH:]], axis=1).astype(BF16)
    w_out_b = w_out[:, ATTN_WIDTH:ATTN_WIDTH + SSM_WIDTH].astype(BF16)
    w_glu_t = jnp.swapaxes(w_glu, 1, 2).astype(BF16)
    norm_g3 = norm_g.reshape(depth, 1, d)
    over_tokens = lambda g: jnp.broadcast_to(g[:, :, None], g.shape + (SUB,))
    gq = over_tokens(q_norm_g * (LOG2E / math.sqrt(HEAD_DIM)))
    gk = over_tokens(k_norm_g)
    gxq = over_tokens(xq_norm_g * (LOG2E / math.sqrt(X_HEAD_DIM)))
    d_skip3 = d_skip.reshape(depth, SSM_WIDTH, 1)
    b_glu3 = b_glu.reshape(depth, SSM_WIDTH, 1)

    n_chunk = s // CHUNK
    pos_blocks = positions.reshape(b, s // BLOCK, ROWS, CHUNK).swapaxes(2, 3).reshape(b, s)
    cos_t, sin_t = _rope_tables(pos_blocks)
    mk, mvt = _mem_kv(mem, mem_norm_g, w_mem_kv, xk_norm_g)
    wtws, wcp, pwr, pwi = _s5_prep(lam_re, lam_im, log_dt, b_re, b_im, c_re, c_im, n_scan)

    xp = x.reshape(b, n_chunk, CHUNK, d).swapaxes(1, 2)
    for layer in range(depth):
        ob = _ssm_layer(layer, xp, norm_g3, w_in_bt, wtws, wcp, pwr, pwi, d_skip3, w_glu_t, b_glu3, n_scan)
        xp = _ac_layer(layer, xp, ob, cos_t, sin_t, sinks, norm_g3, w_in_act, gq, gk, mk, mvt, gxq, w_out_ac,
                       w_out_b, natural_out=layer == depth - 1)
    return xp
```

```python
import functools
import math

import jax
import jax.numpy as jnp
from jax import lax
from jax.experimental import pallas as pl
from jax.experimental.pallas import tpu as pltpu

F32 = jnp.float32
BF16 = jnp.bfloat16

EPS = 1e-6
ROPE_THETA = 10000.0
NEG_INF = -1e30
LOG2E = math.log2(math.e)

HEAD_DIM = 64
N_Q_HEADS = 8
N_KV_HEADS = 2
GQA_GROUP = N_Q_HEADS // N_KV_HEADS
BLOCK = 128
ATTN_WIDTH = N_Q_HEADS * HEAD_DIM
KV_WIDTH = N_KV_HEADS * HEAD_DIM
SSM_CH = 16
SSM_GROUPS = 32
SSM_STATE = 64
SSM_WIDTH = SSM_GROUPS * SSM_CH
X_HEADS = 4
X_HEAD_DIM = 128
X_WIDTH = X_HEADS * X_HEAD_DIM

LANES = 128
CHUNK = 16
CK = CHUNK * SSM_CH
ROWS = BLOCK // CHUNK
TQ = 1024
SUB = 512
SPS = 4
SPS_IN = 4
SM_ROWS = 32
OUT_CHUNKS = 4
SCAN_BLOCK = 8
PAIRS = 8
VMEM_LIMIT = 56 * 1024 * 1024

_OFF_Q = 0
_OFF_K = _OFF_Q + ATTN_WIDTH
_OFF_V = _OFF_K + KV_WIDTH
_OFF_AG = _OFF_V + KV_WIDTH
_OFF_XQ = _OFF_AG + ATTN_WIDTH
_OFF_XG = _OFF_XQ + X_WIDTH
AC_WIDTH = _OFF_XG + X_WIDTH


def _sigmoid(v):
    return 1.0 / (1.0 + jnp.exp(-v))


def _silu(v):
    return v * _sigmoid(v)


def _rms_rows(v, gain):
    return v * lax.rsqrt(jnp.mean(v * v, axis=-1, keepdims=True) + EPS) * gain


def _dot_nt(a, b):
    return lax.dot_general(a, b, (((1,), (1,)), ((), ())), preferred_element_type=F32)


def _rope_kernel(pos_ref, inv_ref, cos_ref, sin_ref):
    ang = inv_ref[...] * pos_ref[0].astype(F32)
    c, s = jnp.cos(ang), jnp.sin(ang)
    cos_ref[0] = jnp.concatenate([c, c], axis=0)
    sin_ref[0] = jnp.concatenate([-s, s], axis=0)


def _rope_tables(positions):
    b, s = positions.shape
    half = HEAD_DIM // 2
    inv = (ROPE_THETA ** (-jnp.arange(half, dtype=F32) / half)).reshape(half, 1)
    ts = min(s, 2048)
    spec = pl.BlockSpec((1, HEAD_DIM, ts), lambda i, j: (i, 0, j))
    return pl.pallas_call(
        _rope_kernel,
        out_shape=(jax.ShapeDtypeStruct((b, HEAD_DIM, s), F32),) * 2,
        grid=(b, s // ts),
        in_specs=[pl.BlockSpec((1, 1, ts), lambda i, j: (i, 0, j)),
                  pl.BlockSpec((HEAD_DIM // 2, 1), lambda i, j: (0, 0))],
        out_specs=(spec, spec),
        name="rope_tables",
    )(positions.reshape(b, 1, s), inv)


def _memkv_kernel(mem_ref, g_ref, w_ref, gk_ref, mk_ref, mv_ref):
    nb, n_mem, d = mem_ref.shape
    h = _rms_rows(mem_ref[...].reshape(nb * n_mem, d), g_ref[0]).astype(BF16)
    kv = jnp.dot(h, w_ref[0], preferred_element_type=F32)
    for hd in range(X_HEADS):
        sl = slice(hd * X_HEAD_DIM, (hd + 1) * X_HEAD_DIM)
        mk_ref[0, :, :, sl] = _rms_rows(kv[:, sl], gk_ref[0]).astype(BF16).reshape(nb, n_mem, X_HEAD_DIM)
    for bi in range(nb):
        mv_ref[0, bi] = kv[bi * n_mem:(bi + 1) * n_mem, X_WIDTH:].T.astype(BF16)


def _mem_kv(mem, mem_norm_g, w_mem_kv, xk_norm_g):
    b, n_mem, d = mem.shape
    depth = w_mem_kv.shape[0]
    out = jax.ShapeDtypeStruct((depth, b, n_mem, X_WIDTH), BF16)
    ospec = pl.BlockSpec((1, b, n_mem, X_WIDTH), lambda l: (l, 0, 0, 0))
    out_t = jax.ShapeDtypeStruct((depth, b, X_WIDTH, n_mem), BF16)
    ospec_t = pl.BlockSpec((1, b, X_WIDTH, n_mem), lambda l: (l, 0, 0, 0))
    return pl.pallas_call(
        _memkv_kernel,
        out_shape=(out, out_t),
        grid=(depth,),
        in_specs=[pl.BlockSpec((b, n_mem, d), lambda l: (0, 0, 0)),
                  pl.BlockSpec((1, 1, d), lambda l: (l, 0, 0)),
                  pl.BlockSpec((1, d, 2 * X_WIDTH), lambda l: (l, 0, 0)),
                  pl.BlockSpec((1, 1, X_HEAD_DIM), lambda l: (l, 0, 0))],
        out_specs=(ospec, ospec_t),
        compiler_params=pltpu.CompilerParams(vmem_limit_bytes=VMEM_LIMIT),
        name="mem_kv",
    )(mem, mem_norm_g.reshape(depth, 1, d), w_mem_kv.astype(BF16), xk_norm_g.reshape(depth, 1, X_HEAD_DIM))


def _s5_prep_kernel(lr_ref, li_ref, ldt_ref, brt_ref, bit_ref, crt_ref, cit_ref,
                    wtws_ref, wcp_ref, pwr_ref, pwi_ref, *, n_scan):
    n_groups = lr_ref.shape[1]
    tab_row = lax.broadcasted_iota(jnp.int32, (3 * CHUNK, LANES), 0)
    n_tab = jnp.where(tab_row < 2 * CHUNK, tab_row, 3 * CHUNK - 1 - tab_row).astype(F32)
    scan_row = lax.broadcasted_iota(jnp.int32, (2 * SCAN_BLOCK, LANES), 0)
    n_scan_rows = jnp.where(scan_row < SCAN_BLOCK,
                            jnp.left_shift(jnp.int32(CHUNK), jnp.minimum(scan_row, n_scan - 1)),
                            CHUNK * (scan_row - SCAN_BLOCK + 1)).astype(F32)
    low = lax.broadcasted_iota(jnp.int32, (CK, LANES), 1) < SSM_STATE
    lane_s = lax.broadcasted_iota(jnp.int32, (CK, LANES), 1) // SSM_CH
    hi = lax.Precision.HIGHEST

    def per_chunk_row(tab, first):
        rows = tab[first:first + CHUNK]
        return jnp.broadcast_to(rows[:, None, :], (CHUNK, SSM_CH, LANES)).reshape(CK, LANES)

    low_rows = lax.broadcasted_iota(jnp.int32, pwr_ref.shape[2:], 1) < SSM_STATE

    def group(g, gp, e):
        own = low if e == 0 else jnp.logical_not(low)
        dt = jnp.exp(ldt_ref[0, g])
        lr, li = lr_ref[0, g], li_ref[0, g]

        def power(n):
            mag = jnp.exp(lr * dt * n)
            return mag * jnp.cos(li * dt * n), mag * jnp.sin(li * dt * n)

        tab_r, tab_i = power(n_tab)
        ar, ai = tab_r[1:2], tab_i[1:2]
        den = lr * lr + li * li
        fr = ((ar - 1.0) * lr + ai * li) / den
        fi = (ai * lr - (ar - 1.0) * li) / den
        every_chunk_row = lambda a: jnp.concatenate([a] * CHUNK, axis=0)
        brt, bit = every_chunk_row(brt_ref[0, g]), every_chunk_row(bit_ref[0, g])
        bbr = fr * brt - fi * bit
        bbi = fr * bit + fi * brt

        crt, cit = every_chunk_row(crt_ref[0, g]), every_chunk_row(cit_ref[0, g])

        def c_times_power(first):
            pr, pi = per_chunk_row(tab_r, first), per_chunk_row(tab_i, first)
            return crt * pr - cit * pi, -(crt * pi + cit * pr)

        g_re, g_mim = c_times_power(0)
        d = lax.dot_general(jnp.where(low, g_re, g_mim), jnp.where(low, bbr, bbi), (((1,), (1,)), ((), ())),
                            precision=hi, preferred_element_type=F32)
        per_tile = LANES // SSM_CH
        for tile in range(CK // LANES):
            d_t = d[:, tile * LANES:(tile + 1) * LANES]
            wt = jnp.zeros((CK, LANES), F32)
            for s in range(tile * per_tile, (tile + 1) * per_tile):
                shifted = d_t if s == 0 else jnp.concatenate(
                    [jnp.zeros((s * SSM_CH, LANES), F32), d_t[:CK - s * SSM_CH]], axis=0)
                wt = jnp.where(lane_s == s - tile * per_tile, shifted, wt)
            wtws_ref[0, g, 0:CK, tile * LANES:(tile + 1) * LANES] = wt.astype(BF16)

        qr, qi = per_chunk_row(tab_r, 2 * CHUNK), per_chunk_row(tab_i, 2 * CHUNK)
        ws = jnp.where(low, qr * bbr - qi * bbi, qr * bbi + qi * bbr)
        wtws_ref[0, g, CK:, :] = ws.T.astype(BF16)

        c_re, c_mim = c_times_power(1)
        wcp_ref[0, gp, 0, e * CK:(e + 1) * CK, :] = jnp.where(own, c_re, 0.0).astype(BF16)
        wcp_ref[0, gp, 1, e * CK:(e + 1) * CK, :] = jnp.where(own, c_mim, 0.0).astype(BF16)

        return power(n_scan_rows)

    def pair(gp, carry):
        (r0, i0), (r1, i1) = group(2 * gp, gp, 0), group(2 * gp + 1, gp, 1)
        pwr_ref[0, gp] = jnp.where(low_rows, r0, r1)
        pwi_ref[0, gp] = jnp.where(low_rows, i0, i1)
        return carry

    lax.fori_loop(0, n_groups // 2, pair, 0)


def _s5_prep(lam_re, lam_im, log_dt, b_re, b_im, c_re, c_im, n_scan):
    depth, g, p = lam_re.shape
    twice = lambda a: jnp.concatenate([a, a], axis=-1)
    row = lambda a: twice(a).reshape(depth, g, 1, 2 * p)
    b_t = lambda a: twice(jnp.swapaxes(a, 2, 3))
    c_t = twice
    assert n_scan <= SCAN_BLOCK
    n_rows = 2 * SCAN_BLOCK
    gb = 8
    spec = lambda *shape: pl.BlockSpec((1, gb) + shape, lambda l, i: (l, i, 0, 0))
    pair_spec = lambda *shape: pl.BlockSpec((1, gb // 2) + shape, lambda l, i: (l, i) + (0,) * len(shape))
    return pl.pallas_call(
        functools.partial(_s5_prep_kernel, n_scan=n_scan),
        out_shape=(jax.ShapeDtypeStruct((depth, g, CK + 2 * p, CK), BF16),
                   jax.ShapeDtypeStruct((depth, g // 2, 2, 2 * CK, 2 * p), BF16),
                   jax.ShapeDtypeStruct((depth, g // 2, n_rows, 2 * p), F32),
                   jax.ShapeDtypeStruct((depth, g // 2, n_rows, 2 * p), F32)),
        grid=(depth, g // gb),
        in_specs=[spec(1, 2 * p), spec(1, 2 * p), spec(1, 1),
                  spec(SSM_CH, 2 * p), spec(SSM_CH, 2 * p), spec(SSM_CH, 2 * p), spec(SSM_CH, 2 * p)],
        out_specs=(spec(CK + 2 * p, CK), pair_spec(2, 2 * CK, 2 * p), pair_spec(n_rows, 2 * p),
                   pair_spec(n_rows, 2 * p)),
        compiler_params=pltpu.CompilerParams(vmem_limit_bytes=VMEM_LIMIT),
        name="s5_prep",
    )(row(lam_re), row(lam_im), log_dt.reshape(depth, g, 1, 1),
      b_t(b_re), b_t(b_im), c_t(c_re), c_t(c_im))


def _col_rms(v):
    return lax.rsqrt(jnp.mean(v * v, axis=0, keepdims=True) + EPS)


def _ac_kernel(sinks_ref, x_ref, ob_ref, cos_ref, sin_ref, ng_ref, win_ref, gq_ref, gk_ref, mk_ref, mvt_ref,
               gxq_ref, wout_ref, woutb_ref, out_ref, k_ref, v_ref, shuf_ref, sc_ref, pt_ref, bias_ref, *sub_refs,
               layer, natural_out):
    i = pl.program_id(1)
    d_model = x_ref.shape[-1]
    half = HEAD_DIM // 2
    n_mem = mk_ref.shape[2]

    @pl.when(i == 0)
    def _():
        k_ref[0:BLOCK] = jnp.zeros((BLOCK, LANES), BF16)
        v_ref[:, 0:BLOCK] = jnp.zeros((KV_WIDTH, BLOCK), BF16)

    key_row = lax.broadcasted_iota(jnp.int32, (2 * BLOCK, BLOCK), 0)
    q_col = lax.broadcasted_iota(jnp.int32, (2 * BLOCK, BLOCK), 1)
    local_tok = lambda rho: (rho % ROWS) * CHUNK + rho // ROWS
    qi = local_tok(q_col)
    kj = (key_row // BLOCK) * BLOCK + local_tok(key_row % BLOCK)
    band = (kj >= qi + 1) & (kj <= qi + BLOCK)
    band_first = band & (kj >= jnp.where(i == 0, BLOCK, 0))
    bias_ref[0] = jnp.where(band_first, 0.0, NEG_INF)
    bias_ref[1] = jnp.where(band, 0.0, NEG_INF)
    ones_rows = jnp.ones((2 * ROWS, 2 * BLOCK), BF16)
    ones_mem = jnp.ones((2 * ROWS, n_mem), BF16)
    zeros_q = jnp.zeros((HEAD_DIM, GQA_GROUP * BLOCK), BF16)

    def tokens(blocks):
        return jnp.concatenate([x_ref[0, :, r * ROWS:(r + 1) * ROWS, :].reshape(BLOCK, d_model) for r in blocks],
                               axis=0)

    n_sub = TQ // SUB
    z_refs, q_refs, mix_refs, h_refs, ob_refs, mixt_refs = (sub_refs[k * n_sub:(k + 1) * n_sub] for k in range(6))

    def part(refs, rows, lanes):
        sub = lanes.start // SUB
        return refs[sub].at[rows, lanes.start - sub * SUB:lanes.stop - sub * SUB]

    lanes_of = lambda sub: slice(sub * SUB, (sub + 1) * SUB)
    blocks_of = lambda sub: range(sub * SUB // BLOCK, (sub + 1) * SUB // BLOCK)
    proj_rows = (slice(0, _OFF_AG), slice(_OFF_AG, _OFF_XQ), slice(_OFF_XQ, _OFF_XG), slice(_OFF_XG, AC_WIDTH))
    out_cols = tuple(slice(c * d_model // OUT_CHUNKS, (c + 1) * d_model // OUT_CHUNKS) for c in range(OUT_CHUNKS))

    def hidden(sub):
        h_refs[sub][...] = _rms_rows(tokens(blocks_of(sub)), ng_ref[0]).astype(BF16).T
        ob_refs[sub][...] = jnp.concatenate(
            [ob_ref[0, :, r * ROWS:(r + 1) * ROWS, :].reshape(BLOCK, SSM_WIDTH) for r in blocks_of(sub)],
            axis=0).astype(BF16)

    def project(sub, c):
        z_refs[sub][proj_rows[c], :] = jnp.dot(win_ref[0, proj_rows[c], :], h_refs[sub][...],
                                               preferred_element_type=F32)

    def qkv(sub):
        lanes = lanes_of(sub)
        cos_t, sin_t = cos_ref[0, :, lanes], sin_ref[0, :, lanes]

        def head_norm_rope(v, gain):
            vn = v * _col_rms(v) * gain
            return vn * cos_t + jnp.concatenate([vn[half:], vn[:half]], axis=0) * sin_t

        kt = jnp.concatenate(
            [head_norm_rope(z_refs[sub][_OFF_K + hk * HEAD_DIM:_OFF_K + (hk + 1) * HEAD_DIM, :], gk_ref[0])
             for hk in range(N_KV_HEADS)], axis=0)
        k_ref[BLOCK + sub * SUB:BLOCK + (sub + 1) * SUB] = kt.T.astype(BF16)
        v_ref[:, BLOCK + sub * SUB:BLOCK + (sub + 1) * SUB] = z_refs[sub][_OFF_V:_OFF_V + KV_WIDTH, :].astype(BF16)
        for hq in range(N_Q_HEADS):
            rows = slice(hq * HEAD_DIM, (hq + 1) * HEAD_DIM)
            q_refs[sub][rows, :] = head_norm_rope(z_refs[sub][rows, :], gq_ref[0]).astype(BF16)

    def column_softmax(slot, bias_idx, floor):
        n_rows, width = sc_ref.shape[1:]

        def chunk(c):
            rows = slice(c * SM_ROWS, (c + 1) * SM_ROWS)
            s = sc_ref[slot, rows, :]
            if bias_idx is not None:
                s = s + jnp.concatenate([bias_ref[bias_idx, rows, :]] * (width // BLOCK), axis=1)
            return rows, s

        m = None
        for c in range(n_rows // SM_ROWS):
            _, s = chunk(c)
            cm = jnp.max(s.reshape(SM_ROWS // 8, 8, width), axis=0)
            m = cm if m is None else jnp.maximum(m, cm)
        m = jnp.max(m, axis=0, keepdims=True)
        if floor is not None:
            m = jnp.maximum(m, floor)
        for c in range(n_rows // SM_ROWS):
            rows, s = chunk(c)
            pt_ref[slot, rows, :] = jnp.exp2(s - m).astype(BF16)
        return m

    def attend(r, hk, slot):
        cols = slice(r * BLOCK, (r + 1) * BLOCK)
        keys = k_ref[r * BLOCK:(r + 2) * BLOCK]
        heads = range(hk * GQA_GROUP, (hk + 1) * GQA_GROUP)
        qt = jnp.concatenate([part(q_refs, slice(hq * HEAD_DIM, (hq + 1) * HEAD_DIM), cols)[...] for hq in heads],
                             axis=1)
        qt = jnp.concatenate([qt, zeros_q] if hk == 0 else [zeros_q, qt], axis=0)
        sc_ref[slot] = jnp.dot(keys, qt, preferred_element_type=F32)
        sink = jnp.concatenate([jnp.full((1, BLOCK), sinks_ref[layer, hq] * LOG2E, F32) for hq in heads], axis=1)
        m = column_softmax(slot, 0 if r == 0 else 1, sink)
        return jnp.exp2(sink - m)

    def attend_finish(r, hk, slot, esink):
        cols = slice(r * BLOCK, (r + 1) * BLOCK)
        vals = jnp.concatenate(
            [v_ref[hk * HEAD_DIM:(hk + 1) * HEAD_DIM, r * BLOCK:(r + 2) * BLOCK], ones_rows], axis=0)
        o = jnp.dot(vals, pt_ref[slot], preferred_element_type=F32)
        o = o[:HEAD_DIM] * (1.0 / (o[HEAD_DIM:HEAD_DIM + 1] + esink))
        for g, hq in enumerate(range(hk * GQA_GROUP, (hk + 1) * GQA_GROUP)):
            rows = slice(hq * HEAD_DIM, (hq + 1) * HEAD_DIM)
            gate = part(z_refs, slice(_OFF_AG + hq * HEAD_DIM, _OFF_AG + (hq + 1) * HEAD_DIM), cols)[...]
            part(mix_refs, rows, cols)[...] = (o[:, g * BLOCK:(g + 1) * BLOCK] * _silu(gate)).astype(BF16)

    def cross(sub, hd, slot):
        rows = slice(hd * X_HEAD_DIM, (hd + 1) * X_HEAD_DIM)
        xq = z_refs[sub][_OFF_XQ + hd * X_HEAD_DIM:_OFF_XQ + (hd + 1) * X_HEAD_DIM, :]
        xq = (xq * _col_rms(xq) * gxq_ref[0]).astype(BF16)
        sc_ref[slot] = jnp.dot(mk_ref[0, 0, :, rows], xq, preferred_element_type=F32)
        column_softmax(slot, None, None)

    def cross_finish(sub, hd, slot, _):
        rows = slice(hd * X_HEAD_DIM, (hd + 1) * X_HEAD_DIM)
        vals = jnp.concatenate([mvt_ref[0, 0, rows, :], ones_mem], axis=0)
        o = jnp.dot(vals, pt_ref[slot], preferred_element_type=F32)
        o = o[:X_HEAD_DIM] * (1.0 / o[X_HEAD_DIM:X_HEAD_DIM + 1])
        gate = z_refs[sub][_OFF_XG + hd * X_HEAD_DIM:_OFF_XG + (hd + 1) * X_HEAD_DIM, :]
        mix_refs[sub][ATTN_WIDTH + hd * X_HEAD_DIM:ATTN_WIDTH + (hd + 1) * X_HEAD_DIM, :] = (
            o * _silu(gate)).astype(BF16)

    def output(sub, c):
        cols = out_cols[c]
        width = d_model // OUT_CHUNKS
        y = jnp.dot(mixt_refs[sub][...], wout_ref[0, :, cols], preferred_element_type=F32)
        y = y + jnp.dot(ob_refs[sub][...], woutb_ref[0, :, cols], preferred_element_type=F32)
        for n, r in enumerate(blocks_of(sub)):
            xr = x_ref[0, :, r * ROWS:(r + 1) * ROWS, cols]
            res = xr + y[n * BLOCK:(n + 1) * BLOCK].reshape(CHUNK, ROWS, width)
            if not natural_out:
                out_ref[0, :, r * ROWS:(r + 1) * ROWS, cols] = res
                continue
            res = res.reshape(BLOCK, width)
            for slab in range(width // LANES):
                shuf_ref[c, n, slab] = res[:, slab * LANES:(slab + 1) * LANES]
                lanes = slice(cols.start + slab * LANES, cols.start + (slab + 1) * LANES)
                for ch in range(ROWS):
                    out_ref[0, r * BLOCK + ch * CHUNK:r * BLOCK + (ch + 1) * CHUNK, lanes] = (
                        shuf_ref[c, n, slab, pl.ds(ch, CHUNK, stride=ROWS), :])

    hidden(0)
    project(0, 0)
    qkv(0)
    for c in range(1, len(proj_rows)):
        project(0, c)
    for sub in range(n_sub):
        units = [(attend, attend_finish, (r, hk)) for r in blocks_of(sub) for hk in range(N_KV_HEADS)]
        units += [(cross, cross_finish, (sub, hd)) for hd in range(X_HEADS)]
        carried = {}

        def stage(n):
            if n < len(units):
                carried[n] = units[n][0](*units[n][2], n % 2)
            if n >= 1:
                units[n - 1][1](*units[n - 1][2], (n - 1) % 2, carried.pop(n - 1))

        vec = [functools.partial(stage, n) for n in range(len(units) + 1)]
        mxu = []
        if sub + 1 < n_sub:
            hidden(sub + 1)
            mxu += [functools.partial(project, sub + 1, c) for c in range(len(proj_rows))]
        if sub >= 1:
            mixt_refs[sub - 1][...] = mix_refs[sub - 1][...].T
            mxu += [functools.partial(output, sub - 1, c) for c in range(OUT_CHUNKS)]
        every = -(-len(vec) // max(len(mxu), 1))
        for n, task in enumerate(vec):
            if n % every == 0 and mxu:
                mxu.pop(0)()
            task()
        for task in mxu:
            task()
        if sub + 1 < n_sub:
            qkv(sub + 1)
    mixt_refs[n_sub - 1][...] = mix_refs[n_sub - 1][...].T
    for c in range(OUT_CHUNKS):
        output(n_sub - 1, c)

    k_ref[0:BLOCK] = k_ref[TQ:TQ + BLOCK]
    v_ref[:, 0:BLOCK] = v_ref[:, TQ:TQ + BLOCK]


def _ac_layer(layer, x, ob, cos_t, sin_t, sinks, norm_g, w_in_act, gq, gk, mk, mvt, gxq, w_out_ac, w_out_b,
              natural_out):
    b, _, n_chunk, d = x.shape
    n_mem = mk.shape[2]
    n_sub = TQ // SUB
    assert n_mem == 2 * BLOCK and SUB == GQA_GROUP * BLOCK
    if natural_out:
        out_shape = jax.ShapeDtypeStruct((b, n_chunk * CHUNK, d), F32)
        out_spec = pl.BlockSpec((1, TQ, d), lambda bi, i: (bi, i, 0))
    else:
        out_shape = jax.ShapeDtypeStruct(x.shape, F32)
        out_spec = pl.BlockSpec((1, CHUNK, TQ // CHUNK, d), lambda bi, i: (bi, 0, i, 0))
    tab = pl.BlockSpec((1, HEAD_DIM, TQ), lambda bi, i: (bi, 0, i))
    xspec = pl.BlockSpec((1, CHUNK, TQ // CHUNK, d), lambda bi, i: (bi, 0, i, 0))
    obspec = pl.BlockSpec((1, CHUNK, TQ // CHUNK, SSM_WIDTH), lambda bi, i: (bi, 0, i, 0))
    per_layer = lambda *shape: pl.BlockSpec((1,) + shape, lambda bi, i: (layer,) + (0,) * len(shape),
                                            pipeline_mode=pl.Buffered(1))
    return pl.pallas_call(
        functools.partial(_ac_kernel, layer=layer, natural_out=natural_out),
        out_shape=out_shape,
        grid=(b, n_chunk * CHUNK // TQ),
        in_specs=[pl.BlockSpec(memory_space=pltpu.SMEM),
                  xspec, obspec, tab, tab,
                  per_layer(1, d), per_layer(AC_WIDTH, d), per_layer(HEAD_DIM, SUB), per_layer(HEAD_DIM, SUB),
                  pl.BlockSpec((1, 1, n_mem, X_WIDTH), lambda bi, i: (layer, bi, 0, 0)),
                  pl.BlockSpec((1, 1, X_WIDTH, n_mem), lambda bi, i: (layer, bi, 0, 0)),
                  per_layer(X_HEAD_DIM, SUB), per_layer(ATTN_WIDTH + X_WIDTH, d), per_layer(SSM_WIDTH, d)],
        out_specs=out_spec,
        scratch_shapes=([pltpu.VMEM((BLOCK + TQ, LANES), BF16),
                         pltpu.VMEM((KV_WIDTH, BLOCK + TQ), BF16),
                         pltpu.VMEM((OUT_CHUNKS, SUB // BLOCK, d // OUT_CHUNKS // LANES, BLOCK, LANES), F32),
                         pltpu.VMEM((2, 2 * BLOCK, SUB), F32),
                         pltpu.VMEM((2, 2 * BLOCK, SUB), BF16),
                         pltpu.VMEM((2, 2 * BLOCK, BLOCK), F32)]
                        + [pltpu.VMEM((AC_WIDTH, SUB), F32)] * n_sub
                        + [pltpu.VMEM((ATTN_WIDTH, SUB), BF16)] * n_sub
                        + [pltpu.VMEM((ATTN_WIDTH + X_WIDTH, SUB), BF16)] * n_sub
                        + [pltpu.VMEM((d, SUB), BF16)] * n_sub
                        + [pltpu.VMEM((SUB, SSM_WIDTH), BF16)] * n_sub
                        + [pltpu.VMEM((SUB, ATTN_WIDTH + X_WIDTH), BF16)] * n_sub),
        compiler_params=pltpu.CompilerParams(dimension_semantics=("arbitrary", "arbitrary"),
                                             vmem_limit_bytes=VMEM_LIMIT),
        name=f"attn_layer{layer}",
    )(sinks, x, ob, cos_t, sin_t, norm_g, w_in_act, gq, gk, mk, mvt, gxq, w_out_ac, w_out_b)


def _ssm_kernel(x_ref, ng_ref, winb_ref, wtws_ref, wcp_ref, pwr_ref, pwi_ref, dsk_ref, wglu_ref, bglu_ref,
                out_ref, z_ref, y_ref, sg_ref, hs_ref, es_ref, ht_ref, *, n_chunk, n_scan):
    step = pl.program_id(1)
    n_proj = CHUNK // SPS_IN
    d_model = x_ref.shape[-1]

    @pl.when(step < n_proj)
    def _project():
        for part in range(SPS_IN // SPS):
            x = x_ref[0, part * SPS:(part + 1) * SPS].reshape(SPS * n_chunk, d_model)
            ht_ref[...] = _rms_rows(x, ng_ref[0]).astype(BF16).T
            ut = jnp.dot(winb_ref[0], ht_ref[...], preferred_element_type=F32)
            for e in range(SPS):
                s = SPS_IN * step + part * SPS + e
                u = ut[:SSM_WIDTH, e * n_chunk:(e + 1) * n_chunk]
                z_ref[s] = u.astype(BF16).reshape(SSM_GROUPS, SSM_CH, n_chunk)
                y_ref[s] = (u * dsk_ref[0]).reshape(SSM_GROUPS, SSM_CH, n_chunk)
                sg_ref[s] = _silu(ut[SSM_WIDTH:, e * n_chunk:(e + 1) * n_chunk])

    @pl.when(step == n_proj - 1)
    def _chunks():
        n_blk = n_chunk // SCAN_BLOCK
        lvl1 = SCAN_BLOCK.bit_length() - 1
        hs_ref[:, :, 0:SCAN_BLOCK, :] = jnp.zeros((PAIRS, 2, SCAN_BLOCK, LANES), F32)
        es_ref[:, :, 0:n_blk, :] = jnp.zeros((PAIRS, 2, n_blk, LANES), F32)
        rows = pl.ds(SCAN_BLOCK, n_chunk)
        in_block = lax.broadcasted_iota(jnp.int32, (n_chunk, LANES), 0) % SCAN_BLOCK

        def mul_add(hr, hi, ar, ai, sr, si):
            return hr + ar * sr - ai * si, hi + ar * si + ai * sr

        def pairs(it, carry):
            gps = [PAIRS * it + k for k in range(PAIRS)]
            zs = [[z_ref[:, 2 * gp + e].reshape(CK, n_chunk) for e in range(2)] for gp in gps]
            power = lambda gp, j: (pwr_ref[0, gp, j:j + 1, :], pwi_ref[0, gp, j:j + 1, :])
            hr, hi = [], []
            for k, gp in enumerate(gps):
                s = [jnp.dot(wtws_ref[0, 2 * gp + e, CK:, :], zs[k][e], preferred_element_type=F32)
                     for e in range(2)]
                hr.append(jnp.concatenate([s[0][:SSM_STATE], s[1][:SSM_STATE]], axis=0).T)
                hi.append(jnp.concatenate([s[0][SSM_STATE:], s[1][SSM_STATE:]], axis=0).T)

            def put(k):
                hs_ref[k, 0, rows, :] = hr[k]
                hs_ref[k, 1, rows, :] = hi[k]

            for j in range(lvl1):
                for k in range(PAIRS):
                    put(k)
                for k, gp in enumerate(gps):
                    keep = in_block >= (1 << j)
                    sr = jnp.where(keep, hs_ref[k, 0, pl.ds(SCAN_BLOCK - (1 << j), n_chunk), :], 0.0)
                    si = jnp.where(keep, hs_ref[k, 1, pl.ds(SCAN_BLOCK - (1 << j), n_chunk), :], 0.0)
                    hr[k], hi[k] = mul_add(hr[k], hi[k], *power(gp, j), sr, si)
            for k in range(PAIRS):
                put(k)
            last = pl.ds(2 * SCAN_BLOCK - 1, n_blk, stride=SCAN_BLOCK)
            er = [hs_ref[k, 0, last, :] for k in range(PAIRS)]
            ei = [hs_ref[k, 1, last, :] for k in range(PAIRS)]
            blk = pl.ds(n_blk, n_blk)
            for j in range(lvl1, n_scan + 1):
                shift = (1 << (j - lvl1)) if j < n_scan else 1
                for k in range(PAIRS):
                    es_ref[k, 0, blk, :] = er[k]
                    es_ref[k, 1, blk, :] = ei[k]
                for k, gp in enumerate(gps):
                    sr, si = es_ref[k, 0, pl.ds(n_blk - shift, n_blk), :], es_ref[k, 1, pl.ds(n_blk - shift, n_blk), :]
                    if j < n_scan:
                        er[k], ei[k] = mul_add(er[k], ei[k], *power(gp, j), sr, si)
                    else:
                        spread = lambda v: jnp.broadcast_to(v[:, None, :], (n_blk, SCAN_BLOCK, LANES)).reshape(
                            n_chunk, LANES)
                        pr = jnp.concatenate([pwr_ref[0, gp, SCAN_BLOCK:2 * SCAN_BLOCK, :]] * n_blk, axis=0)
                        pi = jnp.concatenate([pwi_ref[0, gp, SCAN_BLOCK:2 * SCAN_BLOCK, :]] * n_blk, axis=0)
                        hr[k], hi[k] = mul_add(hr[k], hi[k], pr, pi, spread(sr), spread(si))
            for k in range(PAIRS):
                put(k)
            for k, gp in enumerate(gps):
                sr = hs_ref[k, 0, pl.ds(SCAN_BLOCK - 1, n_chunk), :]
                si = hs_ref[k, 1, pl.ds(SCAN_BLOCK - 1, n_chunk), :]
                yc = (_dot_nt(wcp_ref[0, gp, 0], sr.astype(BF16))
                      + _dot_nt(wcp_ref[0, gp, 1], si.astype(BF16)))
                for e in range(2):
                    y = jnp.dot(wtws_ref[0, 2 * gp + e, 0:CK, :], zs[k][e], preferred_element_type=F32)
                    y = y + yc[e * CK:(e + 1) * CK]
                    y_ref[:, 2 * gp + e] = y_ref[:, 2 * gp + e] + y.reshape(CHUNK, SSM_CH, n_chunk)
            return carry

        lax.fori_loop(0, SSM_GROUPS // (2 * PAIRS), pairs, 0)

    @pl.when(step >= n_proj)
    def _finish():
        t0 = SPS * (step - n_proj)
        y = jnp.concatenate([y_ref[t0 + e].reshape(SSM_WIDTH, n_chunk) for e in range(SPS)], axis=1)
        sg = jnp.concatenate([sg_ref[t0 + e] for e in range(SPS)], axis=1)
        y = jax.nn.gelu(y)
        gate = _sigmoid(jnp.dot(wglu_ref[0], y.astype(BF16), preferred_element_type=F32) + bglu_ref[0])
        out_ref[0] = (y * gate * sg).T.reshape(SPS, n_chunk, SSM_WIDTH)


def _ssm_layer(layer, x, norm_g, w_in_bt, wtws, wcp, pwr, pwi, d_skip, w_glu_t, b_glu, n_scan):
    b, _, n_chunk, d = x.shape
    n_proj, n_fin = CHUNK // SPS_IN, CHUNK // SPS
    per_layer = lambda *shape: pl.BlockSpec((1,) + shape, lambda bi, st: (layer,) + (0,) * len(shape),
                                            pipeline_mode=pl.Buffered(1))
    x_spec = pl.BlockSpec((1, SPS_IN, n_chunk, d),
                          lambda bi, st: (jnp.where(st < n_proj, bi, jnp.minimum(bi + 1, b - 1)),
                                          jnp.where(st < n_proj, st, 0), 0, 0))
    o_spec = pl.BlockSpec((1, SPS, n_chunk, SSM_WIDTH), lambda bi, st: (bi, jnp.maximum(st - n_proj, 0), 0, 0))
    return pl.pallas_call(
        functools.partial(_ssm_kernel, n_chunk=n_chunk, n_scan=n_scan),
        out_shape=jax.ShapeDtypeStruct(x.shape[:3] + (SSM_WIDTH,), F32),
        grid=(b, n_proj + n_fin),
        in_specs=[x_spec,
                  per_layer(1, d), per_layer(2 * SSM_WIDTH, d),
                  per_layer(SSM_GROUPS, CK + 2 * SSM_STATE, CK),
                  per_layer(SSM_GROUPS // 2, 2, 2 * CK, LANES),
                  per_layer(*pwr.shape[1:]), per_layer(*pwi.shape[1:]),
                  per_layer(SSM_WIDTH, 1), per_layer(SSM_WIDTH, SSM_WIDTH), per_layer(SSM_WIDTH, 1)],
        out_specs=o_spec,
        scratch_shapes=[pltpu.VMEM((CHUNK, SSM_GROUPS, SSM_CH, n_chunk), BF16),
                        pltpu.VMEM((CHUNK, SSM_GROUPS, SSM_CH, n_chunk), F32),
                        pltpu.VMEM((CHUNK, SSM_WIDTH, n_chunk), F32),
                        pltpu.VMEM((PAIRS, 2, SCAN_BLOCK + n_chunk, LANES), F32),
                        pltpu.VMEM((PAIRS, 2, 2 * n_chunk // SCAN_BLOCK, LANES), F32),
                        pltpu.VMEM((d, SPS * n_chunk), BF16)],
        compiler_params=pltpu.CompilerParams(dimension_semantics=("arbitrary", "arbitrary"),
                                             vmem_limit_bytes=VMEM_LIMIT),
        name=f"ssm_layer{layer}",
    )(x, norm_g, w_in_bt, wtws, wcp, pwr, pwi, d_skip, w_glu_t, b_glu)


def kernel(x, mem, positions, norm_g, w_in, q_norm_g, k_norm_g, sinks, lam_re, lam_im, log_dt, b_re, b_im,
           c_re, c_im, d_skip, w_glu, b_glu, mem_norm_g, w_mem_kv, xq_norm_g, xk_norm_g, w_out):
    b, s, d = x.shape
    depth = w_in.shape[0]
    assert s % TQ == 0 and (s // CHUNK) % LANES == 0
    n_scan = (s // CHUNK - 1).bit_length()

    o_su = _OFF_AG + ATTN_WIDTH
    o_xq = o_su + 2 * SSM_WIDTH
    w_in_act = jnp.swapaxes(jnp.concatenate([w_in[:, :, :o_su], w_in[:, :, o_xq:]], axis=-1), 1, 2).astype(BF16)
    w_in_bt = jnp.swapaxes(w_in[:, :, o_su:o_xq], 1, 2).astype(BF16)
    w_out_ac = jnp.concatenate([w_out[:, :ATTN_WIDTH], w_out[:, ATTN_WIDTH + SSM_WIDTH:]], axis=1).astype(BF16)
    w_out_b = w_out[:, ATTN_WIDTH:ATTN_WIDTH + SSM_WIDTH].astype(BF16)
    w_glu_t = jnp.swapaxes(w_glu, 1, 2).astype(BF16)
    norm_g3 = norm_g.reshape(depth, 1, d)
    over_tokens = lambda g: jnp.broadcast_to(g[:, :, None], g.shape + (SUB,))
    gq = over_tokens(q_norm_g * (LOG2E / math.sqrt(HEAD_DIM)))
    gk = over_tokens(k_norm_g)
    gxq = over_tokens(xq_norm_g * (LOG2E / math.sqrt(X_HEAD_DIM)))
    d_skip3 = d_skip.reshape(depth, SSM_WIDTH, 1)
    b_glu3 = b_glu.reshape(depth, SSM_WIDTH, 1)

    n_chunk = s // CHUNK
    pos_blocks = positions.reshape(b, s // BLOCK, ROWS, CHUNK).swapaxes(2, 3).reshape(b, s)
    cos_t, sin_t = _rope_tables(pos_blocks)
    mk, mvt = _mem_kv(mem, mem_norm_g, w_mem_kv, xk_norm_g)
    wtws, wcp, pwr, pwi = _s5_prep(lam_re, lam_im, log_dt, b_re, b_im, c_re, c_im, n_scan)

    xp = x.reshape(b, n_chunk, CHUNK, d).swapaxes(1, 2)
    for layer in range(depth):
        ob = _ssm_layer(layer, xp, norm_g3, w_in_bt, wtws, wcp, pwr, pwi, d_skip3, w_glu_t, b_glu3, n_scan)
        xp = _ac_layer(layer, xp, ob, cos_t, sin_t, sinks, norm_g3, w_in_act, gq, gk, mk, mvt, gxq, w_out_ac,
                       w_out_b, natural_out=layer == depth - 1)
    return xp
```

```python
import functools
import math

import jax
import jax.numpy as jnp
from jax import lax
from jax.experimental import pallas as pl
from jax.experimental.pallas import tpu as pltpu

F32 = jnp.float32
BF16 = jnp.bfloat16

EPS = 1e-6
ROPE_THETA = 10000.0
NEG_INF = -1e30
LOG2E = math.log2(math.e)

HEAD_DIM = 64
N_Q_HEADS = 8
N_KV_HEADS = 2
GQA_GROUP = N_Q_HEADS // N_KV_HEADS
BLOCK = 128
ATTN_WIDTH = N_Q_HEADS * HEAD_DIM
KV_WIDTH = N_KV_HEADS * HEAD_DIM
SSM_CH = 16
SSM_GROUPS = 32
SSM_STATE = 64
SSM_WIDTH = SSM_GROUPS * SSM_CH
X_HEADS = 4
X_HEAD_DIM = 128
X_WIDTH = X_HEADS * X_HEAD_DIM

LANES = 128
CHUNK = 16
CK = CHUNK * SSM_CH
ROWS = BLOCK // CHUNK
TQ = 1024
SUB = 512
SPS = 4
SPS_IN = 4
SM_ROWS = 32
OUT_CHUNKS = 4
SCAN_BLOCK = 8
PAIRS = 8
VMEM_LIMIT = 56 * 1024 * 1024

_OFF_Q = 0
_OFF_K = _OFF_Q + ATTN_WIDTH
_OFF_V = _OFF_K + KV_WIDTH
_OFF_AG = _OFF_V + KV_WIDTH
_OFF_XQ = _OFF_AG + ATTN_WIDTH
_OFF_XG = _OFF_XQ + X_WIDTH
AC_WIDTH = _OFF_XG + X_WIDTH


def _sigmoid(v):
    return 1.0 / (1.0 + jnp.exp(-v))


def _silu(v):
    return v * _sigmoid(v)


def _rms_rows(v, gain):
    return v * lax.rsqrt(jnp.mean(v * v, axis=-1, keepdims=True) + EPS) * gain


def _dot_nt(a, b):
    return lax.dot_general(a, b, (((1,), (1,)), ((), ())), preferred_element_type=F32)


def _rope_kernel(pos_ref, inv_ref, cos_ref, sin_ref):
    ang = inv_ref[...] * pos_ref[0].astype(F32)
    c, s = jnp.cos(ang), jnp.sin(ang)
    cos_ref[0] = jnp.concatenate([c, c], axis=0)
    sin_ref[0] = jnp.concatenate([-s, s], axis=0)


def _rope_tables(positions):
    b, s = positions.shape
    half = HEAD_DIM // 2
    inv = (ROPE_THETA ** (-jnp.arange(half, dtype=F32) / half)).reshape(half, 1)
    ts = min(s, 4096)
    spec = pl.BlockSpec((1, HEAD_DIM, ts), lambda i, j: (i, 0, j))
    return pl.pallas_call(
        _rope_kernel,
        out_shape=(jax.ShapeDtypeStruct((b, HEAD_DIM, s), F32),) * 2,
        grid=(b, s // ts),
        in_specs=[pl.BlockSpec((1, 1, ts), lambda i, j: (i, 0, j)),
                  pl.BlockSpec((HEAD_DIM // 2, 1), lambda i, j: (0, 0))],
        out_specs=(spec, spec),
        name="rope_tables",
    )(positions.reshape(b, 1, s), inv)


def _memkv_kernel(mem_ref, g_ref, w_ref, gk_ref, mk_ref, mv_ref):
    nb, n_mem, d = mem_ref.shape
    h = _rms_rows(mem_ref[...].reshape(nb * n_mem, d), g_ref[0]).astype(BF16)
    kv = jnp.dot(h, w_ref[0], preferred_element_type=F32)
    for hd in range(X_HEADS):
        sl = slice(hd * X_HEAD_DIM, (hd + 1) * X_HEAD_DIM)
        mk_ref[0, :, :, sl] = _rms_rows(kv[:, sl], gk_ref[0]).astype(BF16).reshape(nb, n_mem, X_HEAD_DIM)
    for bi in range(nb):
        mv_ref[0, bi] = kv[bi * n_mem:(bi + 1) * n_mem, X_WIDTH:].T.astype(BF16)


def _mem_kv(mem, mem_norm_g, w_mem_kv, xk_norm_g):
    b, n_mem, d = mem.shape
    depth = w_mem_kv.shape[0]
    out = jax.ShapeDtypeStruct((depth, b, n_mem, X_WIDTH), BF16)
    ospec = pl.BlockSpec((1, b, n_mem, X_WIDTH), lambda l: (l, 0, 0, 0))
    out_t = jax.ShapeDtypeStruct((depth, b, X_WIDTH, n_mem), BF16)
    ospec_t = pl.BlockSpec((1, b, X_WIDTH, n_mem), lambda l: (l, 0, 0, 0))
    return pl.pallas_call(
        _memkv_kernel,
        out_shape=(out, out_t),
        grid=(depth,),
        in_specs=[pl.BlockSpec((b, n_mem, d), lambda l: (0, 0, 0)),
                  pl.BlockSpec((1, 1, d), lambda l: (l, 0, 0)),
                  pl.BlockSpec((1, d, 2 * X_WIDTH), lambda l: (l, 0, 0)),
                  pl.BlockSpec((1, 1, X_HEAD_DIM), lambda l: (l, 0, 0))],
        out_specs=(ospec, ospec_t),
        compiler_params=pltpu.CompilerParams(vmem_limit_bytes=VMEM_LIMIT),
        name="mem_kv",
    )(mem, mem_norm_g.reshape(depth, 1, d), w_mem_kv.astype(BF16), xk_norm_g.reshape(depth, 1, X_HEAD_DIM))


def _s5_prep_kernel(lr_ref, li_ref, ldt_ref, brt_ref, bit_ref, crt_ref, cit_ref,
                    wtws_ref, wcp_ref, pwr_ref, pwi_ref, *, n_scan):
    n_groups = lr_ref.shape[1]
    tab_row = lax.broadcasted_iota(jnp.int32, (3 * CHUNK, LANES), 0)
    n_tab = jnp.where(tab_row < 2 * CHUNK, tab_row, 3 * CHUNK - 1 - tab_row).astype(F32)
    scan_row = lax.broadcasted_iota(jnp.int32, (2 * SCAN_BLOCK, LANES), 0)
    n_scan_rows = jnp.where(scan_row < SCAN_BLOCK,
                            jnp.left_shift(jnp.int32(CHUNK), jnp.minimum(scan_row, n_scan - 1)),
                            CHUNK * (scan_row - SCAN_BLOCK + 1)).astype(F32)
    low = lax.broadcasted_iota(jnp.int32, (CK, LANES), 1) < SSM_STATE
    lane_s = lax.broadcasted_iota(jnp.int32, (CK, LANES), 1) // SSM_CH
    hi = lax.Precision.HIGHEST

    def per_chunk_row(tab, first):
        rows = tab[first:first + CHUNK]
        return jnp.broadcast_to(rows[:, None, :], (CHUNK, SSM_CH, LANES)).reshape(CK, LANES)

    low_rows = lax.broadcasted_iota(jnp.int32, pwr_ref.shape[2:], 1) < SSM_STATE

    def group(g, gp, e):
        own = low if e == 0 else jnp.logical_not(low)
        dt = jnp.exp(ldt_ref[0, g])
        lr, li = lr_ref[0, g], li_ref[0, g]

        def power(n):
            mag = jnp.exp(lr * dt * n)
            return mag * jnp.cos(li * dt * n), mag * jnp.sin(li * dt * n)

        tab_r, tab_i = power(n_tab)
        ar, ai = tab_r[1:2], tab_i[1:2]
        den = lr * lr + li * li
        fr = ((ar - 1.0) * lr + ai * li) / den
        fi = (ai * lr - (ar - 1.0) * li) / den
        every_chunk_row = lambda a: jnp.concatenate([a] * CHUNK, axis=0)
        brt, bit = every_chunk_row(brt_ref[0, g]), every_chunk_row(bit_ref[0, g])
        bbr = fr * brt - fi * bit
        bbi = fr * bit + fi * brt

        crt, cit = every_chunk_row(crt_ref[0, g]), every_chunk_row(cit_ref[0, g])

        def c_times_power(first):
            pr, pi = per_chunk_row(tab_r, first), per_chunk_row(tab_i, first)
            return crt * pr - cit * pi, -(crt * pi + cit * pr)

        g_re, g_mim = c_times_power(0)
        d = lax.dot_general(jnp.where(low, g_re, g_mim), jnp.where(low, bbr, bbi), (((1,), (1,)), ((), ())),
                            precision=hi, preferred_element_type=F32)
        per_tile = LANES // SSM_CH
        for tile in range(CK // LANES):
            d_t = d[:, tile * LANES:(tile + 1) * LANES]
            wt = jnp.zeros((CK, LANES), F32)
            for s in range(tile * per_tile, (tile + 1) * per_tile):
                shifted = d_t if s == 0 else jnp.concatenate(
                    [jnp.zeros((s * SSM_CH, LANES), F32), d_t[:CK - s * SSM_CH]], axis=0)
                wt = jnp.where(lane_s == s - tile * per_tile, shifted, wt)
            wtws_ref[0, g, 0:CK, tile * LANES:(tile + 1) * LANES] = wt.astype(BF16)

        qr, qi = per_chunk_row(tab_r, 2 * CHUNK), per_chunk_row(tab_i, 2 * CHUNK)
        ws = jnp.where(low, qr * bbr - qi * bbi, qr * bbi + qi * bbr)
        wtws_ref[0, g, CK:, :] = ws.T.astype(BF16)

        c_re, c_mim = c_times_power(1)
        wcp_ref[0, gp, 0, e * CK:(e + 1) * CK, :] = jnp.where(own, c_re, 0.0).astype(BF16)
        wcp_ref[0, gp, 1, e * CK:(e + 1) * CK, :] = jnp.where(own, c_mim, 0.0).astype(BF16)

        return power(n_scan_rows)

    def pair(gp, carry):
        (r0, i0), (r1, i1) = group(2 * gp, gp, 0), group(2 * gp + 1, gp, 1)
        pwr_ref[0, gp] = jnp.where(low_rows, r0, r1)
        pwi_ref[0, gp] = jnp.where(low_rows, i0, i1)
        return carry

    lax.fori_loop(0, n_groups // 2, pair, 0)


def _s5_prep(lam_re, lam_im, log_dt, b_re, b_im, c_re, c_im, n_scan):
    depth, g, p = lam_re.shape
    twice = lambda a: jnp.concatenate([a, a], axis=-1)
    row = lambda a: twice(a).reshape(depth, g, 1, 2 * p)
    b_t = lambda a: twice(jnp.swapaxes(a, 2, 3))
    c_t = twice
    assert n_scan <= SCAN_BLOCK
    n_rows = 2 * SCAN_BLOCK
    gb = 16
    spec = lambda *shape: pl.BlockSpec((1, gb) + shape, lambda l, i: (l, i, 0, 0))
    pair_spec = lambda *shape: pl.BlockSpec((1, gb // 2) + shape, lambda l, i: (l, i) + (0,) * len(shape))
    return pl.pallas_call(
        functools.partial(_s5_prep_kernel, n_scan=n_scan),
        out_shape=(jax.ShapeDtypeStruct((depth, g, CK + 2 * p, CK), BF16),
                   jax.ShapeDtypeStruct((depth, g // 2, 2, 2 * CK, 2 * p), BF16),
                   jax.ShapeDtypeStruct((depth, g // 2, n_rows, 2 * p), F32),
                   jax.ShapeDtypeStruct((depth, g // 2, n_rows, 2 * p), F32)),
        grid=(depth, g // gb),
        in_specs=[spec(1, 2 * p), spec(1, 2 * p), spec(1, 1),
                  spec(SSM_CH, 2 * p), spec(SSM_CH, 2 * p), spec(SSM_CH, 2 * p), spec(SSM_CH, 2 * p)],
        out_specs=(spec(CK + 2 * p, CK), pair_spec(2, 2 * CK, 2 * p), pair_spec(n_rows, 2 * p),
                   pair_spec(n_rows, 2 * p)),
        compiler_params=pltpu.CompilerParams(vmem_limit_bytes=VMEM_LIMIT),
        name="s5_prep",
    )(row(lam_re), row(lam_im), log_dt.reshape(depth, g, 1, 1),
      b_t(b_re), b_t(b_im), c_t(c_re), c_t(c_im))


def _col_rms(v):
    return lax.rsqrt(jnp.mean(v * v, axis=0, keepdims=True) + EPS)


def _ac_kernel(sinks_ref, x_ref, ob_ref, cos_ref, sin_ref, ng_ref, win_ref, gq_ref, gk_ref, mk_ref, mvt_ref,
               gxq_ref, wout_ref, woutb_ref, out_ref, k_ref, v_ref, shuf_ref, sc_ref, pt_ref, bias_ref, *sub_refs,
               layer, natural_out):
    i = pl.program_id(1)
    d_model = x_ref.shape[-1]
    half = HEAD_DIM // 2
    n_mem = mk_ref.shape[2]

    @pl.when(i == 0)
    def _():
        k_ref[0:BLOCK] = jnp.zeros((BLOCK, LANES), BF16)
        v_ref[:, 0:BLOCK] = jnp.zeros((KV_WIDTH, BLOCK), BF16)

    key_row = lax.broadcasted_iota(jnp.int32, (2 * BLOCK, BLOCK), 0)
    q_col = lax.broadcasted_iota(jnp.int32, (2 * BLOCK, BLOCK), 1)
    local_tok = lambda rho: (rho % ROWS) * CHUNK + rho // ROWS
    qi = local_tok(q_col)
    kj = (key_row // BLOCK) * BLOCK + local_tok(key_row % BLOCK)
    band = (kj >= qi + 1) & (kj <= qi + BLOCK)
    band_first = band & (kj >= jnp.where(i == 0, BLOCK, 0))
    bias_ref[0] = jnp.where(band_first, 0.0, NEG_INF)
    bias_ref[1] = jnp.where(band, 0.0, NEG_INF)
    ones_rows = jnp.ones((2 * ROWS, 2 * BLOCK), BF16)
    ones_mem = jnp.ones((2 * ROWS, n_mem), BF16)
    zeros_q = jnp.zeros((HEAD_DIM, GQA_GROUP * BLOCK), BF16)

    def tokens(blocks):
        return jnp.concatenate([x_ref[0, :, r * ROWS:(r + 1) * ROWS, :].reshape(BLOCK, d_model) for r in blocks],
                               axis=0)

    n_sub = TQ // SUB
    z_refs, q_refs, mix_refs, h_refs, ob_refs, mixt_refs = (sub_refs[k * n_sub:(k + 1) * n_sub] for k in range(6))

    def part(refs, rows, lanes):
        sub = lanes.start // SUB
        return refs[sub].at[rows, lanes.start - sub * SUB:lanes.stop - sub * SUB]

    lanes_of = lambda sub: slice(sub * SUB, (sub + 1) * SUB)
    blocks_of = lambda sub: range(sub * SUB // BLOCK, (sub + 1) * SUB // BLOCK)
    proj_rows = (slice(0, _OFF_AG), slice(_OFF_AG, _OFF_XQ), slice(_OFF_XQ, _OFF_XG), slice(_OFF_XG, AC_WIDTH))
    out_cols = tuple(slice(c * d_model // OUT_CHUNKS, (c + 1) * d_model // OUT_CHUNKS) for c in range(OUT_CHUNKS))

    def hidden(sub):
        h_refs[sub][...] = _rms_rows(tokens(blocks_of(sub)), ng_ref[0]).astype(BF16).T
        ob_refs[sub][...] = jnp.concatenate(
            [ob_ref[0, :, r * ROWS:(r + 1) * ROWS, :].reshape(BLOCK, SSM_WIDTH) for r in blocks_of(sub)],
            axis=0).astype(BF16)

    def project(sub, c):
        z_refs[sub][proj_rows[c], :] = jnp.dot(win_ref[0, proj_rows[c], :], h_refs[sub][...],
                                               preferred_element_type=F32)

    def qkv(sub):
        lanes = lanes_of(sub)
        cos_t, sin_t = cos_ref[0, :, lanes], sin_ref[0, :, lanes]

        def head_norm_rope(v, gain):
            vn = v * _col_rms(v) * gain
            return vn * cos_t + jnp.concatenate([vn[half:], vn[:half]], axis=0) * sin_t

        kt = jnp.concatenate(
            [head_norm_rope(z_refs[sub][_OFF_K + hk * HEAD_DIM:_OFF_K + (hk + 1) * HEAD_DIM, :], gk_ref[0])
             for hk in range(N_KV_HEADS)], axis=0)
        k_ref[BLOCK + sub * SUB:BLOCK + (sub + 1) * SUB] = kt.T.astype(BF16)
        v_ref[:, BLOCK + sub * SUB:BLOCK + (sub + 1) * SUB] = z_refs[sub][_OFF_V:_OFF_V + KV_WIDTH, :].astype(BF16)
        for hq in range(N_Q_HEADS):
            rows = slice(hq * HEAD_DIM, (hq + 1) * HEAD_DIM)
            q_refs[sub][rows, :] = head_norm_rope(z_refs[sub][rows, :], gq_ref[0]).astype(BF16)

    def column_softmax(slot, bias_idx, floor):
        n_rows, width = sc_ref.shape[1:]

        def chunk(c):
            rows = slice(c * SM_ROWS, (c + 1) * SM_ROWS)
            s = sc_ref[slot, rows, :]
            if bias_idx is not None:
                s = s + jnp.concatenate([bias_ref[bias_idx, rows, :]] * (width // BLOCK), axis=1)
            return rows, s

        m = None
        for c in range(n_rows // SM_ROWS):
            _, s = chunk(c)
            cm = jnp.max(s.reshape(SM_ROWS // 8, 8, width), axis=0)
            m = cm if m is None else jnp.maximum(m, cm)
        m = jnp.max(m, axis=0, keepdims=True)
        if floor is not None:
            m = jnp.maximum(m, floor)
        for c in range(n_rows // SM_ROWS):
            rows, s = chunk(c)
            pt_ref[slot, rows, :] = jnp.exp2(s - m).astype(BF16)
        return m

    def attend(r, hk, slot):
        cols = slice(r * BLOCK, (r + 1) * BLOCK)
        keys = k_ref[r * BLOCK:(r + 2) * BLOCK]
        heads = range(hk * GQA_GROUP, (hk + 1) * GQA_GROUP)
        qt = jnp.concatenate([part(q_refs, slice(hq * HEAD_DIM, (hq + 1) * HEAD_DIM), cols)[...] for hq in heads],
                             axis=1)
        qt = jnp.concatenate([qt, zeros_q] if hk == 0 else [zeros_q, qt], axis=0)
        sc_ref[slot] = jnp.dot(keys, qt, preferred_element_type=F32)
        sink = jnp.concatenate([jnp.full((1, BLOCK), sinks_ref[layer, hq] * LOG2E, F32) for hq in heads], axis=1)
        m = column_softmax(slot, 0 if r == 0 else 1, sink)
        return jnp.exp2(sink - m)

    def attend_finish(r, hk, slot, esink):
        cols = slice(r * BLOCK, (r + 1) * BLOCK)
        vals = jnp.concatenate(
            [v_ref[hk * HEAD_DIM:(hk + 1) * HEAD_DIM, r * BLOCK:(r + 2) * BLOCK], ones_rows], axis=0)
        o = jnp.dot(vals, pt_ref[slot], preferred_element_type=F32)
        o = o[:HEAD_DIM] * (1.0 / (o[HEAD_DIM:HEAD_DIM + 1] + esink))
        for g, hq in enumerate(range(hk * GQA_GROUP, (hk + 1) * GQA_GROUP)):
            rows = slice(hq * HEAD_DIM, (hq + 1) * HEAD_DIM)
            gate = part(z_refs, slice(_OFF_AG + hq * HEAD_DIM, _OFF_AG + (hq + 1) * HEAD_DIM), cols)[...]
            part(mix_refs, rows, cols)[...] = (o[:, g * BLOCK:(g + 1) * BLOCK] * _silu(gate)).astype(BF16)

    def cross(sub, hd, slot):
        rows = slice(hd * X_HEAD_DIM, (hd + 1) * X_HEAD_DIM)
        xq = z_refs[sub][_OFF_XQ + hd * X_HEAD_DIM:_OFF_XQ + (hd + 1) * X_HEAD_DIM, :]
        xq = (xq * _col_rms(xq) * gxq_ref[0]).astype(BF16)
        sc_ref[slot] = jnp.dot(mk_ref[0, 0, :, rows], xq, preferred_element_type=F32)
        column_softmax(slot, None, None)

    def cross_finish(sub, hd, slot, _):
        rows = slice(hd * X_HEAD_DIM, (hd + 1) * X_HEAD_DIM)
        vals = jnp.concatenate([mvt_ref[0, 0, rows, :], ones_mem], axis=0)
        o = jnp.dot(vals, pt_ref[slot], preferred_element_type=F32)
        o = o[:X_HEAD_DIM] * (1.0 / o[X_HEAD_DIM:X_HEAD_DIM + 1])
        gate = z_refs[sub][_OFF_XG + hd * X_HEAD_DIM:_OFF_XG + (hd + 1) * X_HEAD_DIM, :]
        mix_refs[sub][ATTN_WIDTH + hd * X_HEAD_DIM:ATTN_WIDTH + (hd + 1) * X_HEAD_DIM, :] = (
            o * _silu(gate)).astype(BF16)

    def output(sub, c):
        cols = out_cols[c]
        width = d_model // OUT_CHUNKS
        y = jnp.dot(mixt_refs[sub][...], wout_ref[0, :, cols], preferred_element_type=F32)
        y = y + jnp.dot(ob_refs[sub][...], woutb_ref[0, :, cols], preferred_element_type=F32)
        for n, r in enumerate(blocks_of(sub)):
            xr = x_ref[0, :, r * ROWS:(r + 1) * ROWS, cols]
            res = xr + y[n * BLOCK:(n + 1) * BLOCK].reshape(CHUNK, ROWS, width)
            if not natural_out:
                out_ref[0, :, r * ROWS:(r + 1) * ROWS, cols] = res
                continue
            res = res.reshape(BLOCK, width)
            for slab in range(width // LANES):
                shuf_ref[c, n, slab] = res[:, slab * LANES:(slab + 1) * LANES]
                lanes = slice(cols.start + slab * LANES, cols.start + (slab + 1) * LANES)
                for ch in range(ROWS):
                    out_ref[0, r * BLOCK + ch * CHUNK:r * BLOCK + (ch + 1) * CHUNK, lanes] = (
                        shuf_ref[c, n, slab, pl.ds(ch, CHUNK, stride=ROWS), :])

    hidden(0)
    project(0, 0)
    qkv(0)
    for c in range(1, len(proj_rows)):
        project(0, c)
    for sub in range(n_sub):
        units = [(attend, attend_finish, (r, hk)) for r in blocks_of(sub) for hk in range(N_KV_HEADS)]
        units += [(cross, cross_finish, (sub, hd)) for hd in range(X_HEADS)]
        carried = {}

        def stage(n):
            if n < len(units):
                carried[n] = units[n][0](*units[n][2], n % 2)
            if n >= 1:
                units[n - 1][1](*units[n - 1][2], (n - 1) % 2, carried.pop(n - 1))

        vec = [functools.partial(stage, n) for n in range(len(units) + 1)]
        mxu = []
        if sub + 1 < n_sub:
            hidden(sub + 1)
            mxu += [functools.partial(project, sub + 1, c) for c in range(len(proj_rows))]
        if sub >= 1:
            mixt_refs[sub - 1][...] = mix_refs[sub - 1][...].T
            mxu += [functools.partial(output, sub - 1, c) for c in range(OUT_CHUNKS)]
        every = -(-len(vec) // max(len(mxu), 1))
        for n, task in enumerate(vec):
            if n % every == 0 and mxu:
                mxu.pop(0)()
            task()
        for task in mxu:
            task()
        if sub + 1 < n_sub:
            qkv(sub + 1)
    mixt_refs[n_sub - 1][...] = mix_refs[n_sub - 1][...].T
    for c in range(OUT_CHUNKS):
        output(n_sub - 1, c)

    k_ref[0:BLOCK] = k_ref[TQ:TQ + BLOCK]
    v_ref[:, 0:BLOCK] = v_ref[:, TQ:TQ + BLOCK]


def _ac_layer(layer, x, ob, cos_t, sin_t, sinks, norm_g, w_in_act, gq, gk, mk, mvt, gxq, w_out_ac, w_out_b,
              natural_out):
    b, _, n_chunk, d = x.shape
    n_mem = mk.shape[2]
    n_sub = TQ // SUB
    assert n_mem == 2 * BLOCK and SUB == GQA_GROUP * BLOCK
    if natural_out:
        out_shape = jax.ShapeDtypeStruct((b, n_chunk * CHUNK, d), F32)
        out_spec = pl.BlockSpec((1, TQ, d), lambda bi, i: (bi, i, 0))
    else:
        out_shape = jax.ShapeDtypeStruct(x.shape, F32)
        out_spec = pl.BlockSpec((1, CHUNK, TQ // CHUNK, d), lambda bi, i: (bi, 0, i, 0))
    tab = pl.BlockSpec((1, HEAD_DIM, TQ), lambda bi, i: (bi, 0, i))
    xspec = pl.BlockSpec((1, CHUNK, TQ // CHUNK, d), lambda bi, i: (bi, 0, i, 0))
    obspec = pl.BlockSpec((1, CHUNK, TQ // CHUNK, SSM_WIDTH), lambda bi, i: (bi, 0, i, 0))
    per_layer = lambda *shape: pl.BlockSpec((1,) + shape, lambda bi, i: (layer,) + (0,) * len(shape),
                                            pipeline_mode=pl.Buffered(1))
    return pl.pallas_call(
        functools.partial(_ac_kernel, layer=layer, natural_out=natural_out),
        out_shape=out_shape,
        grid=(b, n_chunk * CHUNK // TQ),
        in_specs=[pl.BlockSpec(memory_space=pltpu.SMEM),
                  xspec, obspec, tab, tab,
                  per_layer(1, d), per_layer(AC_WIDTH, d), per_layer(HEAD_DIM, SUB), per_layer(HEAD_DIM, SUB),
                  pl.BlockSpec((1, 1, n_mem, X_WIDTH), lambda bi, i: (layer, bi, 0, 0)),
                  pl.BlockSpec((1, 1, X_WIDTH, n_mem), lambda bi, i: (layer, bi, 0, 0)),
                  per_layer(X_HEAD_DIM, SUB), per_layer(ATTN_WIDTH + X_WIDTH, d), per_layer(SSM_WIDTH, d)],
        out_specs=out_spec,
        scratch_shapes=([pltpu.VMEM((BLOCK + TQ, LANES), BF16),
                         pltpu.VMEM((KV_WIDTH, BLOCK + TQ), BF16),
                         pltpu.VMEM((OUT_CHUNKS, SUB // BLOCK, d // OUT_CHUNKS // LANES, BLOCK, LANES), F32),
                         pltpu.VMEM((2, 2 * BLOCK, SUB), F32),
                         pltpu.VMEM((2, 2 * BLOCK, SUB), BF16),
                         pltpu.VMEM((2, 2 * BLOCK, BLOCK), F32)]
                        + [pltpu.VMEM((AC_WIDTH, SUB), F32)] * n_sub
                        + [pltpu.VMEM((ATTN_WIDTH, SUB), BF16)] * n_sub
                        + [pltpu.VMEM((ATTN_WIDTH + X_WIDTH, SUB), BF16)] * n_sub
                        + [pltpu.VMEM((d, SUB), BF16)] * n_sub
                        + [pltpu.VMEM((SUB, SSM_WIDTH), BF16)] * n_sub
                        + [pltpu.VMEM((SUB, ATTN_WIDTH + X_WIDTH), BF16)] * n_sub),
        compiler_params=pltpu.CompilerParams(dimension_semantics=("arbitrary", "arbitrary"),
                                             vmem_limit_bytes=VMEM_LIMIT),
        name=f"attn_layer{layer}",
    )(sinks, x, ob, cos_t, sin_t, norm_g, w_in_act, gq, gk, mk, mvt, gxq, w_out_ac, w_out_b)


def _ssm_kernel(x_ref, ng_ref, winb_ref, wtws_ref, wcp_ref, pwr_ref, pwi_ref, dsk_ref, wglu_ref, bglu_ref,
                out_ref, z_ref, y_ref, sg_ref, hs_ref, es_ref, ht_ref, *, n_chunk, n_scan):
    step = pl.program_id(1)
    n_proj = CHUNK // SPS_IN
    d_model = x_ref.shape[-1]

    @pl.when(step < n_proj)
    def _project():
        for part in range(SPS_IN // SPS):
            x = x_ref[0, part * SPS:(part + 1) * SPS].reshape(SPS * n_chunk, d_model)
            ht_ref[...] = _rms_rows(x, ng_ref[0]).astype(BF16).T
            ut = jnp.dot(winb_ref[0], ht_ref[...], preferred_element_type=F32)
            for e in range(SPS):
                s = SPS_IN * step + part * SPS + e
                u = ut[:SSM_WIDTH, e * n_chunk:(e + 1) * n_chunk]
                z_ref[s] = u.astype(BF16).reshape(SSM_GROUPS, SSM_CH, n_chunk)
                y_ref[s] = (u * dsk_ref[0]).reshape(SSM_GROUPS, SSM_CH, n_chunk)
                sg_ref[s] = _silu(ut[SSM_WIDTH:, e * n_chunk:(e + 1) * n_chunk])

    @pl.when(step == n_proj - 1)
    def _chunks():
        n_blk = n_chunk // SCAN_BLOCK
        lvl1 = SCAN_BLOCK.bit_length() - 1
        hs_ref[:, :, 0:SCAN_BLOCK, :] = jnp.zeros((PAIRS, 2, SCAN_BLOCK, LANES), F32)
        es_ref[:, :, 0:n_blk, :] = jnp.zeros((PAIRS, 2, n_blk, LANES), F32)
        rows = pl.ds(SCAN_BLOCK, n_chunk)
        in_block = lax.broadcasted_iota(jnp.int32, (n_chunk, LANES), 0) % SCAN_BLOCK

        def mul_add(hr, hi, ar, ai, sr, si):
            return hr + ar * sr - ai * si, hi + ar * si + ai * sr

        def pairs(it, carry):
            gps = [PAIRS * it + k for k in range(PAIRS)]
            zs = [[z_ref[:, 2 * gp + e].reshape(CK, n_chunk) for e in range(2)] for gp in gps]
            power = lambda gp, j: (pwr_ref[0, gp, j:j + 1, :], pwi_ref[0, gp, j:j + 1, :])
            hr, hi = [], []
            for k, gp in enumerate(gps):
                s = [jnp.dot(wtws_ref[0, 2 * gp + e, CK:, :], zs[k][e], preferred_element_type=F32)
                     for e in range(2)]
                hr.append(jnp.concatenate([s[0][:SSM_STATE], s[1][:SSM_STATE]], axis=0).T)
                hi.append(jnp.concatenate([s[0][SSM_STATE:], s[1][SSM_STATE:]], axis=0).T)

            def put(k):
                hs_ref[k, 0, rows, :] = hr[k]
                hs_ref[k, 1, rows, :] = hi[k]

            for j in range(lvl1):
                for k in range(PAIRS):
                    put(k)
                for k, gp in enumerate(gps):
                    keep = in_block >= (1 << j)
                    sr = jnp.where(keep, hs_ref[k, 0, pl.ds(SCAN_BLOCK - (1 << j), n_chunk), :], 0.0)
                    si = jnp.where(keep, hs_ref[k, 1, pl.ds(SCAN_BLOCK - (1 << j), n_chunk), :], 0.0)
                    hr[k], hi[k] = mul_add(hr[k], hi[k], *power(gp, j), sr, si)
            for k in range(PAIRS):
                put(k)
            last = pl.ds(2 * SCAN_BLOCK - 1, n_blk, stride=SCAN_BLOCK)
            er = [hs_ref[k, 0, last, :] for k in range(PAIRS)]
            ei = [hs_ref[k, 1, last, :] for k in range(PAIRS)]
            blk = pl.ds(n_blk, n_blk)
            for j in range(lvl1, n_scan + 1):
                shift = (1 << (j - lvl1)) if j < n_scan else 1
                for k in range(PAIRS):
                    es_ref[k, 0, blk, :] = er[k]
                    es_ref[k, 1, blk, :] = ei[k]
                for k, gp in enumerate(gps):
                    sr, si = es_ref[k, 0, pl.ds(n_blk - shift, n_blk), :], es_ref[k, 1, pl.ds(n_blk - shift, n_blk), :]
                    if j < n_scan:
                        er[k], ei[k] = mul_add(er[k], ei[k], *power(gp, j), sr, si)
                    else:
                        spread = lambda v: jnp.broadcast_to(v[:, None, :], (n_blk, SCAN_BLOCK, LANES)).reshape(
                            n_chunk, LANES)
                        pr = jnp.concatenate([pwr_ref[0, gp, SCAN_BLOCK:2 * SCAN_BLOCK, :]] * n_blk, axis=0)
                        pi = jnp.concatenate([pwi_ref[0, gp, SCAN_BLOCK:2 * SCAN_BLOCK, :]] * n_blk, axis=0)
                        hr[k], hi[k] = mul_add(hr[k], hi[k], pr, pi, spread(sr), spread(si))
            for k in range(PAIRS):
                put(k)
            for k, gp in enumerate(gps):
                sr = hs_ref[k, 0, pl.ds(SCAN_BLOCK - 1, n_chunk), :]
                si = hs_ref[k, 1, pl.ds(SCAN_BLOCK - 1, n_chunk), :]
                yc = (_dot_nt(wcp_ref[0, gp, 0], sr.astype(BF16))
                      + _dot_nt(wcp_ref[0, gp, 1], si.astype(BF16)))
                for e in range(2):
                    y = jnp.dot(wtws_ref[0, 2 * gp + e, 0:CK, :], zs[k][e], preferred_element_type=F32)
                    y = y + yc[e * CK:(e + 1) * CK]
                    y_ref[:, 2 * gp + e] = y_ref[:, 2 * gp + e] + y.reshape(CHUNK, SSM_CH, n_chunk)
            return carry

        lax.fori_loop(0, SSM_GROUPS // (2 * PAIRS), pairs, 0)

    @pl.when(step >= n_proj)
    def _finish():
        t0 = SPS * (step - n_proj)
        y = jnp.concatenate([y_ref[t0 + e].reshape(SSM_WIDTH, n_chunk) for e in range(SPS)], axis=1)
        sg = jnp.concatenate([sg_ref[t0 + e] for e in range(SPS)], axis=1)
        y = jax.nn.gelu(y)
        gate = _sigmoid(jnp.dot(wglu_ref[0], y.astype(BF16), preferred_element_type=F32) + bglu_ref[0])
        out_ref[0] = (y * gate * sg).T.reshape(SPS, n_chunk, SSM_WIDTH)


def _ssm_layer(layer, x, norm_g, w_in_bt, wtws, wcp, pwr, pwi, d_skip, w_glu_t, b_glu, n_scan):
    b, _, n_chunk, d = x.shape
    n_proj, n_fin = CHUNK // SPS_IN, CHUNK // SPS
    per_layer = lambda *shape: pl.BlockSpec((1,) + shape, lambda bi, st: (layer,) + (0,) * len(shape),
                                            pipeline_mode=pl.Buffered(1))
    x_spec = pl.BlockSpec((1, SPS_IN, n_chunk, d),
                          lambda bi, st: (jnp.where(st < n_proj, bi, jnp.minimum(bi + 1, b - 1)),
                                          jnp.where(st < n_proj, st, 0), 0, 0))
    o_spec = pl.BlockSpec((1, SPS, n_chunk, SSM_WIDTH), lambda bi, st: (bi, jnp.maximum(st - n_proj, 0), 0, 0))
    return pl.pallas_call(
        functools.partial(_ssm_kernel, n_chunk=n_chunk, n_scan=n_scan),
        out_shape=jax.ShapeDtypeStruct(x.shape[:3] + (SSM_WIDTH,), F32),
        grid=(b, n_proj + n_fin),
        in_specs=[x_spec,
                  per_layer(1, d), per_layer(2 * SSM_WIDTH, d),
                  per_layer(SSM_GROUPS, CK + 2 * SSM_STATE, CK),
                  per_layer(SSM_GROUPS // 2, 2, 2 * CK, LANES),
                  per_layer(*pwr.shape[1:]), per_layer(*pwi.shape[1:]),
                  per_layer(SSM_WIDTH, 1), per_layer(SSM_WIDTH, SSM_WIDTH), per_layer(SSM_WIDTH, 1)],
        out_specs=o_spec,
        scratch_shapes=[pltpu.VMEM((CHUNK, SSM_GROUPS, SSM_CH, n_chunk), BF16),
                        pltpu.VMEM((CHUNK, SSM_GROUPS, SSM_CH, n_chunk), F32),
                        pltpu.VMEM((CHUNK, SSM_WIDTH, n_chunk), F32),
                        pltpu.VMEM((PAIRS, 2, SCAN_BLOCK + n_chunk, LANES), F32),
                        pltpu.VMEM((PAIRS, 2, 2 * n_chunk // SCAN_BLOCK, LANES), F32),
                        pltpu.VMEM((d, SPS * n_chunk), BF16)],
        compiler_params=pltpu.CompilerParams(dimension_semantics=("arbitrary", "arbitrary"),
                                             vmem_limit_bytes=VMEM_LIMIT),
        name=f"ssm_layer{layer}",
    )(x, norm_g, w_in_bt, wtws, wcp, pwr, pwi, d_skip, w_glu_t, b_glu)


def kernel(x, mem, positions, norm_g, w_in, q_norm_g, k_norm_g, sinks, lam_re, lam_im, log_dt, b_re, b_im,
           c_re, c_im, d_skip, w_glu, b_glu, mem_norm_g, w_mem_kv, xq_norm_g, xk_norm_g, w_out):
    b, s, d = x.shape
    depth = w_in.shape[0]
    assert s % TQ == 0 and (s // CHUNK) % LANES == 0
    n_scan = (s // CHUNK - 1).bit_length()

    o_su = _OFF_AG + ATTN_WIDTH
    o_xq = o_su + 2 * SSM_WIDTH
    w_in_act = jnp.swapaxes(jnp.concatenate([w_in[:, :, :o_su], w_in[:, :, o_xq:]], axis=-1), 1, 2).astype(BF16)
    w_in_bt = jnp.swapaxes(w_in[:, :, o_su:o_xq], 1, 2).astype(BF16)
    w_out_ac = jnp.concatenate([w_out[:, :ATTN_WIDTH], w_out[:, ATTN_WIDTH + SSM_WIDTH:]], axis=1).astype(BF16)
    w_out_b = w_out[:, ATTN_WIDTH:ATTN_WIDTH + SSM_WIDTH].astype(BF16)
    w_glu_t = jnp.swapaxes(w_glu, 1, 2).astype(BF16)
    norm_g3 = norm_g.reshape(depth, 1, d)
    over_tokens = lambda g: jnp.broadcast_to(g[:, :, None], g.shape + (SUB,))
    gq = over_tokens(q_norm_g * (LOG2E / math.sqrt(HEAD_DIM)))
    gk = over_tokens(k_norm_g)
    gxq = over_tokens(xq_norm_g * (LOG2E / math.sqrt(X_HEAD_DIM)))
    d_skip3 = d_skip.reshape(depth, SSM_WIDTH, 1)
    b_glu3 = b_glu.reshape(depth, SSM_WIDTH, 1)

    n_chunk = s // CHUNK
    pos_blocks = positions.reshape(b, s // BLOCK, ROWS, CHUNK).swapaxes(2, 3).reshape(b, s)
    cos_t, sin_t = _rope_tables(pos_blocks)
    mk, mvt = _mem_kv(mem, mem_norm_g, w_mem_kv, xk_norm_g)
    wtws, wcp, pwr, pwi = _s5_prep(lam_re, lam_im, log_dt, b_re, b_im, c_re, c_im, n_scan)

    xp = x.reshape(b, n_chunk, CHUNK, d).swapaxes(1, 2)
    for layer in range(depth):
        ob = _ssm_layer(layer, xp, norm_g3, w_in_bt, wtws, wcp, pwr, pwi, d_skip3, w_glu_t, b_glu3, n_scan)
        xp = _ac_layer(layer, xp, ob, cos_t, sin_t, sinks, norm_g3, w_in_act, gq, gk, mk, mvt, gxq, w_out_ac,
                       w_out_b, natural_out=layer == depth - 1)
    return xp
```
